```python
import jax
import jax.numpy as jnp
from jax import lax
import numpy as np

D_MODEL = 1024
BATCH = 8
SEQ = 2048
DEPTH = 4

N_MIXERS = 3
N_HEADS = 16
HEAD_DIM = 64
D_FF = 4 * D_MODEL
ROPE_THETA = 10000.0
NORM_EPS = 1e-6
NEG_INF = -1e30
POS_BIG = 1e30

QKV_COLS = 3 * N_HEADS * HEAD_DIM

MOBA_BLOCK = 256
MOBA_TOPK = 3
MOBA_Q_CHUNK = 16

SB_Q_BLOCK = 128

NSA_KV_GROUPS = 4
NSA_HEADS_PER_GROUP = N_HEADS // NSA_KV_GROUPS
NSA_CMP_BLOCK = 32
NSA_CMP_STRIDE = 16
NSA_CMP_HIDDEN = 256
NSA_SLC_BLOCK = 64
NSA_SLC_TOPN = 16
NSA_SLC_Q_CHUNK = 32
NSA_WINDOW = 512
NSA_WIN_Q_BLOCK = 128
NSA_KV_COLS = NSA_KV_GROUPS * HEAD_DIM
NSA_IN_COLS = N_HEADS * HEAD_DIM + 6 * NSA_KV_COLS + 3 * N_HEADS

kernel_name = 'hybrid_moba_stickbreak_nsa_trunk'


def rms_norm(x, g):
    xf = x.astype(jnp.float32)
    y = xf * lax.rsqrt(jnp.mean(xf * xf, axis=-1, keepdims=True) + NORM_EPS)
    return (y * g.astype(jnp.float32)).astype(x.dtype)


def rope_tables(seq_len, dtype):
    inv_freq = 1.0 / (ROPE_THETA ** (jnp.arange(0, HEAD_DIM, 2, dtype=jnp.float32) / HEAD_DIM))
    ang = jnp.arange(seq_len, dtype=jnp.float32)[:, None] * inv_freq[None, :]
    return jnp.cos(ang).astype(dtype), jnp.sin(ang).astype(dtype)


def apply_rope(t, cos, sin):
    t1, t2 = jnp.split(t, 2, axis=-1)
    return jnp.concatenate([t1 * cos - t2 * sin, t1 * sin + t2 * cos], axis=-1)


def to_heads(t, n):
    B, S, _ = t.shape
    return t.reshape(B, S, n, HEAD_DIM).transpose(0, 2, 1, 3)


def merge_heads(o):
    B, H, S, Dh = o.shape
    return o.transpose(0, 2, 1, 3).reshape(B, S, H * Dh)


def gather_blocks(blocks, idx):
    return jax.vmap(jax.vmap(lambda b_, i_: b_[i_]))(blocks, idx)


def moba_attention(q, k, v):
    B, H, S, Dh = q.shape
    scale = Dh ** -0.5
    nb = -(-S // MOBA_BLOCK)
    pad = nb * MOBA_BLOCK - S
    kb = jnp.pad(k, ((0, 0), (0, 0), (0, pad), (0, 0))).reshape(B, H, nb, MOBA_BLOCK, Dh)
    vb = jnp.pad(v, ((0, 0), (0, 0), (0, pad), (0, 0))).reshape(B, H, nb, MOBA_BLOCK, Dh)
    k_mean = jnp.mean(kb.astype(jnp.float32), axis=3)
    gate = jnp.einsum('bhsd,bhnd->bhsn', q.astype(jnp.float32), k_mean)
    q_blk = jnp.arange(S) // MOBA_BLOCK
    past = jnp.arange(nb)[None, :] < q_blk[:, None]
    gate = jnp.where(past, gate, NEG_INF)
    n_top = min(MOBA_TOPK, nb)
    _, sel = lax.top_k(gate, n_top)
    sel_valid = jnp.arange(n_top)[None, :] < jnp.minimum(q_blk, n_top)[:, None]
    qc_len = MOBA_Q_CHUNK

    def chunk(c):
        t0 = c * qc_len
        qc = lax.dynamic_slice_in_dim(q, t0, qc_len, axis=2)
        ic = lax.dynamic_slice_in_dim(sel, t0, qc_len, axis=2)
        mc = lax.dynamic_slice_in_dim(sel_valid, t0, qc_len, axis=0)
        k_sel = gather_blocks(kb, ic)
        v_sel = gather_blocks(vb, ic)
        s_sel = jnp.einsum('bhqd,bhqjkd->bhqjk', qc, k_sel).astype(jnp.float32) * scale
        s_sel = jnp.where(mc[None, None, :, :, None], s_sel, NEG_INF).reshape(B, H, qc_len, n_top * MOBA_BLOCK)
        own = t0 // MOBA_BLOCK
        k_own = lax.dynamic_index_in_dim(kb, own, axis=2, keepdims=False)
        v_own = lax.dynamic_index_in_dim(vb, own, axis=2, keepdims=False)
        s_own = jnp.einsum('bhqd,bhkd->bhqk', qc, k_own).astype(jnp.float32) * scale
        t = t0 + jnp.arange(qc_len)
        kpos = own * MOBA_BLOCK + jnp.arange(MOBA_BLOCK)
        s_own = jnp.where(kpos[None, :] <= t[:, None], s_own, NEG_INF)
        p = jax.nn.softmax(jnp.concatenate([s_own, s_sel], axis=-1), axis=-1)
        p_own = p[..., :MOBA_BLOCK].astype(v.dtype)
        p_sel = p[..., MOBA_BLOCK:].reshape(B, H, qc_len, n_top, MOBA_BLOCK).astype(v.dtype)
        return (jnp.einsum('bhqk,bhkd->bhqd', p_own, v_own)
                + jnp.einsum('bhqjk,bhqjkd->bhqd', p_sel, v_sel))

    out = lax.map(chunk, jnp.arange(S // qc_len))
    return out.transpose(1, 2, 0, 3, 4).reshape(B, H, S, Dh)


def moba_mixer(h, w_in, w_out, cos, sin):
    q, k, v = jnp.split(h @ w_in, 3, axis=-1)
    q = apply_rope(to_heads(q, N_HEADS), cos, sin)
    k = apply_rope(to_heads(k, N_HEADS), cos, sin)
    v = to_heads(v, N_HEADS)
    return merge_heads(moba_attention(q, k, v)) @ w_out


def stick_breaking_attention(q, k, v):
    B, H, S, Dh = q.shape
    scale = Dh ** -0.5
    s_pos = jnp.arange(S)

    def block(i):
        t0 = i * SB_Q_BLOCK
        qb = lax.dynamic_slice_in_dim(q, t0, SB_Q_BLOCK, axis=2)
        z = jnp.einsum('bhqd,bhkd->bhqk', qb, k).astype(jnp.float32) * scale
        t = t0 + jnp.arange(SB_Q_BLOCK)
        causal = s_pos[None, :] < t[:, None]
        log_beta = jax.nn.log_sigmoid(z)
        log_1m = jnp.where(causal, jax.nn.log_sigmoid(-z), 0.0)
        log_1m_next = jnp.concatenate([log_1m[..., 1:], jnp.zeros_like(log_1m[..., :1])], axis=-1)
        suffix = lax.cumsum(log_1m_next, axis=log_1m_next.ndim - 1, reverse=True)
        w = jnp.where(causal, jnp.exp(log_beta + suffix), 0.0)
        return jnp.einsum('bhqk,bhkd->bhqd', w.astype(v.dtype), v)

    out = lax.map(block, jnp.arange(S // SB_Q_BLOCK))
    return out.transpose(1, 2, 0, 3, 4).reshape(B, H, S, Dh)


def stick_breaking_mixer(h, w_in, w_out):
    q, k, v = jnp.split(h @ w_in, 3, axis=-1)
    o = stick_breaking_attention(to_heads(q, N_HEADS), to_heads(k, N_HEADS), to_heads(v, N_HEADS))
    return merge_heads(o) @ w_out


def nsa_compress(t, pos_emb, w1, w2):
    S = t.shape[2]
    n_cmp = (S - NSA_CMP_BLOCK) // NSA_CMP_STRIDE + 1
    idx = jnp.arange(n_cmp)[:, None] * NSA_CMP_STRIDE + jnp.arange(NSA_CMP_BLOCK)[None, :]
    blocks = t[:, :, idx, :] + pos_emb
    flat = blocks.reshape(blocks.shape[:3] + (NSA_CMP_BLOCK * HEAD_DIM,))
    return jax.nn.gelu(flat @ w1) @ w2


def nsa_compressed_branch(q, kc, vc):
    S = q.shape[3]
    n_cmp = kc.shape[2]
    s = jnp.einsum('bgnsd,bgcd->bgnsc', q, kc).astype(jnp.float32) * (HEAD_DIM ** -0.5)
    blk_end = jnp.arange(n_cmp) * NSA_CMP_STRIDE + NSA_CMP_BLOCK - 1
    valid = blk_end[None, :] <= jnp.arange(S)[:, None]
    p = jnp.where(valid, jax.nn.softmax(jnp.where(valid, s, NEG_INF), axis=-1), 0.0)
    return jnp.einsum('bgnsc,bgcd->bgnsd', p.astype(vc.dtype), vc), p


def nsa_selected_branch(q, ks, vs, p_cmp):
    B, G, Hg, S, Dh = q.shape
    n_cmp = p_cmp.shape[-1]
    n_sel = S // NSA_SLC_BLOCK
    c_start = np.arange(n_cmp) * NSA_CMP_STRIDE
    s_start = np.arange(n_sel) * NSA_SLC_BLOCK
    lo = np.maximum(c_start[:, None], s_start[None, :])
    hi = np.minimum(c_start[:, None] + NSA_CMP_BLOCK, s_start[None, :] + NSA_SLC_BLOCK)
    overlap = jnp.asarray(np.clip(hi - lo, 0, None) / NSA_CMP_BLOCK, dtype=jnp.float32)
    imp = jnp.einsum('bgnsc,cj->bgsj', p_cmp, overlap)
    q_blk = jnp.arange(S) // NSA_SLC_BLOCK
    j = jnp.arange(n_sel)[None, :]
    forced = (j == 0) | (j == q_blk[:, None]) | (j == q_blk[:, None] - 1)
    score = jnp.where(j <= q_blk[:, None], jnp.where(forced, POS_BIG, imp), NEG_INF)
    n_top = min(NSA_SLC_TOPN, n_sel)
    top_val, top_idx = lax.top_k(score, n_top)
    slot_valid = top_val > 0.5 * NEG_INF
    kb = ks.reshape(B, G, n_sel, NSA_SLC_BLOCK, Dh)
    vb = vs.reshape(B, G, n_sel, NSA_SLC_BLOCK, Dh)
    qc_len = NSA_SLC_Q_CHUNK

    def chunk(c):
        t0 = c * qc_len
        qc = lax.dynamic_slice_in_dim(q, t0, qc_len, axis=3)
        ic = lax.dynamic_slice_in_dim(top_idx, t0, qc_len, axis=2)
        mc = lax.dynamic_slice_in_dim(slot_valid, t0, qc_len, axis=2)
        k_sel = gather_blocks(kb, ic)
        v_sel = gather_blocks(vb, ic)
        s = jnp.einsum('bgnqd,bgqjkd->bgnqjk', qc, k_sel).astype(jnp.float32) * (Dh ** -0.5)
        kpos = ic[..., None] * NSA_SLC_BLOCK + jnp.arange(NSA_SLC_BLOCK)
        t = t0 + jnp.arange(qc_len)
        ok = mc[..., None] & (kpos <= t[None, None, :, None, None])
        s = jnp.where(ok[:, :, None], s, NEG_INF).reshape(B, G, Hg, qc_len, n_top * NSA_SLC_BLOCK)
        p = jax.nn.softmax(s, axis=-1).reshape(B, G, Hg, qc_len, n_top, NSA_SLC_BLOCK)
        return jnp.einsum('bgnqjk,bgqjkd->bgnqd', p.astype(vs.dtype), v_sel)

    out = lax.map(chunk, jnp.arange(S // qc_len))
    return out.transpose(1, 2, 3, 0, 4, 5).reshape(B, G, Hg, S, Dh)


def nsa_window_branch(q, kw, vw):
    B, G, Hg, S, Dh = q.shape
    W, Qb = NSA_WINDOW, NSA_WIN_Q_BLOCK
    kp = jnp.pad(kw, ((0, 0), (0, 0), (W, 0), (0, 0)))
    vp = jnp.pad(vw, ((0, 0), (0, 0), (W, 0), (0, 0)))

    def block(i):
        t0 = i * Qb
        qb = lax.dynamic_slice_in_dim(q, t0, Qb, axis=3)
        kb = lax.dynamic_slice_in_dim(kp, t0, W + Qb, axis=2)
        vb = lax.dynamic_slice_in_dim(vp, t0, W + Qb, axis=2)
        s = jnp.einsum('bgnqd,bgkd->bgnqk', qb, kb).astype(jnp.float32) * (Dh ** -0.5)
        t = t0 + jnp.arange(Qb)
        kpos = t0 - W + jnp.arange(W + Qb)
        diff = t[:, None] - kpos[None, :]
        ok = (diff >= 0) & (diff < W) & (kpos[None, :] >= 0)
        p = jax.nn.softmax(jnp.where(ok, s, NEG_INF), axis=-1)
        return jnp.einsum('bgnqk,bgkd->bgnqd', p.astype(vw.dtype), vb)

    out = lax.map(block, jnp.arange(S // Qb))
    return out.transpose(1, 2, 3, 0, 4, 5).reshape(B, G, Hg, S, Dh)


def nsa_mixer(h, w_in, pos_k, w1_k, w2_k, pos_v, w1_v, w2_v, w_out, cos, sin):
    B, S, _ = h.shape
    G, Hg = NSA_KV_GROUPS, NSA_HEADS_PER_GROUP
    sizes = [N_HEADS * HEAD_DIM] + [NSA_KV_COLS] * 6 + [3 * N_HEADS]
    q, kc, vc, ks, vs, kw, vw, g = jnp.split(h @ w_in, np.cumsum(sizes)[:-1].tolist(), axis=-1)
    q = apply_rope(to_heads(q, N_HEADS), cos, sin).reshape(B, G, Hg, S, HEAD_DIM)
    kc = apply_rope(to_heads(kc, G), cos, sin)
    ks = apply_rope(to_heads(ks, G), cos, sin)
    kw = apply_rope(to_heads(kw, G), cos, sin)
    vc, vs, vw = to_heads(vc, G), to_heads(vs, G), to_heads(vw, G)
    k_cmp = nsa_compress(kc, pos_k, w1_k, w2_k)
    v_cmp = nsa_compress(vc, pos_v, w1_v, w2_v)
    o_cmp, p_cmp = nsa_compressed_branch(q, k_cmp, v_cmp)
    o_slc = nsa_selected_branch(q, ks, vs, p_cmp)
    o_win = nsa_window_branch(q, kw, vw)
    gates = jax.nn.sigmoid(g.astype(jnp.float32)).reshape(B, S, G, Hg, 3)
    gates = gates.transpose(0, 2, 3, 1, 4).astype(q.dtype)
    o = gates[..., 0:1] * o_cmp + gates[..., 1:2] * o_slc + gates[..., 2:3] * o_win
    return merge_heads(o.reshape(B, N_HEADS, S, HEAD_DIM)) @ w_out


def squared_relu_mlp(h, w_up, w_down):
    return jnp.square(jax.nn.relu(h @ w_up)) @ w_down


def setup_inputs(seed: int = 0) -> dict:
    key = jax.random.key(seed)
    keys = iter(jax.random.split(key, 64))

    def nrm(shape, scale):
        return scale * jax.random.normal(next(keys), shape, jnp.float32)

    def gain():
        return 1.0 + 0.05 * jax.random.normal(next(keys), (D_MODEL,), jnp.float32)

    p = {'x': jax.random.normal(next(keys), (BATCH, SEQ, D_MODEL), jnp.float32)}
    for i in range(DEPTH):
        kind = i % N_MIXERS
        cols = NSA_IN_COLS if kind == 2 else QKV_COLS
        p[f'l{i}_ln_mix_pre'] = gain()
        p[f'l{i}_w_in'] = nrm((D_MODEL, cols), D_MODEL ** -0.5)
        if kind == 2:
            for kv in ('k', 'v'):
                p[f'l{i}_cmp_pos_{kv}'] = nrm((NSA_CMP_BLOCK, HEAD_DIM), 0.1)
                p[f'l{i}_cmp_w1_{kv}'] = nrm((NSA_CMP_BLOCK * HEAD_DIM, NSA_CMP_HIDDEN), (NSA_CMP_BLOCK * HEAD_DIM) ** -0.5)
                p[f'l{i}_cmp_w2_{kv}'] = nrm((NSA_CMP_HIDDEN, HEAD_DIM), NSA_CMP_HIDDEN ** -0.5)
        p[f'l{i}_w_out'] = nrm((N_HEADS * HEAD_DIM, D_MODEL), (N_HEADS * HEAD_DIM) ** -0.5)
        p[f'l{i}_ln_mix_post'] = gain()
        p[f'l{i}_ln_mlp_pre'] = gain()
        p[f'l{i}_w_up'] = nrm((D_MODEL, D_FF), D_MODEL ** -0.5)
        p[f'l{i}_w_down'] = nrm((D_FF, D_MODEL), D_FF ** -0.5)
        p[f'l{i}_ln_mlp_post'] = gain()
    return p


def reference(x,
              l0_ln_mix_pre, l0_w_in, l0_w_out, l0_ln_mix_post, l0_ln_mlp_pre, l0_w_up, l0_w_down, l0_ln_mlp_post,
              l1_ln_mix_pre, l1_w_in, l1_w_out, l1_ln_mix_post, l1_ln_mlp_pre, l1_w_up, l1_w_down, l1_ln_mlp_post,
              l2_ln_mix_pre, l2_w_in, l2_cmp_pos_k, l2_cmp_w1_k, l2_cmp_w2_k, l2_cmp_pos_v, l2_cmp_w1_v, l2_cmp_w2_v,
              l2_w_out, l2_ln_mix_post, l2_ln_mlp_pre, l2_w_up, l2_w_down, l2_ln_mlp_post,
              l3_ln_mix_pre, l3_w_in, l3_w_out, l3_ln_mix_post, l3_ln_mlp_pre, l3_w_up, l3_w_down, l3_ln_mlp_post):
    layers = [
        (l0_ln_mix_pre, (l0_w_in, l0_w_out), l0_ln_mix_post, l0_ln_mlp_pre, l0_w_up, l0_w_down, l0_ln_mlp_post),
        (l1_ln_mix_pre, (l1_w_in, l1_w_out), l1_ln_mix_post, l1_ln_mlp_pre, l1_w_up, l1_w_down, l1_ln_mlp_post),
        (l2_ln_mix_pre, (l2_w_in, l2_cmp_pos_k, l2_cmp_w1_k, l2_cmp_w2_k, l2_cmp_pos_v, l2_cmp_w1_v, l2_cmp_w2_v, l2_w_out),
         l2_ln_mix_post, l2_ln_mlp_pre, l2_w_up, l2_w_down, l2_ln_mlp_post),
        (l3_ln_mix_pre, (l3_w_in, l3_w_out), l3_ln_mix_post, l3_ln_mlp_pre, l3_w_up, l3_w_down, l3_ln_mlp_post),
    ]
    cos, sin = rope_tables(x.shape[1], x.dtype)
    for i in range(DEPTH):
        ln_mix_pre, mix_w, ln_mix_post, ln_mlp_pre, w_up, w_down, ln_mlp_post = layers[i]
        h = rms_norm(x, ln_mix_pre)
        kind = i % N_MIXERS
        if kind == 0:
            m = moba_mixer(h, *mix_w, cos, sin)
        elif kind == 1:
            m = stick_breaking_mixer(h, *mix_w)
        else:
            m = nsa_mixer(h, *mix_w, cos, sin)
        x = x + rms_norm(m, ln_mix_post)
        x = x + rms_norm(squared_relu_mlp(rms_norm(x, ln_mlp_pre), w_up, w_down), ln_mlp_post)
    return x
```

```python
import functools

import numpy as np
import jax
import jax.numpy as jnp
from jax import lax
from jax.experimental import pallas as pl
from jax.experimental.pallas import tpu as pltpu

D_MODEL = 1024
N_HEADS = 16
HEAD_DIM = 64
D_FF = 4 * D_MODEL
ROPE_THETA = 10000.0
NORM_EPS = 1e-6
NEG_INF = -1e30
POS_BIG = 1e30

MOBA_BLOCK = 256
MOBA_TOPK = 3

NSA_KV_GROUPS = 4
NSA_HEADS_PER_GROUP = N_HEADS // NSA_KV_GROUPS
NSA_CMP_BLOCK = 32
NSA_CMP_STRIDE = 16
NSA_CMP_HIDDEN = 256
NSA_SLC_BLOCK = 64
NSA_SLC_TOPN = 16
NSA_WINDOW = 512
NSA_KV_COLS = NSA_KV_GROUPS * HEAD_DIM

LANES = 128
KEY_TILE = 256
Q_TILE = 256
ROW_TILE = 512
FF_CHUNK = 512
PROJ_COLS = 3 * N_HEADS * HEAD_DIM
VMEM_LIMIT = 56 * 1024 * 1024

F32 = jnp.float32
BF16 = jnp.bfloat16


def _nt(a, b):
    return lax.dot_general(a, b, (((1,), (1,)), ((), ())), preferred_element_type=F32)


def _nn(a, b):
    return jnp.dot(a, b, preferred_element_type=F32)


def _split_bf16(x):
    hi = x.astype(BF16)
    lo = (x - hi.astype(F32)).astype(BF16)
    return hi, lo


def _rms(x, g):
    ms = jnp.mean(x * x, axis=-1, keepdims=True)
    return x * lax.rsqrt(ms + NORM_EPS) * g


def _const_spec(shape):
    nd = len(shape)
    return pl.BlockSpec(shape, lambda *_: (0,) * nd, pipeline_mode=pl.Buffered(1))


def _proj_kernel(x_ref, g_ref, w_ref, cos_ref, sina_ref, sinb_ref, *rest, rope_slabs, has_gate):
    if has_gate:
        wg_ref, o_ref, og_ref = rest
    else:
        (o_ref,) = rest
    h = _rms(x_ref[...], g_ref[...]).astype(BF16)
    cos = cos_ref[...]
    sina = sina_ref[...]
    sinb = sinb_ref[...]
    n_slabs = len(rope_slabs)
    for c in range(n_slabs // 2):
        y = _nn(h, w_ref[:, c * 2 * LANES:(c + 1) * 2 * LANES])
        for half in range(2):
            s = 2 * c + half
            ys = y[:, half * LANES:(half + 1) * LANES]
            if rope_slabs[s]:
                ys = (ys * cos + pltpu.roll(ys, LANES - HEAD_DIM // 2, 1) * sina
                      + pltpu.roll(ys, HEAD_DIM // 2, 1) * sinb)
            o_ref[:, s * LANES:(s + 1) * LANES] = ys.astype(o_ref.dtype)
    if has_gate:
        og_ref[...] = _nn(h, wg_ref[...])


def _norm_proj(x2, g, w, tables, rope_slabs, seq, wg=None):
    n = x2.shape[0]
    cols = w.shape[1]
    pos_blocks = seq // ROW_TILE
    has_gate = wg is not None
    row = lambda i: (i, 0)
    tab = pl.BlockSpec((ROW_TILE, LANES), lambda i: (i % pos_blocks, 0))
    in_specs = [pl.BlockSpec((ROW_TILE, D_MODEL), row), _const_spec((1, D_MODEL)),
                _const_spec((D_MODEL, cols)), tab, tab, tab]
    args = [x2, g.reshape(1, D_MODEL), w, *tables]
    out_shape = [jax.ShapeDtypeStruct((n, cols), BF16)]
    out_specs = [pl.BlockSpec((ROW_TILE, cols), row)]
    if has_gate:
        in_specs.append(_const_spec((D_MODEL, LANES)))
        args.append(wg)
        out_shape.append(jax.ShapeDtypeStruct((n, LANES), F32))
        out_specs.append(pl.BlockSpec((ROW_TILE, LANES), row))
    outs = pl.pallas_call(
        functools.partial(_proj_kernel, rope_slabs=rope_slabs, has_gate=has_gate),
        grid=(n // ROW_TILE,),
        in_specs=in_specs,
        out_specs=out_specs,
        out_shape=out_shape,
        compiler_params=pltpu.CompilerParams(dimension_semantics=("arbitrary",),
                                             vmem_limit_bytes=VMEM_LIMIT),
        name="norm_proj",
    )(*args)
    return outs if has_gate else outs[0]


def _tail_kernel(o_ref, x_ref, wout_ref, gpost_ref, gpre_ref, wup_ref, wdown_ref, gmlp_ref, out_ref):
    m = _nn(o_ref[...], wout_ref[...])
    x1 = x_ref[...] + _rms(m, gpost_ref[...])
    h = _rms(x1, gpre_ref[...]).astype(BF16)
    acc = jnp.zeros((ROW_TILE, D_MODEL), F32)
    for j in range(D_FF // FF_CHUNK):
        u = _nn(h, wup_ref[:, j * FF_CHUNK:(j + 1) * FF_CHUNK])
        a = jnp.square(jnp.maximum(u, 0.0)).astype(BF16)
        acc = acc + _nn(a, wdown_ref[j * FF_CHUNK:(j + 1) * FF_CHUNK, :])
    out_ref[...] = x1 + _rms(acc, gmlp_ref[...])


def _layer_tail(o2, x2, wout, gpost, gpre, wup, wdown, gmlp):
    n = x2.shape[0]
    row = lambda i: (i, 0)
    vec = lambda v: v.reshape(1, D_MODEL)
    return pl.pallas_call(
        _tail_kernel,
        grid=(n // ROW_TILE,),
        in_specs=[pl.BlockSpec((ROW_TILE, N_HEADS * HEAD_DIM), row),
                  pl.BlockSpec((ROW_TILE, D_MODEL), row),
                  _const_spec((N_HEADS * HEAD_DIM, D_MODEL)), _const_spec((1, D_MODEL)),
                  _const_spec((1, D_MODEL)), _const_spec((D_MODEL, D_FF)),
                  _const_spec((D_FF, D_MODEL)), _const_spec((1, D_MODEL))],
        out_specs=pl.BlockSpec((ROW_TILE, D_MODEL), row),
        out_shape=jax.ShapeDtypeStruct((n, D_MODEL), F32),
        compiler_params=pltpu.CompilerParams(dimension_semantics=("arbitrary",),
                                             vmem_limit_bytes=VMEM_LIMIT),
        name="layer_tail",
    )(o2, x2, wout, vec(gpost), vec(gpre), wup, wdown, vec(gmlp))


def _pair_queries(q, scale):
    lane = lax.broadcasted_iota(jnp.int32, q.shape, 1)
    qs = q * jnp.asarray(scale, q.dtype)
    zero = jnp.zeros_like(qs)
    return jnp.concatenate([jnp.where(lane < HEAD_DIM, qs, zero),
                            jnp.where(lane >= HEAD_DIM, qs, zero)], axis=0)


def _transpose_value_tiles(v_ref, vt_ref, n_tiles):
    for n in range(n_tiles):
        blk = v_ref[n * KEY_TILE:(n + 1) * KEY_TILE, :].astype(F32)
        vt_ref[n] = blk.T.astype(BF16)


def _pair_output(acc, inv_l):
    t = acc.shape[1] // 2
    top = acc[:HEAD_DIM, :t] * inv_l[:, :t]
    bot = acc[HEAD_DIM:, t:] * inv_l[:, t:]
    return jnp.concatenate([top, bot], axis=0).T


def _moba_kernel(q_ref, k_ref, v_ref, o_ref, kmean_ref, vt_ref, sel_ref, acc_ref, *, n_blocks):
    qi = pl.program_id(2)
    w2 = 2 * Q_TILE

    @pl.when(qi == 0)
    def _():
        for n in range(n_blocks):
            blk = k_ref[n * MOBA_BLOCK:(n + 1) * MOBA_BLOCK, :].astype(F32)
            kmean_ref[n:n + 1, :] = jnp.mean(blk, axis=0, keepdims=True)
        _transpose_value_tiles(v_ref, vt_ref, n_blocks)

    qcat = _pair_queries(q_ref[...], HEAD_DIM ** -0.5)

    km_hi, km_lo = _split_bf16(kmean_ref[...])
    gate = _nt(km_hi, qcat) + _nt(km_lo, qcat)
    blk_id = lax.broadcasted_iota(jnp.int32, gate.shape, 0)
    g = jnp.where(blk_id < qi, gate, NEG_INF)
    sel = jnp.zeros(gate.shape, F32)
    for j in range(MOBA_TOPK):
        mx = jnp.max(g, axis=0, keepdims=True)
        first = jnp.min(jnp.where(g == mx, blk_id, n_blocks), axis=0, keepdims=True)
        hit = blk_id == first
        sel = jnp.where(hit, jnp.where(j < qi, 1.0, 0.0), sel)
        g = jnp.where(hit, -jnp.inf, g)
    sel_ref[...] = sel

    acc_ref[...] = jnp.zeros(acc_ref.shape, F32)

    def past_block(n, carry):
        m, l = carry
        off = pl.multiple_of(n * KEY_TILE, KEY_TILE)
        s = _nt(k_ref[pl.ds(off, KEY_TILE), :], qcat)
        chosen = sel_ref[pl.ds(n, 1), :] > 0.5
        bm = jnp.max(s, axis=0, keepdims=True)
        m_new = jnp.maximum(m, jnp.where(chosen, bm, NEG_INF))
        p = jnp.exp(s - jnp.where(chosen, m_new, POS_BIG))
        alpha = jnp.exp(m - m_new)
        l = alpha * l + jnp.sum(p, axis=0, keepdims=True)
        acc_ref[...] = acc_ref[...] * alpha + _nn(vt_ref[n], p.astype(BF16))
        return m_new, l

    m0 = jnp.full((1, w2), NEG_INF, F32)
    l0 = jnp.zeros((1, w2), F32)
    m, l = lax.fori_loop(0, qi, past_block, (m0, l0))

    off = pl.multiple_of(qi * KEY_TILE, KEY_TILE)
    s = _nt(k_ref[pl.ds(off, KEY_TILE), :], qcat)
    kpos = lax.broadcasted_iota(jnp.int32, s.shape, 0)
    tpos = lax.broadcasted_iota(jnp.int32, s.shape, 1) % Q_TILE
    s = jnp.where(kpos <= tpos, s, NEG_INF)
    m_new = jnp.maximum(m, jnp.max(s, axis=0, keepdims=True))
    p = jnp.exp(s - m_new)
    alpha = jnp.exp(m - m_new)
    l = alpha * l + jnp.sum(p, axis=0, keepdims=True)
    acc = acc_ref[...] * alpha + _nn(vt_ref[qi], p.astype(BF16))
    o_ref[...] = _pair_output(acc, 1.0 / l).astype(o_ref.dtype)


def _moba_attention(qkv, batch, seq):
    n = qkv.shape[0]
    nq = seq // Q_TILE
    n_blocks = seq // MOBA_BLOCK
    pairs = N_HEADS // 2
    return pl.pallas_call(
        functools.partial(_moba_kernel, n_blocks=n_blocks),
        grid=(batch, pairs, nq),
        in_specs=[pl.BlockSpec((Q_TILE, LANES), lambda b, p, i: (b * nq + i, p)),
                  pl.BlockSpec((seq, LANES), lambda b, p, i: (b, pairs + p)),
                  pl.BlockSpec((seq, LANES), lambda b, p, i: (b, 2 * pairs + p))],
        out_specs=pl.BlockSpec((Q_TILE, LANES), lambda b, p, i: (b * nq + i, p)),
        out_shape=jax.ShapeDtypeStruct((n, N_HEADS * HEAD_DIM), BF16),
        scratch_shapes=[pltpu.VMEM((n_blocks, LANES), F32),
                        pltpu.VMEM((n_blocks, LANES, KEY_TILE), BF16),
                        pltpu.VMEM((n_blocks, 2 * Q_TILE), F32),
                        pltpu.VMEM((LANES, 2 * Q_TILE), F32)],
        compiler_params=pltpu.CompilerParams(
            dimension_semantics=("arbitrary", "arbitrary", "arbitrary"),
            vmem_limit_bytes=VMEM_LIMIT),
        name="moba_attention",
    )(qkv, qkv, qkv)


def _log_sigmoid_pair(z):
    tail = jnp.log(1.0 + jnp.exp(-jnp.abs(z)))
    return jnp.minimum(z, 0.0) - tail, jnp.minimum(-z, 0.0) - tail


def _sb_kernel(q_ref, k_ref, v_ref, o_ref, vt_ref, acc_ref, *, n_tiles):
    qi = pl.program_id(2)
    w2 = 2 * Q_TILE

    @pl.when(qi == 0)
    def _():
        _transpose_value_tiles(v_ref, vt_ref, n_tiles)

    qcat = _pair_queries(q_ref[...], HEAD_DIM ** -0.5)
    r = lax.broadcasted_iota(jnp.int32, (KEY_TILE, KEY_TILE), 0)
    c = lax.broadcasted_iota(jnp.int32, (KEY_TILE, KEY_TILE), 1)
    later = jnp.where(c > r, 1.0, 0.0).astype(BF16)

    def suffix_sum(log_1m):
        hi, lo = _split_bf16(log_1m)
        return _nn(later, hi) + _nn(later, lo)

    off = pl.multiple_of(qi * KEY_TILE, KEY_TILE)
    z = _nt(k_ref[pl.ds(off, KEY_TILE), :], qcat)
    kpos = lax.broadcasted_iota(jnp.int32, z.shape, 0)
    tpos = lax.broadcasted_iota(jnp.int32, z.shape, 1) % Q_TILE
    causal = kpos < tpos
    log_beta, log_1m = _log_sigmoid_pair(z)
    log_1m = jnp.where(causal, log_1m, 0.0)
    wgt = jnp.where(causal, jnp.exp(log_beta + suffix_sum(log_1m)), 0.0)
    acc_ref[...] = _nn(vt_ref[qi], wgt.astype(BF16))
    carry0 = jnp.sum(log_1m, axis=0, keepdims=True)

    def earlier_tile(i, carry):
        n = qi - 1 - i
        off = pl.multiple_of(n * KEY_TILE, KEY_TILE)
        z = _nt(k_ref[pl.ds(off, KEY_TILE), :], qcat)
        log_beta, log_1m = _log_sigmoid_pair(z)
        wgt = jnp.exp(log_beta + suffix_sum(log_1m) + carry)
        acc_ref[...] += _nn(vt_ref[n], wgt.astype(BF16))
        return carry + jnp.sum(log_1m, axis=0, keepdims=True)

    lax.fori_loop(0, qi, earlier_tile, carry0)
    o_ref[...] = _pair_output(acc_ref[...], jnp.ones((1, w2), F32)).astype(o_ref.dtype)


def _sb_attention(qkv, batch, seq):
    n = qkv.shape[0]
    nq = seq // Q_TILE
    n_tiles = seq // KEY_TILE
    pairs = N_HEADS // 2
    return pl.pallas_call(
        functools.partial(_sb_kernel, n_tiles=n_tiles),
        grid=(batch, pairs, nq),
        in_specs=[pl.BlockSpec((Q_TILE, LANES), lambda b, p, i: (b * nq + i, p)),
                  pl.BlockSpec((seq, LANES), lambda b, p, i: (b, pairs + p)),
                  pl.BlockSpec((seq, LANES), lambda b, p, i: (b, 2 * pairs + p))],
        out_specs=pl.BlockSpec((Q_TILE, LANES), lambda b, p, i: (b * nq + i, p)),
        out_shape=jax.ShapeDtypeStruct((n, N_HEADS * HEAD_DIM), BF16),
        scratch_shapes=[pltpu.VMEM((n_tiles, LANES, KEY_TILE), BF16),
                        pltpu.VMEM((LANES, 2 * Q_TILE), F32)],
        compiler_params=pltpu.CompilerParams(
            dimension_semantics=("arbitrary", "arbitrary", "arbitrary"),
            vmem_limit_bytes=VMEM_LIMIT),
        name="stick_breaking_attention",
    )(qkv, qkv, qkv)


def _nsa_compress_kernel(ak_ref, av_ref, posk_ref, posv_ref, w1k_ref, w1v_ref, w2k_ref, w2v_ref,
                         kc_ref, vct_ref):
    half = NSA_CMP_STRIDE * HEAD_DIM
    n_rows = ak_ref.shape[1]

    def compress(a_ref, pos_ref, w1_ref, w2_ref):
        a = a_ref[0].astype(F32)
        first = (a + pos_ref[0:1, :]).astype(BF16)
        second = (a + pos_ref[1:2, :]).astype(BF16)
        pre = _nn(first, w1_ref[:half, :]) + pltpu.roll(_nn(second, w1_ref[half:, :]), n_rows - 1, 0)
        return _nn(jax.nn.gelu(pre).astype(BF16), w2_ref[...])

    kc_ref[0] = compress(ak_ref, posk_ref, w1k_ref, w2k_ref).astype(BF16)
    vct_ref[0] = compress(av_ref, posv_ref, w1v_ref, w2v_ref).T[:HEAD_DIM, :].astype(BF16)


def _nsa_compress(ak, av, posk, posv, w1k, w1v, w2k, w2v):
    bg, rows, feat = ak.shape
    blk = pl.BlockSpec((1, rows, feat), lambda i: (i, 0, 0))
    return pl.pallas_call(
        _nsa_compress_kernel,
        grid=(bg,),
        in_specs=[blk, blk, _const_spec(posk.shape), _const_spec(posv.shape),
                  _const_spec(w1k.shape), _const_spec(w1v.shape),
                  _const_spec(w2k.shape), _const_spec(w2v.shape)],
        out_specs=[pl.BlockSpec((1, rows, LANES), lambda i: (i, 0, 0)),
                   pl.BlockSpec((1, HEAD_DIM, rows), lambda i: (i, 0, 0))],
        out_shape=[jax.ShapeDtypeStruct((bg, rows, LANES), BF16),
                   jax.ShapeDtypeStruct((bg, HEAD_DIM, rows), BF16)],
        compiler_params=pltpu.CompilerParams(dimension_semantics=("arbitrary",),
                                             vmem_limit_bytes=VMEM_LIMIT),
        name="nsa_compress",
    )(ak, av, posk, posv, w1k, w1v, w2k, w2v)


def _nsa_kernel(q_ref, kcmp_ref, vcmpt_ref, ks_ref, kw_ref, vsw_ref, gate_ref, ovl_ref, o_ref,
                vst_ref, vwt_ref, sel_ref, gt_ref, m_ref, l_ref, acc_ref, *, n_tiles, n_sel):
    grp = pl.program_id(1)
    qi = pl.program_id(2)
    hg = NSA_HEADS_PER_GROUP
    t = Q_TILE
    wq = hg * t
    sub = KEY_TILE // NSA_SLC_BLOCK
    scale = HEAD_DIM ** -0.5

    @pl.when(qi == 0)
    def _():
        for n in range(n_tiles):
            blk = vsw_ref[n * KEY_TILE:(n + 1) * KEY_TILE, :].astype(F32).T.astype(BF16)
            vst_ref[n] = blk[:HEAD_DIM, :]
            vwt_ref[n] = blk[HEAD_DIM:, :]

    qcat = jnp.concatenate([_pair_queries(q_ref[:, :LANES], scale),
                            _pair_queries(q_ref[:, LANES:], scale)], axis=0)
    tq = qi * t + lax.broadcasted_iota(jnp.int32, (1, wq), 1) % t

    s = _nt(kcmp_ref[0], qcat)
    cmp_end = lax.broadcasted_iota(jnp.int32, s.shape, 0) * NSA_CMP_STRIDE + (NSA_CMP_BLOCK - 1)
    valid = cmp_end <= tq
    m = jnp.max(jnp.where(valid, s, NEG_INF), axis=0, keepdims=True)
    e = jnp.where(valid, jnp.exp(s - m), 0.0)
    l = jnp.sum(e, axis=0, keepdims=True)
    p_cmp = e * jnp.where(l > 0.0, 1.0 / l, 0.0)
    o_cmp = _nn(vcmpt_ref[0], p_cmp.astype(BF16))

    p_grp = p_cmp[:, :t]
    for h in range(1, hg):
        p_grp = p_grp + p_cmp[:, h * t:(h + 1) * t]
    p_hi, p_lo = _split_bf16(p_grp)
    imp = _nn(ovl_ref[...], p_hi) + _nn(ovl_ref[...], p_lo)
    j = lax.broadcasted_iota(jnp.int32, imp.shape, 0)
    q_blk = (qi * t + lax.broadcasted_iota(jnp.int32, imp.shape, 1)) // NSA_SLC_BLOCK
    forced = jnp.logical_or(j == 0, jnp.logical_or(j == q_blk, j == q_blk - 1))
    score = jnp.where(j <= q_blk, jnp.where(forced, POS_BIG, imp), NEG_INF)
    sel = jnp.zeros(imp.shape, F32)
    for _ in range(min(NSA_SLC_TOPN, n_sel)):
        mx = jnp.max(score, axis=0, keepdims=True)
        first = jnp.min(jnp.where(score == mx, j, n_sel), axis=0, keepdims=True)
        hit = j == first
        sel = jnp.where(jnp.logical_and(hit, mx > 0.5 * NEG_INF), 1.0, sel)
        score = jnp.where(hit, -jnp.inf, score)
    sel_ref[...] = jnp.concatenate([sel] * hg, axis=1)

    def softmax_step(s, v_tile):
        m_old = m_ref[...]
        m_new = jnp.maximum(m_old, jnp.max(s, axis=0, keepdims=True))
        p = jnp.exp(s - m_new)
        alpha = jnp.exp(m_old - m_new)
        l_ref[...] = alpha * l_ref[...] + jnp.sum(p, axis=0, keepdims=True)
        acc_ref[...] = acc_ref[...] * alpha + _nn(v_tile, p.astype(BF16))
        m_ref[...] = m_new

    def reset_state():
        m_ref[...] = jnp.full(m_ref.shape, NEG_INF, F32)
        l_ref[...] = jnp.zeros(l_ref.shape, F32)
        acc_ref[...] = jnp.zeros(acc_ref.shape, F32)

    def mask_unselected(s, n):
        parts = []
        for jj in range(sub):
            chosen = sel_ref[pl.ds(n * sub + jj, 1), :] > 0.5
            parts.append(s[jj * NSA_SLC_BLOCK:(jj + 1) * NSA_SLC_BLOCK, :]
                         + jnp.where(chosen, 0.0, NEG_INF))
        return jnp.concatenate(parts, axis=0)

    kloc = lax.broadcasted_iota(jnp.int32, (KEY_TILE, wq), 0)
    tloc = lax.broadcasted_iota(jnp.int32, (KEY_TILE, wq), 1) % t
    own = pl.multiple_of(qi * KEY_TILE, KEY_TILE)

    reset_state()

    def sel_tile(n, carry):
        off = pl.multiple_of(n * KEY_TILE, KEY_TILE)
        softmax_step(mask_unselected(_nt(ks_ref[pl.ds(off, KEY_TILE), :], qcat), n), vst_ref[n])
        return carry

    lax.fori_loop(0, qi, sel_tile, 0)
    s_own = mask_unselected(_nt(ks_ref[pl.ds(own, KEY_TILE), :], qcat), qi)
    softmax_step(jnp.where(kloc <= tloc, s_own, NEG_INF), vst_ref[qi])
    o_slc = acc_ref[...] * (1.0 / l_ref[...])

    reset_state()
    s_own = _nt(kw_ref[pl.ds(own, KEY_TILE), :], qcat)
    softmax_step(jnp.where(kloc <= tloc, s_own, NEG_INF), vwt_ref[qi])

    @pl.when(qi >= 1)
    def _():
        off = pl.multiple_of((qi - 1) * KEY_TILE, KEY_TILE)
        softmax_step(_nt(kw_ref[pl.ds(off, KEY_TILE), :], qcat), vwt_ref[qi - 1])

    @pl.when(qi >= NSA_WINDOW // KEY_TILE)
    def _():
        back = NSA_WINDOW // KEY_TILE
        off = pl.multiple_of((qi - back) * KEY_TILE, KEY_TILE)
        s_back = _nt(kw_ref[pl.ds(off, KEY_TILE), :], qcat)
        softmax_step(jnp.where(kloc > tloc, s_back, NEG_INF), vwt_ref[qi - back])

    o_win = acc_ref[...] * (1.0 / l_ref[...])

    gt_ref[...] = jax.nn.sigmoid(gate_ref[...]).T
    gates = gt_ref[pl.ds(pl.multiple_of(grp * 16, 16), 16), :]
    heads = []
    for h in range(hg):
        cols = slice(h * t, (h + 1) * t)
        heads.append(gates[3 * h:3 * h + 1, :] * o_cmp[:, cols]
                     + gates[3 * h + 1:3 * h + 2, :] * o_slc[:, cols]
                     + gates[3 * h + 2:3 * h + 3, :] * o_win[:, cols])
    for pair in range(hg // 2):
        both = jnp.concatenate([heads[2 * pair], heads[2 * pair + 1]], axis=0)
        o_ref[:, pair * LANES:(pair + 1) * LANES] = both.T.astype(o_ref.dtype)


def _nsa_attention(proj, gates, kcmp, vcmpt, overlap_t, batch, seq):
    n = proj.shape[0]
    nq = seq // Q_TILE
    n_tiles = seq // KEY_TILE
    n_sel = seq // NSA_SLC_BLOCK
    g = NSA_KV_GROUPS
    hg = NSA_HEADS_PER_GROUP
    n_cmp_pad = kcmp.shape[1]
    ks0, kw0, vsw0 = 12, 16, 20
    return pl.pallas_call(
        functools.partial(_nsa_kernel, n_tiles=n_tiles, n_sel=n_sel),
        grid=(batch, g, nq),
        in_specs=[pl.BlockSpec((Q_TILE, hg * HEAD_DIM), lambda b, k, i: (b * nq + i, k)),
                  pl.BlockSpec((1, n_cmp_pad, LANES), lambda b, k, i: (b * g + k, 0, 0)),
                  pl.BlockSpec((1, HEAD_DIM, n_cmp_pad), lambda b, k, i: (b * g + k, 0, 0)),
                  pl.BlockSpec((seq, LANES), lambda b, k, i: (b, ks0 + k)),
                  pl.BlockSpec((seq, LANES), lambda b, k, i: (b, kw0 + k)),
                  pl.BlockSpec((seq, LANES), lambda b, k, i: (b, vsw0 + k)),
                  pl.BlockSpec((Q_TILE, LANES), lambda b, k, i: (b * nq + i, 0)),
                  _const_spec(overlap_t.shape)],
        out_specs=pl.BlockSpec((Q_TILE, hg * HEAD_DIM), lambda b, k, i: (b * nq + i, k)),
        out_shape=jax.ShapeDtypeStruct((n, N_HEADS * HEAD_DIM), BF16),
        scratch_shapes=[pltpu.VMEM((n_tiles, HEAD_DIM, KEY_TILE), BF16),
                        pltpu.VMEM((n_tiles, HEAD_DIM, KEY_TILE), BF16),
                        pltpu.VMEM((n_sel, hg * Q_TILE), F32),
                        pltpu.VMEM((LANES, Q_TILE), F32),
                        pltpu.VMEM((1, hg * Q_TILE), F32),
                        pltpu.VMEM((1, hg * Q_TILE), F32),
                        pltpu.VMEM((HEAD_DIM, hg * Q_TILE), F32)],
        compiler_params=pltpu.CompilerParams(
            dimension_semantics=("arbitrary", "arbitrary", "arbitrary"),
            vmem_limit_bytes=VMEM_LIMIT),
        name="nsa_attention",
    )(proj, kcmp, vcmpt, proj, proj, proj, gates, overlap_t)


def _rope_tables(seq):
    inv_freq = 1.0 / (ROPE_THETA ** (jnp.arange(0, HEAD_DIM, 2, dtype=F32) / HEAD_DIM))
    ang = jnp.arange(seq, dtype=F32)[:, None] * inv_freq[None, :]
    reps = LANES // (HEAD_DIM // 2)
    cos = jnp.tile(jnp.cos(ang), (1, reps))
    sin = jnp.tile(jnp.sin(ang), (1, reps))
    first_half = (jnp.arange(LANES) % HEAD_DIM) < HEAD_DIM // 2
    return cos, jnp.where(first_half, -sin, 0.0), jnp.where(first_half, 0.0, sin)


def _nsa_weight_layout(w_in):
    qd = N_HEADS * HEAD_DIM
    kv = NSA_KV_COLS
    q, kc, vc, ks, vs, kw, vw, gt = jnp.split(
        w_in, np.cumsum([qd] + [kv] * 6).tolist(), axis=1)

    def dup(w):
        w = w.reshape(D_MODEL, NSA_KV_GROUPS, 1, HEAD_DIM)
        return jnp.broadcast_to(w, (D_MODEL, NSA_KV_GROUPS, 2, HEAD_DIM)).reshape(D_MODEL, -1)

    vsw = jnp.concatenate([vs.reshape(D_MODEL, NSA_KV_GROUPS, 1, HEAD_DIM),
                           vw.reshape(D_MODEL, NSA_KV_GROUPS, 1, HEAD_DIM)], axis=2)
    main = jnp.concatenate([q, kc, vc, dup(ks), dup(kw), vsw.reshape(D_MODEL, -1)], axis=1)
    per_group = 3 * NSA_HEADS_PER_GROUP
    gt = gt.reshape(D_MODEL, NSA_KV_GROUPS, per_group)
    gt = jnp.pad(gt, ((0, 0), (0, 0), (0, 16 - per_group))).reshape(D_MODEL, -1)
    gt = jnp.pad(gt, ((0, 0), (0, LANES - gt.shape[1])))
    return main.astype(BF16), gt.astype(BF16)


def _overlap_t(seq):
    n_cmp = (seq - NSA_CMP_BLOCK) // NSA_CMP_STRIDE + 1
    n_sel = seq // NSA_SLC_BLOCK
    c_start = np.arange(n_cmp) * NSA_CMP_STRIDE
    s_start = np.arange(n_sel) * NSA_SLC_BLOCK
    lo = np.maximum(c_start[:, None], s_start[None, :])
    hi = np.minimum(c_start[:, None] + NSA_CMP_BLOCK, s_start[None, :] + NSA_SLC_BLOCK)
    ov = np.clip(hi - lo, 0, None) / NSA_CMP_BLOCK
    ov = np.pad(ov, ((0, seq // NSA_CMP_STRIDE - n_cmp), (0, 0)))
    return jnp.asarray(ov.T, dtype=BF16)


_QK_ROPE = (True,) * 16 + (False,) * 8
_NO_ROPE = (False,) * 24
_NSA_ROPE = (True,) * 10 + (False,) * 2 + (True,) * 8 + (False,) * 4


def kernel(x, l0_ln_mix_pre, l0_w_in, l0_w_out, l0_ln_mix_post, l0_ln_mlp_pre, l0_w_up, l0_w_down, l0_ln_mlp_post, l1_ln_mix_pre, l1_w_in, l1_w_out, l1_ln_mix_post, l1_ln_mlp_pre, l1_w_up, l1_w_down, l1_ln_mlp_post, l2_ln_mix_pre, l2_w_in, l2_cmp_pos_k, l2_cmp_w1_k, l2_cmp_w2_k, l2_cmp_pos_v, l2_cmp_w1_v, l2_cmp_w2_v, l2_w_out, l2_ln_mix_post, l2_ln_mlp_pre, l2_w_up, l2_w_down, l2_ln_mlp_post, l3_ln_mix_pre, l3_w_in, l3_w_out, l3_ln_mix_post, l3_ln_mlp_pre, l3_w_up, l3_w_down, l3_ln_mlp_post):
    batch, seq, _ = x.shape
    assert seq % ROW_TILE == 0 and seq % KEY_TILE == 0 and Q_TILE == KEY_TILE == MOBA_BLOCK
    tables = _rope_tables(seq)
    x2 = x.reshape(batch * seq, D_MODEL)
    bf = lambda w: w.astype(BF16)

    def tail(o, x2, w_out, g_post, g_pre, w_up, w_down, g_mlp):
        return _layer_tail(o, x2, bf(w_out), g_post, g_pre, bf(w_up), bf(w_down), g_mlp)

    qkv = _norm_proj(x2, l0_ln_mix_pre, bf(l0_w_in), tables, _QK_ROPE, seq)
    x2 = tail(_moba_attention(qkv, batch, seq), x2, l0_w_out, l0_ln_mix_post, l0_ln_mlp_pre,
              l0_w_up, l0_w_down, l0_ln_mlp_post)

    qkv = _norm_proj(x2, l1_ln_mix_pre, bf(l1_w_in), tables, _NO_ROPE, seq)
    x2 = tail(_sb_attention(qkv, batch, seq), x2, l1_w_out, l1_ln_mix_post, l1_ln_mlp_pre,
              l1_w_up, l1_w_down, l1_ln_mlp_post)

    w_main, w_gate = _nsa_weight_layout(l2_w_in)
    proj, gates = _norm_proj(x2, l2_ln_mix_pre, w_main, tables, _NSA_ROPE, seq, wg=w_gate)
    qd = N_HEADS * HEAD_DIM
    rows = seq // NSA_CMP_STRIDE

    def row_groups(cols):
        a = cols.reshape(batch, seq, NSA_KV_GROUPS, HEAD_DIM).transpose(0, 2, 1, 3)
        return a.reshape(batch * NSA_KV_GROUPS, rows, NSA_CMP_STRIDE * HEAD_DIM)

    flat_pos = lambda p: p.reshape(2, NSA_CMP_STRIDE * HEAD_DIM)
    w2_dup = lambda w: bf(jnp.concatenate([w, w], axis=1))
    kcmp, vcmpt = _nsa_compress(
        row_groups(proj[:, qd:qd + NSA_KV_COLS]), row_groups(proj[:, qd + NSA_KV_COLS:qd + 2 * NSA_KV_COLS]),
        flat_pos(l2_cmp_pos_k), flat_pos(l2_cmp_pos_v), bf(l2_cmp_w1_k), bf(l2_cmp_w1_v),
        w2_dup(l2_cmp_w2_k), w2_dup(l2_cmp_w2_v))
    o = _nsa_attention(proj, gates, kcmp, vcmpt, _overlap_t(seq), batch, seq)
    x2 = tail(o, x2, l2_w_out, l2_ln_mix_post, l2_ln_mlp_pre, l2_w_up, l2_w_down, l2_ln_mlp_post)

    qkv = _norm_proj(x2, l3_ln_mix_pre, bf(l3_w_in), tables, _QK_ROPE, seq)
    x2 = tail(_moba_attention(qkv, batch, seq), x2, l3_w_out, l3_ln_mix_post, l3_ln_mlp_pre,
              l3_w_up, l3_w_down, l3_ln_mlp_post)
    return x2.reshape(batch, seq, D_MODEL)
```

```python
import functools
import math

import numpy as np
import jax
import jax.numpy as jnp
from jax import lax
from jax.experimental import pallas as pl
from jax.experimental.pallas import tpu as pltpu

D_MODEL = 1024
N_HEADS = 16
HEAD_DIM = 64
D_FF = 4 * D_MODEL
ROPE_THETA = 10000.0
NORM_EPS = 1e-6
NEG_INF = -1e30
POS_BIG = 1e30

MOBA_BLOCK = 256
MOBA_TOPK = 3

NSA_KV_GROUPS = 4
NSA_HEADS_PER_GROUP = N_HEADS // NSA_KV_GROUPS
NSA_CMP_BLOCK = 32
NSA_CMP_STRIDE = 16
NSA_CMP_HIDDEN = 256
NSA_SLC_BLOCK = 64
NSA_SLC_TOPN = 16
NSA_WINDOW = 512
NSA_KV_COLS = NSA_KV_GROUPS * HEAD_DIM

LANES = 128
SUBLANES = 8
KEY_TILE = 256
Q_TILE = 256
ROW_TILE = 512
FF_CHUNK = 512
PROJ_COLS = 3 * N_HEADS * HEAD_DIM
VMEM_LIMIT = 56 * 1024 * 1024
QUERY_SCALE = HEAD_DIM ** -0.5 * math.log2(math.e)

F32 = jnp.float32
BF16 = jnp.bfloat16


def _nt(a, b):
    return lax.dot_general(a, b, (((1,), (1,)), ((), ())), preferred_element_type=F32)


def _nn(a, b):
    return jnp.dot(a, b, preferred_element_type=F32)


def _split_bf16(x):
    hi = x.astype(BF16)
    lo = (x - hi.astype(F32)).astype(BF16)
    return hi, lo


def _rms(x, g):
    ms = jnp.mean(x * x, axis=-1, keepdims=True)
    return x * lax.rsqrt(ms + NORM_EPS) * g


def _const_spec(shape):
    nd = len(shape)
    return pl.BlockSpec(shape, lambda *_: (0,) * nd, pipeline_mode=pl.Buffered(1))


def _proj_kernel(x_ref, g_ref, w_ref, cos_ref, sina_ref, sinb_ref, *rest, rope_slabs, has_gate):
    if has_gate:
        wg_ref, o_ref, og_ref = rest
    else:
        (o_ref,) = rest
    h = _rms(x_ref[...], g_ref[...]).astype(BF16)
    cos = cos_ref[...]
    sina = sina_ref[...]
    sinb = sinb_ref[...]
    n_slabs = len(rope_slabs)
    for c in range(n_slabs // 2):
        y = _nn(h, w_ref[:, c * 2 * LANES:(c + 1) * 2 * LANES])
        for half in range(2):
            s = 2 * c + half
            ys = y[:, half * LANES:(half + 1) * LANES]
            if s < N_HEADS * HEAD_DIM // LANES:
                ys = ys * QUERY_SCALE
            if rope_slabs[s]:
                ys = (ys * cos + pltpu.roll(ys, LANES - HEAD_DIM // 2, 1) * sina
                      + pltpu.roll(ys, HEAD_DIM // 2, 1) * sinb)
            o_ref[:, s * LANES:(s + 1) * LANES] = ys.astype(o_ref.dtype)
    if has_gate:
        og_ref[...] = _nn(h, wg_ref[...])


def _norm_proj(x2, g, w, tables, rope_slabs, seq, wg=None):
    n = x2.shape[0]
    cols = w.shape[1]
    pos_blocks = seq // ROW_TILE
    has_gate = wg is not None
    row = lambda i: (i, 0)
    tab = pl.BlockSpec((ROW_TILE, LANES), lambda i: (i % pos_blocks, 0))
    in_specs = [pl.BlockSpec((ROW_TILE, D_MODEL), row), _const_spec((1, D_MODEL)),
                _const_spec((D_MODEL, cols)), tab, tab, tab]
    args = [x2, g.reshape(1, D_MODEL), w, *tables]
    out_shape = [jax.ShapeDtypeStruct((n, cols), BF16)]
    out_specs = [pl.BlockSpec((ROW_TILE, cols), row)]
    if has_gate:
        in_specs.append(_const_spec((D_MODEL, LANES)))
        args.append(wg)
        out_shape.append(jax.ShapeDtypeStruct((n, LANES), F32))
        out_specs.append(pl.BlockSpec((ROW_TILE, LANES), row))
    outs = pl.pallas_call(
        functools.partial(_proj_kernel, rope_slabs=rope_slabs, has_gate=has_gate),
        grid=(n // ROW_TILE,),
        in_specs=in_specs,
        out_specs=out_specs,
        out_shape=out_shape,
        compiler_params=pltpu.CompilerParams(dimension_semantics=("arbitrary",),
                                             vmem_limit_bytes=VMEM_LIMIT),
        name="norm_proj",
    )(*args)
    return outs if has_gate else outs[0]


def _tail_kernel(o_ref, x_ref, wout_ref, gpost_ref, gpre_ref, wup_ref, wdown_ref, gmlp_ref, out_ref):
    m = _nn(o_ref[...], wout_ref[...])
    x1 = x_ref[...] + _rms(m, gpost_ref[...])
    h = _rms(x1, gpre_ref[...]).astype(BF16)
    acc = jnp.zeros((ROW_TILE, D_MODEL), F32)
    for j in range(D_FF // FF_CHUNK):
        u = _nn(h, wup_ref[:, j * FF_CHUNK:(j + 1) * FF_CHUNK])
        a = jnp.square(jnp.maximum(u, 0.0)).astype(BF16)
        acc = acc + _nn(a, wdown_ref[j * FF_CHUNK:(j + 1) * FF_CHUNK, :])
    out_ref[...] = x1 + _rms(acc, gmlp_ref[...])


def _layer_tail(o2, x2, wout, gpost, gpre, wup, wdown, gmlp):
    n = x2.shape[0]
    row = lambda i: (i, 0)
    vec = lambda v: v.reshape(1, D_MODEL)
    return pl.pallas_call(
        _tail_kernel,
        grid=(n // ROW_TILE,),
        in_specs=[pl.BlockSpec((ROW_TILE, N_HEADS * HEAD_DIM), row),
                  pl.BlockSpec((ROW_TILE, D_MODEL), row),
                  _const_spec((N_HEADS * HEAD_DIM, D_MODEL)), _const_spec((1, D_MODEL)),
                  _const_spec((1, D_MODEL)), _const_spec((D_MODEL, D_FF)),
                  _const_spec((D_FF, D_MODEL)), _const_spec((1, D_MODEL))],
        out_specs=pl.BlockSpec((ROW_TILE, D_MODEL), row),
        out_shape=jax.ShapeDtypeStruct((n, D_MODEL), F32),
        compiler_params=pltpu.CompilerParams(dimension_semantics=("arbitrary",),
                                             vmem_limit_bytes=VMEM_LIMIT),
        name="layer_tail",
    )(o2, x2, wout, vec(gpost), vec(gpre), wup, wdown, vec(gmlp))


def _pair_queries(q):
    lane = lax.broadcasted_iota(jnp.int32, q.shape, 1)
    zero = jnp.zeros_like(q)
    return jnp.concatenate([jnp.where(lane < HEAD_DIM, q, zero),
                            jnp.where(lane >= HEAD_DIM, q, zero)], axis=0)


def _transpose_value_tiles(v_ref, vt_ref, n_tiles):
    for n in range(n_tiles):
        blk = v_ref[n * KEY_TILE:(n + 1) * KEY_TILE, :].astype(F32)
        vt_ref[n] = blk.T.astype(BF16)


def _fold_rows(x, op):
    r, w = x.shape
    return op(x.reshape(r // SUBLANES, SUBLANES, w), axis=0)


def _pair_pv(vt, p):
    t = p.shape[1] // 2
    return _nn(vt[:HEAD_DIM, :], p[:, :t]), _nn(vt[HEAD_DIM:, :], p[:, t:])


def _pair_output(acc_a, acc_b, inv_l=None):
    if inv_l is not None:
        t = acc_a.shape[1]
        acc_a = acc_a * inv_l[:, :t]
        acc_b = acc_b * inv_l[:, t:]
    return jnp.concatenate([acc_a, acc_b], axis=0).T


def _moba_kernel(q_ref, k_ref, v_ref, o_ref, kmean_ref, vt_ref, s_ref, *, n_blocks):
    w2 = 2 * Q_TILE
    for n in range(n_blocks):
        blk = k_ref[n * MOBA_BLOCK:(n + 1) * MOBA_BLOCK, :].astype(F32)
        kmean_ref[n:n + 1, :] = jnp.mean(blk, axis=0, keepdims=True)
    _transpose_value_tiles(v_ref, vt_ref, n_blocks)
    km_hi, km_lo = _split_bf16(kmean_ref[...])
    kpos = lax.broadcasted_iota(jnp.int32, (KEY_TILE, w2), 0)
    tpos = lax.broadcasted_iota(jnp.int32, (KEY_TILE, w2), 1) % Q_TILE
    causal = kpos <= tpos

    def begin(i):
        qcat = _pair_queries(q_ref[i * Q_TILE:(i + 1) * Q_TILE, :])
        chosen = [None] * i
        if i > MOBA_TOPK:
            gate = _nt(km_hi, qcat) + _nt(km_lo, qcat)
            blk_id = lax.broadcasted_iota(jnp.int32, gate.shape, 0)
            g = jnp.where(blk_id < i, gate, NEG_INF)
            sel = jnp.zeros(gate.shape, F32)
            for _ in range(MOBA_TOPK):
                mx = jnp.max(g, axis=0, keepdims=True)
                first = jnp.min(jnp.where(g == mx, blk_id, n_blocks), axis=0, keepdims=True)
                hit = blk_id == first
                sel = jnp.where(hit, 1.0, sel)
                g = jnp.where(hit, -jnp.inf, g)
            chosen = [sel[n:n + 1, :] > 0.5 for n in range(i)]
        return dict(i=i, qcat=qcat, chosen=chosen, m8=None, l8=jnp.zeros((SUBLANES, w2), F32),
                    acc_a=jnp.zeros((HEAD_DIM, Q_TILE), F32), acc_b=jnp.zeros((HEAD_DIM, Q_TILE), F32))

    def score_tile(st, n):
        i = st["i"]
        s = _nt(k_ref[n * KEY_TILE:(n + 1) * KEY_TILE, :], st["qcat"])
        if n == i:
            s = jnp.where(causal, s, NEG_INF)
        s_ref[i % 2, n] = s
        bm = _fold_rows(s, jnp.max)
        if n < i and st["chosen"][n] is not None:
            bm = jnp.where(st["chosen"][n], bm, NEG_INF)
        st["m8"] = bm if st["m8"] is None else jnp.maximum(st["m8"], bm)

    def weight_tile(st, n):
        i = st["i"]
        keep = None if n == i else st["chosen"][n]
        shift = st["m"] if keep is None else jnp.where(keep, st["m"], POS_BIG)
        p = jnp.exp2(s_ref[i % 2, n] - shift)
        st["l8"] = st["l8"] + _fold_rows(p, jnp.sum)
        pv_a, pv_b = _pair_pv(vt_ref[n], p.astype(BF16))
        st["acc_a"] = st["acc_a"] + pv_a
        st["acc_b"] = st["acc_b"] + pv_b

    cur = begin(0)
    score_tile(cur, 0)
    for i in range(n_blocks):
        cur["m"] = jnp.max(cur["m8"], axis=0, keepdims=True)
        nxt = begin(i + 1) if i + 1 < n_blocks else None
        for n in range(i + 2):
            if nxt is not None:
                score_tile(nxt, n)
            if n <= i:
                weight_tile(cur, n)
        l = jnp.sum(cur["l8"], axis=0, keepdims=True)
        o_ref[i * Q_TILE:(i + 1) * Q_TILE, :] = _pair_output(
            cur["acc_a"], cur["acc_b"], 1.0 / l).astype(o_ref.dtype)
        cur = nxt


def _moba_attention(qkv, batch, seq):
    n = qkv.shape[0]
    n_blocks = seq // MOBA_BLOCK
    pairs = N_HEADS // 2
    return pl.pallas_call(
        functools.partial(_moba_kernel, n_blocks=n_blocks),
        grid=(batch, pairs),
        in_specs=[pl.BlockSpec((seq, LANES), lambda b, p: (b, p)),
                  pl.BlockSpec((seq, LANES), lambda b, p: (b, pairs + p)),
                  pl.BlockSpec((seq, LANES), lambda b, p: (b, 2 * pairs + p))],
        out_specs=pl.BlockSpec((seq, LANES), lambda b, p: (b, p)),
        out_shape=jax.ShapeDtypeStruct((n, N_HEADS * HEAD_DIM), BF16),
        scratch_shapes=[pltpu.VMEM((n_blocks, LANES), F32),
                        pltpu.VMEM((n_blocks, LANES, KEY_TILE), BF16),
                        pltpu.VMEM((2, n_blocks, KEY_TILE, 2 * Q_TILE), F32)],
        compiler_params=pltpu.CompilerParams(
            dimension_semantics=("arbitrary", "arbitrary"),
            vmem_limit_bytes=VMEM_LIMIT),
        name="moba_attention",
    )(qkv, qkv, qkv)


def _log2_sigmoid_pair(z):
    sign = jnp.asarray(0x80000000, jnp.uint32)
    neg_abs = lax.bitcast_convert_type(lax.bitcast_convert_type(z, jnp.uint32) | sign, F32)
    log_beta = jnp.minimum(z, 0.0) - jnp.log2(1.0 + jnp.exp2(neg_abs))
    return log_beta, log_beta - z


def _sb_kernel(q_ref, k_ref, v_ref, o_ref, vt_ref, *, n_tiles):
    w2 = 2 * Q_TILE
    _transpose_value_tiles(v_ref, vt_ref, n_tiles)
    r = lax.broadcasted_iota(jnp.int32, (KEY_TILE, KEY_TILE), 0)
    c = lax.broadcasted_iota(jnp.int32, (KEY_TILE, KEY_TILE), 1)
    later = jnp.where(c > r, 1.0, 0.0).astype(BF16)
    kpos = lax.broadcasted_iota(jnp.int32, (KEY_TILE, w2), 0)
    tpos = lax.broadcasted_iota(jnp.int32, (KEY_TILE, w2), 1) % Q_TILE
    strict = kpos < tpos

    state = [dict(qcat=None, carry=None, acc_a=jnp.zeros((HEAD_DIM, Q_TILE), F32),
                  acc_b=jnp.zeros((HEAD_DIM, Q_TILE), F32)) for _ in range(n_tiles)]
    tiles = [dict(i=i, n=n) for i in range(n_tiles) for n in range(i, -1, -1)]

    def scores(t):
        st = state[t["i"]]
        if st["qcat"] is None:
            st["qcat"] = _pair_queries(q_ref[t["i"] * Q_TILE:(t["i"] + 1) * Q_TILE, :])
        t["z"] = _nt(k_ref[t["n"] * KEY_TILE:(t["n"] + 1) * KEY_TILE, :], st["qcat"])

    def log_terms(t):
        st = state[t["i"]]
        log_beta, log_1m = _log2_sigmoid_pair(t.pop("z"))
        if t["n"] == t["i"]:
            log_1m = jnp.where(strict, log_1m, 0.0)
        t["carry"] = st["carry"]
        if t["n"] > 0:
            tile_sum = jnp.sum(log_1m, axis=0, keepdims=True)
            st["carry"] = tile_sum if st["carry"] is None else st["carry"] + tile_sum
        t["log_beta"] = log_beta
        t["suffix"] = _nn(later, log_1m.astype(BF16))

    def weights(t):
        st = state[t["i"]]
        expo = t.pop("log_beta") + t.pop("suffix")
        if t["carry"] is not None:
            expo = expo + t["carry"]
        wgt = jnp.exp2(expo)
        if t["n"] == t["i"]:
            wgt = jnp.where(strict, wgt, 0.0)
        pv_a, pv_b = _pair_pv(vt_ref[t["n"]], wgt.astype(BF16))
        st["acc_a"] = st["acc_a"] + pv_a
        st["acc_b"] = st["acc_b"] + pv_b
        if t["n"] == 0:
            i = t["i"]
            o_ref[i * Q_TILE:(i + 1) * Q_TILE, :] = _pair_output(
                st["acc_a"], st["acc_b"]).astype(o_ref.dtype)

    stages = (scores, log_terms, weights)
    for step in range(len(tiles) + len(stages) - 1):
        for lag, stage in enumerate(stages):
            if 0 <= step - lag < len(tiles):
                stage(tiles[step - lag])


def _sb_attention(qkv, batch, seq):
    n = qkv.shape[0]
    n_tiles = seq // KEY_TILE
    pairs = N_HEADS // 2
    return pl.pallas_call(
        functools.partial(_sb_kernel, n_tiles=n_tiles),
        grid=(batch, pairs),
        in_specs=[pl.BlockSpec((seq, LANES), lambda b, p: (b, p)),
                  pl.BlockSpec((seq, LANES), lambda b, p: (b, pairs + p)),
                  pl.BlockSpec((seq, LANES), lambda b, p: (b, 2 * pairs + p))],
        out_specs=pl.BlockSpec((seq, LANES), lambda b, p: (b, p)),
        out_shape=jax.ShapeDtypeStruct((n, N_HEADS * HEAD_DIM), BF16),
        scratch_shapes=[pltpu.VMEM((n_tiles, LANES, KEY_TILE), BF16)],
        compiler_params=pltpu.CompilerParams(
            dimension_semantics=("arbitrary", "arbitrary"),
            vmem_limit_bytes=VMEM_LIMIT),
        name="stick_breaking_attention",
    )(qkv, qkv, qkv)


def _nsa_compress_kernel(ak_ref, av_ref, posk_ref, posv_ref, w1k_ref, w1v_ref, w2k_ref, w2v_ref,
                         kc_ref, vct_ref):
    half = NSA_CMP_STRIDE * HEAD_DIM
    n_rows = ak_ref.shape[1]

    def compress(a_ref, pos_ref, w1_ref, w2_ref):
        a = a_ref[0].astype(F32)
        first = (a + pos_ref[0:1, :]).astype(BF16)
        second = (a + pos_ref[1:2, :]).astype(BF16)
        pre = _nn(first, w1_ref[:half, :]) + pltpu.roll(_nn(second, w1_ref[half:, :]), n_rows - 1, 0)
        return _nn(jax.nn.gelu(pre).astype(BF16), w2_ref[...])

    kc_ref[0] = compress(ak_ref, posk_ref, w1k_ref, w2k_ref).astype(BF16)
    vct_ref[0] = compress(av_ref, posv_ref, w1v_ref, w2v_ref).T[:HEAD_DIM, :].astype(BF16)


def _nsa_compress(ak, av, posk, posv, w1k, w1v, w2k, w2v):
    bg, rows, feat = ak.shape
    blk = pl.BlockSpec((1, rows, feat), lambda i: (i, 0, 0))
    return pl.pallas_call(
        _nsa_compress_kernel,
        grid=(bg,),
        in_specs=[blk, blk, _const_spec(posk.shape), _const_spec(posv.shape),
                  _const_spec(w1k.shape), _const_spec(w1v.shape),
                  _const_spec(w2k.shape), _const_spec(w2v.shape)],
        out_specs=[pl.BlockSpec((1, rows, LANES), lambda i: (i, 0, 0)),
                   pl.BlockSpec((1, HEAD_DIM, rows), lambda i: (i, 0, 0))],
        out_shape=[jax.ShapeDtypeStruct((bg, rows, LANES), BF16),
                   jax.ShapeDtypeStruct((bg, HEAD_DIM, rows), BF16)],
        compiler_params=pltpu.CompilerParams(dimension_semantics=("arbitrary",),
                                             vmem_limit_bytes=VMEM_LIMIT),
        name="nsa_compress",
    )(ak, av, posk, posv, w1k, w1v, w2k, w2v)


def _nsa_kernel(q_ref, kcmp_ref, vcmpt_ref, ks_ref, kw_ref, vsw_ref, gate_ref, ovl_ref, o_ref,
                vst_ref, vwt_ref, sel_ref, gt_ref, m_ref, l_ref, acc_ref, *, n_tiles, n_sel):
    grp = pl.program_id(1)
    qi = pl.program_id(2)
    hg = NSA_HEADS_PER_GROUP
    t = Q_TILE
    wq = hg * t
    sub = KEY_TILE // NSA_SLC_BLOCK

    @pl.when(qi == 0)
    def _():
        for n in range(n_tiles):
            blk = vsw_ref[n * KEY_TILE:(n + 1) * KEY_TILE, :].astype(F32).T.astype(BF16)
            vst_ref[n] = blk[:HEAD_DIM, :]
            vwt_ref[n] = blk[HEAD_DIM:, :]

    qcat = jnp.concatenate([_pair_queries(q_ref[:, :LANES]),
                            _pair_queries(q_ref[:, LANES:])], axis=0)
    tq = qi * t + lax.broadcasted_iota(jnp.int32, (1, wq), 1) % t

    s = _nt(kcmp_ref[0], qcat)
    cmp_end = lax.broadcasted_iota(jnp.int32, s.shape, 0) * NSA_CMP_STRIDE + (NSA_CMP_BLOCK - 1)
    valid = cmp_end <= tq
    m = jnp.max(jnp.where(valid, s, NEG_INF), axis=0, keepdims=True)
    e = jnp.where(valid, jnp.exp2(s - m), 0.0)
    l = jnp.sum(e, axis=0, keepdims=True)
    p_cmp = e * jnp.where(l > 0.0, 1.0 / l, 0.0)
    o_cmp = _nn(vcmpt_ref[0], p_cmp.astype(BF16))

    p_grp = p_cmp[:, :t]
    for h in range(1, hg):
        p_grp = p_grp + p_cmp[:, h * t:(h + 1) * t]
    p_hi, p_lo = _split_bf16(p_grp)
    imp = _nn(ovl_ref[...], p_hi) + _nn(ovl_ref[...], p_lo)
    j = lax.broadcasted_iota(jnp.int32, imp.shape, 0)
    q_blk = (qi * t + lax.broadcasted_iota(jnp.int32, imp.shape, 1)) // NSA_SLC_BLOCK
    forced = jnp.logical_or(j == 0, jnp.logical_or(j == q_blk, j == q_blk - 1))
    score = jnp.where(j <= q_blk, jnp.where(forced, POS_BIG, imp), NEG_INF)
    sel = jnp.zeros(imp.shape, F32)
    for _ in range(min(NSA_SLC_TOPN, n_sel)):
        mx = jnp.max(score, axis=0, keepdims=True)
        first = jnp.min(jnp.where(score == mx, j, n_sel), axis=0, keepdims=True)
        hit = j == first
        sel = jnp.where(jnp.logical_and(hit, mx > 0.5 * NEG_INF), 1.0, sel)
        score = jnp.where(hit, -jnp.inf, score)
    sel_ref[...] = jnp.concatenate([sel] * hg, axis=1)

    def softmax_step(s, v_tile):
        m_old = m_ref[...]
        m_new = jnp.maximum(m_old, jnp.max(s, axis=0, keepdims=True))
        p = jnp.exp2(s - m_new)
        alpha = jnp.exp2(m_old - m_new)
        l_ref[...] = alpha * l_ref[...] + jnp.sum(p, axis=0, keepdims=True)
        acc_ref[...] = acc_ref[...] * alpha + _nn(v_tile, p.astype(BF16))
        m_ref[...] = m_new

    def reset_state():
        m_ref[...] = jnp.full(m_ref.shape, NEG_INF, F32)
        l_ref[...] = jnp.zeros(l_ref.shape, F32)
        acc_ref[...] = jnp.zeros(acc_ref.shape, F32)

    def mask_unselected(s, n):
        parts = []
        for jj in range(sub):
            chosen = sel_ref[pl.ds(n * sub + jj, 1), :] > 0.5
            parts.append(s[jj * NSA_SLC_BLOCK:(jj + 1) * NSA_SLC_BLOCK, :]
                         + jnp.where(chosen, 0.0, NEG_INF))
        return jnp.concatenate(parts, axis=0)

    kloc = lax.broadcasted_iota(jnp.int32, (KEY_TILE, wq), 0)
    tloc = lax.broadcasted_iota(jnp.int32, (KEY_TILE, wq), 1) % t
    own = pl.multiple_of(qi * KEY_TILE, KEY_TILE)

    reset_state()

    def sel_tile(n, carry):
        off = pl.multiple_of(n * KEY_TILE, KEY_TILE)
        softmax_step(mask_unselected(_nt(ks_ref[pl.ds(off, KEY_TILE), :], qcat), n), vst_ref[n])
        return carry

    lax.fori_loop(0, qi, sel_tile, 0)
    s_own = mask_unselected(_nt(ks_ref[pl.ds(own, KEY_TILE), :], qcat), qi)
    softmax_step(jnp.where(kloc <= tloc, s_own, NEG_INF), vst_ref[qi])
    o_slc = acc_ref[...] * (1.0 / l_ref[...])

    reset_state()
    s_own = _nt(kw_ref[pl.ds(own, KEY_TILE), :], qcat)
    softmax_step(jnp.where(kloc <= tloc, s_own, NEG_INF), vwt_ref[qi])

    @pl.when(qi >= 1)
    def _():
        off = pl.multiple_of((qi - 1) * KEY_TILE, KEY_TILE)
        softmax_step(_nt(kw_ref[pl.ds(off, KEY_TILE), :], qcat), vwt_ref[qi - 1])

    @pl.when(qi >= NSA_WINDOW // KEY_TILE)
    def _():
        back = NSA_WINDOW // KEY_TILE
        off = pl.multiple_of((qi - back) * KEY_TILE, KEY_TILE)
        s_back = _nt(kw_ref[pl.ds(off, KEY_TILE), :], qcat)
        softmax_step(jnp.where(kloc > tloc, s_back, NEG_INF), vwt_ref[qi - back])

    o_win = acc_ref[...] * (1.0 / l_ref[...])

    gt_ref[...] = jax.nn.sigmoid(gate_ref[...]).T
    gates = gt_ref[pl.ds(pl.multiple_of(grp * 16, 16), 16), :]
    heads = []
    for h in range(hg):
        cols = slice(h * t, (h + 1) * t)
        heads.append(gates[3 * h:3 * h + 1, :] * o_cmp[:, cols]
                     + gates[3 * h + 1:3 * h + 2, :] * o_slc[:, cols]
                     + gates[3 * h + 2:3 * h + 3, :] * o_win[:, cols])
    for pair in range(hg // 2):
        both = jnp.concatenate([heads[2 * pair], heads[2 * pair + 1]], axis=0)
        o_ref[:, pair * LANES:(pair + 1) * LANES] = both.T.astype(o_ref.dtype)


def _nsa_attention(proj, gates, kcmp, vcmpt, overlap_t, batch, seq):
    n = proj.shape[0]
    nq = seq // Q_TILE
    n_tiles = seq // KEY_TILE
    n_sel = seq // NSA_SLC_BLOCK
    g = NSA_KV_GROUPS
    hg = NSA_HEADS_PER_GROUP
    n_cmp_pad = kcmp.shape[1]
    ks0, kw0, vsw0 = 12, 16, 20
    return pl.pallas_call(
        functools.partial(_nsa_kernel, n_tiles=n_tiles, n_sel=n_sel),
        grid=(batch, g, nq),
        in_specs=[pl.BlockSpec((Q_TILE, hg * HEAD_DIM), lambda b, k, i: (b * nq + i, k)),
                  pl.BlockSpec((1, n_cmp_pad, LANES), lambda b, k, i: (b * g + k, 0, 0)),
                  pl.BlockSpec((1, HEAD_DIM, n_cmp_pad), lambda b, k, i: (b * g + k, 0, 0)),
                  pl.BlockSpec((seq, LANES), lambda b, k, i: (b, ks0 + k)),
                  pl.BlockSpec((seq, LANES), lambda b, k, i: (b, kw0 + k)),
                  pl.BlockSpec((seq, LANES), lambda b, k, i: (b, vsw0 + k)),
                  pl.BlockSpec((Q_TILE, LANES), lambda b, k, i: (b * nq + i, 0)),
                  _const_spec(overlap_t.shape)],
        out_specs=pl.BlockSpec((Q_TILE, hg * HEAD_DIM), lambda b, k, i: (b * nq + i, k)),
        out_shape=jax.ShapeDtypeStruct((n, N_HEADS * HEAD_DIM), BF16),
        scratch_shapes=[pltpu.VMEM((n_tiles, HEAD_DIM, KEY_TILE), BF16),
                        pltpu.VMEM((n_tiles, HEAD_DIM, KEY_TILE), BF16),
                        pltpu.VMEM((n_sel, hg * Q_TILE), F32),
                        pltpu.VMEM((LANES, Q_TILE), F32),
                        pltpu.VMEM((1, hg * Q_TILE), F32),
                        pltpu.VMEM((1, hg * Q_TILE), F32),
                        pltpu.VMEM((HEAD_DIM, hg * Q_TILE), F32)],
        compiler_params=pltpu.CompilerParams(
            dimension_semantics=("arbitrary", "arbitrary", "arbitrary"),
            vmem_limit_bytes=VMEM_LIMIT),
        name="nsa_attention",
    )(proj, kcmp, vcmpt, proj, proj, proj, gates, overlap_t)


def _rope_tables(seq):
    inv_freq = 1.0 / (ROPE_THETA ** (jnp.arange(0, HEAD_DIM, 2, dtype=F32) / HEAD_DIM))
    ang = jnp.arange(seq, dtype=F32)[:, None] * inv_freq[None, :]
    reps = LANES // (HEAD_DIM // 2)
    cos = jnp.tile(jnp.cos(ang), (1, reps))
    sin = jnp.tile(jnp.sin(ang), (1, reps))
    first_half = (jnp.arange(LANES) % HEAD_DIM) < HEAD_DIM // 2
    return cos, jnp.where(first_half, -sin, 0.0), jnp.where(first_half, 0.0, sin)


def _nsa_weight_layout(w_in):
    qd = N_HEADS * HEAD_DIM
    kv = NSA_KV_COLS
    q, kc, vc, ks, vs, kw, vw, gt = jnp.split(
        w_in, np.cumsum([qd] + [kv] * 6).tolist(), axis=1)

    def dup(w):
        w = w.reshape(D_MODEL, NSA_KV_GROUPS, 1, HEAD_DIM)
        return jnp.broadcast_to(w, (D_MODEL, NSA_KV_GROUPS, 2, HEAD_DIM)).reshape(D_MODEL, -1)

    vsw = jnp.concatenate([vs.reshape(D_MODEL, NSA_KV_GROUPS, 1, HEAD_DIM),
                           vw.reshape(D_MODEL, NSA_KV_GROUPS, 1, HEAD_DIM)], axis=2)
    main = jnp.concatenate([q, kc, vc, dup(ks), dup(kw), vsw.reshape(D_MODEL, -1)], axis=1)
    per_group = 3 * NSA_HEADS_PER_GROUP
    gt = gt.reshape(D_MODEL, NSA_KV_GROUPS, per_group)
    gt = jnp.pad(gt, ((0, 0), (0, 0), (0, 16 - per_group))).reshape(D_MODEL, -1)
    gt = jnp.pad(gt, ((0, 0), (0, LANES - gt.shape[1])))
    return main.astype(BF16), gt.astype(BF16)


def _overlap_t(seq):
    n_cmp = (seq - NSA_CMP_BLOCK) // NSA_CMP_STRIDE + 1
    n_sel = seq // NSA_SLC_BLOCK
    c_start = np.arange(n_cmp) * NSA_CMP_STRIDE
    s_start = np.arange(n_sel) * NSA_SLC_BLOCK
    lo = np.maximum(c_start[:, None], s_start[None, :])
    hi = np.minimum(c_start[:, None] + NSA_CMP_BLOCK, s_start[None, :] + NSA_SLC_BLOCK)
    ov = np.clip(hi - lo, 0, None) / NSA_CMP_BLOCK
    ov = np.pad(ov, ((0, seq // NSA_CMP_STRIDE - n_cmp), (0, 0)))
    return jnp.asarray(ov.T, dtype=BF16)


_QK_ROPE = (True,) * 16 + (False,) * 8
_NO_ROPE = (False,) * 24
_NSA_ROPE = (True,) * 10 + (False,) * 2 + (True,) * 8 + (False,) * 4


def kernel(x, l0_ln_mix_pre, l0_w_in, l0_w_out, l0_ln_mix_post, l0_ln_mlp_pre, l0_w_up, l0_w_down, l0_ln_mlp_post, l1_ln_mix_pre, l1_w_in, l1_w_out, l1_ln_mix_post, l1_ln_mlp_pre, l1_w_up, l1_w_down, l1_ln_mlp_post, l2_ln_mix_pre, l2_w_in, l2_cmp_pos_k, l2_cmp_w1_k, l2_cmp_w2_k, l2_cmp_pos_v, l2_cmp_w1_v, l2_cmp_w2_v, l2_w_out, l2_ln_mix_post, l2_ln_mlp_pre, l2_w_up, l2_w_down, l2_ln_mlp_post, l3_ln_mix_pre, l3_w_in, l3_w_out, l3_ln_mix_post, l3_ln_mlp_pre, l3_w_up, l3_w_down, l3_ln_mlp_post):
    batch, seq, _ = x.shape
    assert seq % ROW_TILE == 0 and seq % KEY_TILE == 0 and Q_TILE == KEY_TILE == MOBA_BLOCK
    tables = _rope_tables(seq)
    x2 = x.reshape(batch * seq, D_MODEL)
    bf = lambda w: w.astype(BF16)

    def tail(o, x2, w_out, g_post, g_pre, w_up, w_down, g_mlp):
        return _layer_tail(o, x2, bf(w_out), g_post, g_pre, bf(w_up), bf(w_down), g_mlp)

    qkv = _norm_proj(x2, l0_ln_mix_pre, bf(l0_w_in), tables, _QK_ROPE, seq)
    x2 = tail(_moba_attention(qkv, batch, seq), x2, l0_w_out, l0_ln_mix_post, l0_ln_mlp_pre,
              l0_w_up, l0_w_down, l0_ln_mlp_post)

    qkv = _norm_proj(x2, l1_ln_mix_pre, bf(l1_w_in), tables, _NO_ROPE, seq)
    x2 = tail(_sb_attention(qkv, batch, seq), x2, l1_w_out, l1_ln_mix_post, l1_ln_mlp_pre,
              l1_w_up, l1_w_down, l1_ln_mlp_post)

    w_main, w_gate = _nsa_weight_layout(l2_w_in)
    proj, gates = _norm_proj(x2, l2_ln_mix_pre, w_main, tables, _NSA_ROPE, seq, wg=w_gate)
    qd = N_HEADS * HEAD_DIM
    rows = seq // NSA_CMP_STRIDE

    def row_groups(cols):
        a = cols.reshape(batch, seq, NSA_KV_GROUPS, HEAD_DIM).transpose(0, 2, 1, 3)
        return a.reshape(batch * NSA_KV_GROUPS, rows, NSA_CMP_STRIDE * HEAD_DIM)

    flat_pos = lambda p: p.reshape(2, NSA_CMP_STRIDE * HEAD_DIM)
    w2_dup = lambda w: bf(jnp.concatenate([w, w], axis=1))
    kcmp, vcmpt = _nsa_compress(
        row_groups(proj[:, qd:qd + NSA_KV_COLS]), row_groups(proj[:, qd + NSA_KV_COLS:qd + 2 * NSA_KV_COLS]),
        flat_pos(l2_cmp_pos_k), flat_pos(l2_cmp_pos_v), bf(l2_cmp_w1_k), bf(l2_cmp_w1_v),
        w2_dup(l2_cmp_w2_k), w2_dup(l2_cmp_w2_v))
    o = _nsa_attention(proj, gates, kcmp, vcmpt, _overlap_t(seq), batch, seq)
    x2 = tail(o, x2, l2_w_out, l2_ln_mix_post, l2_ln_mlp_pre, l2_w_up, l2_w_down, l2_ln_mlp_post)

    qkv = _norm_proj(x2, l3_ln_mix_pre, bf(l3_w_in), tables, _QK_ROPE, seq)
    x2 = tail(_moba_attention(qkv, batch, seq), x2, l3_w_out, l3_ln_mix_post, l3_ln_mlp_pre,
              l3_w_up, l3_w_down, l3_ln_mlp_post)
    return x2.reshape(batch, seq, D_MODEL)
```

```python
import functools
import math

import numpy as np
import jax
import jax.numpy as jnp
from jax import lax
from jax.experimental import pallas as pl
from jax.experimental.pallas import tpu as pltpu

D_MODEL = 1024
N_HEADS = 16
HEAD_DIM = 64
D_FF = 4 * D_MODEL
ROPE_THETA = 10000.0
NORM_EPS = 1e-6
NEG_INF = -1e30
POS_BIG = 1e30

MOBA_BLOCK = 256
MOBA_TOPK = 3

NSA_KV_GROUPS = 4
NSA_HEADS_PER_GROUP = N_HEADS // NSA_KV_GROUPS
NSA_CMP_BLOCK = 32
NSA_CMP_STRIDE = 16
NSA_CMP_HIDDEN = 256
NSA_SLC_BLOCK = 64
NSA_SLC_TOPN = 16
NSA_WINDOW = 512
NSA_KV_COLS = NSA_KV_GROUPS * HEAD_DIM

LANES = 128
SUBLANES = 8
KEY_TILE = 256
Q_TILE = 256
ROW_TILE = 512
FF_CHUNK = 512
PROJ_COLS = 3 * N_HEADS * HEAD_DIM
VMEM_LIMIT = 56 * 1024 * 1024
QUERY_SCALE = HEAD_DIM ** -0.5 * math.log2(math.e)

F32 = jnp.float32
BF16 = jnp.bfloat16


def _nt(a, b):
    return lax.dot_general(a, b, (((1,), (1,)), ((), ())), preferred_element_type=F32)


def _nn(a, b):
    return jnp.dot(a, b, preferred_element_type=F32)


def _split_bf16(x):
    hi = x.astype(BF16)
    lo = (x - hi.astype(F32)).astype(BF16)
    return hi, lo


def _rms(x, g):
    ms = jnp.mean(x * x, axis=-1, keepdims=True)
    return x * lax.rsqrt(ms + NORM_EPS) * g


def _const_spec(shape):
    nd = len(shape)
    return pl.BlockSpec(shape, lambda *_: (0,) * nd, pipeline_mode=pl.Buffered(1))


def _proj_kernel(x_ref, g_ref, w_ref, cos_ref, sina_ref, sinb_ref, *rest, rope_slabs, has_gate):
    if has_gate:
        wg_ref, o_ref, og_ref = rest
    else:
        (o_ref,) = rest
    h = _rms(x_ref[...], g_ref[...]).astype(BF16)
    cos = cos_ref[...]
    sina = sina_ref[...]
    sinb = sinb_ref[...]
    n_slabs = len(rope_slabs)
    for c in range(n_slabs // 2):
        y = _nn(h, w_ref[:, c * 2 * LANES:(c + 1) * 2 * LANES])
        for half in range(2):
            s = 2 * c + half
            ys = y[:, half * LANES:(half + 1) * LANES]
            if s < N_HEADS * HEAD_DIM // LANES:
                ys = ys * QUERY_SCALE
            if rope_slabs[s]:
                ys = (ys * cos + pltpu.roll(ys, LANES - HEAD_DIM // 2, 1) * sina
                      + pltpu.roll(ys, HEAD_DIM // 2, 1) * sinb)
            o_ref[:, s * LANES:(s + 1) * LANES] = ys.astype(o_ref.dtype)
    if has_gate:
        og_ref[...] = _nn(h, wg_ref[...])


def _norm_proj(x2, g, w, tables, rope_slabs, seq, wg=None):
    n = x2.shape[0]
    cols = w.shape[1]
    pos_blocks = seq // ROW_TILE
    has_gate = wg is not None
    row = lambda i: (i, 0)
    tab = pl.BlockSpec((ROW_TILE, LANES), lambda i: (i % pos_blocks, 0))
    in_specs = [pl.BlockSpec((ROW_TILE, D_MODEL), row), _const_spec((1, D_MODEL)),
                _const_spec((D_MODEL, cols)), tab, tab, tab]
    args = [x2, g.reshape(1, D_MODEL), w, *tables]
    out_shape = [jax.ShapeDtypeStruct((n, cols), BF16)]
    out_specs = [pl.BlockSpec((ROW_TILE, cols), row)]
    if has_gate:
        in_specs.append(_const_spec((D_MODEL, LANES)))
        args.append(wg)
        out_shape.append(jax.ShapeDtypeStruct((n, LANES), F32))
        out_specs.append(pl.BlockSpec((ROW_TILE, LANES), row))
    outs = pl.pallas_call(
        functools.partial(_proj_kernel, rope_slabs=rope_slabs, has_gate=has_gate),
        grid=(n // ROW_TILE,),
        in_specs=in_specs,
        out_specs=out_specs,
        out_shape=out_shape,
        compiler_params=pltpu.CompilerParams(dimension_semantics=("arbitrary",),
                                             vmem_limit_bytes=VMEM_LIMIT),
        name="norm_proj",
    )(*args)
    return outs if has_gate else outs[0]


def _tail_kernel(o_ref, x_ref, wout_ref, gpost_ref, gpre_ref, wup_ref, wdown_ref, gmlp_ref, out_ref):
    m = _nn(o_ref[...], wout_ref[...])
    x1 = x_ref[...] + _rms(m, gpost_ref[...])
    h = _rms(x1, gpre_ref[...]).astype(BF16)
    acc = jnp.zeros((ROW_TILE, D_MODEL), F32)
    for j in range(D_FF // FF_CHUNK):
        u = _nn(h, wup_ref[:, j * FF_CHUNK:(j + 1) * FF_CHUNK])
        a = jnp.square(jnp.maximum(u, 0.0)).astype(BF16)
        acc = acc + _nn(a, wdown_ref[j * FF_CHUNK:(j + 1) * FF_CHUNK, :])
    out_ref[...] = x1 + _rms(acc, gmlp_ref[...])


def _layer_tail(o2, x2, wout, gpost, gpre, wup, wdown, gmlp):
    n = x2.shape[0]
    row = lambda i: (i, 0)
    vec = lambda v: v.reshape(1, D_MODEL)
    return pl.pallas_call(
        _tail_kernel,
        grid=(n // ROW_TILE,),
        in_specs=[pl.BlockSpec((ROW_TILE, N_HEADS * HEAD_DIM), row),
                  pl.BlockSpec((ROW_TILE, D_MODEL), row),
                  _const_spec((N_HEADS * HEAD_DIM, D_MODEL)), _const_spec((1, D_MODEL)),
                  _const_spec((1, D_MODEL)), _const_spec((D_MODEL, D_FF)),
                  _const_spec((D_FF, D_MODEL)), _const_spec((1, D_MODEL))],
        out_specs=pl.BlockSpec((ROW_TILE, D_MODEL), row),
        out_shape=jax.ShapeDtypeStruct((n, D_MODEL), F32),
        compiler_params=pltpu.CompilerParams(dimension_semantics=("arbitrary",),
                                             vmem_limit_bytes=VMEM_LIMIT),
        name="layer_tail",
    )(o2, x2, wout, vec(gpost), vec(gpre), wup, wdown, vec(gmlp))


def _pair_queries(q):
    lane = lax.broadcasted_iota(jnp.int32, q.shape, 1)
    zero = jnp.zeros_like(q)
    return jnp.concatenate([jnp.where(lane < HEAD_DIM, q, zero),
                            jnp.where(lane >= HEAD_DIM, q, zero)], axis=0)


def _transpose_value_tiles(v_ref, vt_ref, n_tiles):
    for n in range(n_tiles):
        blk = v_ref[n * KEY_TILE:(n + 1) * KEY_TILE, :].astype(F32)
        vt_ref[n] = blk.T.astype(BF16)


def _fold_rows(x, op):
    r, w = x.shape
    return op(x.reshape(r // SUBLANES, SUBLANES, w), axis=0)


def _pair_pv(vt, p):
    t = p.shape[1] // 2
    return _nn(vt[:HEAD_DIM, :], p[:, :t]), _nn(vt[HEAD_DIM:, :], p[:, t:])


def _pair_output(acc_a, acc_b, inv_l=None):
    if inv_l is not None:
        t = acc_a.shape[1]
        acc_a = acc_a * inv_l[:, :t]
        acc_b = acc_b * inv_l[:, t:]
    return jnp.concatenate([acc_a, acc_b], axis=0).T


def _moba_kernel(q_ref, k_ref, v_ref, o_ref, kmean_ref, vt_ref, s_ref, *, n_blocks):
    w2 = 2 * Q_TILE
    for n in range(n_blocks):
        blk = k_ref[n * MOBA_BLOCK:(n + 1) * MOBA_BLOCK, :].astype(F32)
        kmean_ref[n:n + 1, :] = jnp.mean(blk, axis=0, keepdims=True)
    _transpose_value_tiles(v_ref, vt_ref, n_blocks)
    km_hi, km_lo = _split_bf16(kmean_ref[...])
    kpos = lax.broadcasted_iota(jnp.int32, (KEY_TILE, w2), 0)
    tpos = lax.broadcasted_iota(jnp.int32, (KEY_TILE, w2), 1) % Q_TILE
    causal = kpos <= tpos

    def begin(i):
        qcat = _pair_queries(q_ref[i * Q_TILE:(i + 1) * Q_TILE, :])
        chosen = [None] * i
        if i > MOBA_TOPK:
            gate = _nt(km_hi, qcat) + _nt(km_lo, qcat)
            blk_id = lax.broadcasted_iota(jnp.int32, gate.shape, 0)
            g = jnp.where(blk_id < i, gate, NEG_INF)
            sel = jnp.zeros(gate.shape, F32)
            for _ in range(MOBA_TOPK):
                mx = jnp.max(g, axis=0, keepdims=True)
                first = jnp.min(jnp.where(g == mx, blk_id, n_blocks), axis=0, keepdims=True)
                hit = blk_id == first
                sel = jnp.where(hit, 1.0, sel)
                g = jnp.where(hit, -jnp.inf, g)
            chosen = [sel[n:n + 1, :] > 0.5 for n in range(i)]
        return dict(i=i, qcat=qcat, chosen=chosen, m8=None, l8=jnp.zeros((SUBLANES, w2), F32),
                    acc_a=jnp.zeros((HEAD_DIM, Q_TILE), F32), acc_b=jnp.zeros((HEAD_DIM, Q_TILE), F32))

    def score_tile(st, n):
        i = st["i"]
        s = _nt(k_ref[n * KEY_TILE:(n + 1) * KEY_TILE, :], st["qcat"])
        if n == i:
            s = jnp.where(causal, s, NEG_INF)
        s_ref[i % 2, n] = s
        bm = _fold_rows(s, jnp.max)
        if n < i and st["chosen"][n] is not None:
            bm = jnp.where(st["chosen"][n], bm, NEG_INF)
        st["m8"] = bm if st["m8"] is None else jnp.maximum(st["m8"], bm)

    def weight_tile(st, n):
        i = st["i"]
        keep = None if n == i else st["chosen"][n]
        shift = st["m"] if keep is None else jnp.where(keep, st["m"], POS_BIG)
        p = jnp.exp2(s_ref[i % 2, n] - shift)
        st["l8"] = st["l8"] + _fold_rows(p, jnp.sum)
        pv_a, pv_b = _pair_pv(vt_ref[n], p.astype(BF16))
        st["acc_a"] = st["acc_a"] + pv_a
        st["acc_b"] = st["acc_b"] + pv_b

    cur = begin(0)
    score_tile(cur, 0)
    for i in range(n_blocks):
        cur["m"] = jnp.max(cur["m8"], axis=0, keepdims=True)
        nxt = begin(i + 1) if i + 1 < n_blocks else None
        for n in range(i + 2):
            if nxt is not None:
                score_tile(nxt, n)
            if n <= i:
                weight_tile(cur, n)
        l = jnp.sum(cur["l8"], axis=0, keepdims=True)
        o_ref[i * Q_TILE:(i + 1) * Q_TILE, :] = _pair_output(
            cur["acc_a"], cur["acc_b"], 1.0 / l).astype(o_ref.dtype)
        cur = nxt


def _moba_attention(qkv, batch, seq):
    n = qkv.shape[0]
    n_blocks = seq // MOBA_BLOCK
    pairs = N_HEADS // 2
    return pl.pallas_call(
        functools.partial(_moba_kernel, n_blocks=n_blocks),
        grid=(batch, pairs),
        in_specs=[pl.BlockSpec((seq, LANES), lambda b, p: (b, p)),
                  pl.BlockSpec((seq, LANES), lambda b, p: (b, pairs + p)),
                  pl.BlockSpec((seq, LANES), lambda b, p: (b, 2 * pairs + p))],
        out_specs=pl.BlockSpec((seq, LANES), lambda b, p: (b, p)),
        out_shape=jax.ShapeDtypeStruct((n, N_HEADS * HEAD_DIM), BF16),
        scratch_shapes=[pltpu.VMEM((n_blocks, LANES), F32),
                        pltpu.VMEM((n_blocks, LANES, KEY_TILE), BF16),
                        pltpu.VMEM((2, n_blocks, KEY_TILE, 2 * Q_TILE), F32)],
        compiler_params=pltpu.CompilerParams(
            dimension_semantics=("arbitrary", "arbitrary"),
            vmem_limit_bytes=VMEM_LIMIT),
        name="moba_attention",
    )(qkv, qkv, qkv)


def _log2_sigmoid_pair(z):
    log_beta = jnp.minimum(z, 0.0) - jnp.log2(1.0 + jnp.exp2(-jnp.abs(z)))
    return log_beta, log_beta - z


def _sb_kernel(q_ref, k_ref, v_ref, o_ref, vt_ref, *, n_tiles):
    w2 = 2 * Q_TILE
    _transpose_value_tiles(v_ref, vt_ref, n_tiles)
    r = lax.broadcasted_iota(jnp.int32, (KEY_TILE, KEY_TILE), 0)
    c = lax.broadcasted_iota(jnp.int32, (KEY_TILE, KEY_TILE), 1)
    later = jnp.where(c > r, 1.0, 0.0).astype(BF16)
    kpos = lax.broadcasted_iota(jnp.int32, (KEY_TILE, w2), 0)
    tpos = lax.broadcasted_iota(jnp.int32, (KEY_TILE, w2), 1) % Q_TILE
    strict = kpos < tpos

    state = [dict(qcat=None, carry=None, acc_a=jnp.zeros((HEAD_DIM, Q_TILE), F32),
                  acc_b=jnp.zeros((HEAD_DIM, Q_TILE), F32)) for _ in range(n_tiles)]
    tiles = [dict(i=i, n=n) for i in range(n_tiles) for n in range(i, -1, -1)]

    def scores(t):
        st = state[t["i"]]
        if st["qcat"] is None:
            st["qcat"] = _pair_queries(q_ref[t["i"] * Q_TILE:(t["i"] + 1) * Q_TILE, :])
        t["z"] = _nt(k_ref[t["n"] * KEY_TILE:(t["n"] + 1) * KEY_TILE, :], st["qcat"])

    def log_terms(t):
        st = state[t["i"]]
        log_beta, log_1m = _log2_sigmoid_pair(t.pop("z"))
        if t["n"] == t["i"]:
            log_1m = jnp.where(strict, log_1m, 0.0)
        t["carry"] = st["carry"]
        if t["n"] > 0:
            tile_sum = jnp.sum(log_1m, axis=0, keepdims=True)
            st["carry"] = tile_sum if st["carry"] is None else st["carry"] + tile_sum
        t["log_beta"] = log_beta
        t["suffix"] = _nn(later, log_1m.astype(BF16))

    def weights(t):
        st = state[t["i"]]
        expo = t.pop("log_beta") + t.pop("suffix")
        if t["carry"] is not None:
            expo = expo + t["carry"]
        wgt = jnp.exp2(expo)
        if t["n"] == t["i"]:
            wgt = jnp.where(strict, wgt, 0.0)
        pv_a, pv_b = _pair_pv(vt_ref[t["n"]], wgt.astype(BF16))
        st["acc_a"] = st["acc_a"] + pv_a
        st["acc_b"] = st["acc_b"] + pv_b
        if t["n"] == 0:
            i = t["i"]
            o_ref[i * Q_TILE:(i + 1) * Q_TILE, :] = _pair_output(
                st["acc_a"], st["acc_b"]).astype(o_ref.dtype)

    stages = (scores, log_terms, weights)
    for step in range(len(tiles) + len(stages) - 1):
        for lag, stage in enumerate(stages):
            if 0 <= step - lag < len(tiles):
                stage(tiles[step - lag])


def _sb_attention(qkv, batch, seq):
    n = qkv.shape[0]
    n_tiles = seq // KEY_TILE
    pairs = N_HEADS // 2
    return pl.pallas_call(
        functools.partial(_sb_kernel, n_tiles=n_tiles),
        grid=(batch, pairs),
        in_specs=[pl.BlockSpec((seq, LANES), lambda b, p: (b, p)),
                  pl.BlockSpec((seq, LANES), lambda b, p: (b, pairs + p)),
                  pl.BlockSpec((seq, LANES), lambda b, p: (b, 2 * pairs + p))],
        out_specs=pl.BlockSpec((seq, LANES), lambda b, p: (b, p)),
        out_shape=jax.ShapeDtypeStruct((n, N_HEADS * HEAD_DIM), BF16),
        scratch_shapes=[pltpu.VMEM((n_tiles, LANES, KEY_TILE), BF16)],
        compiler_params=pltpu.CompilerParams(
            dimension_semantics=("arbitrary", "arbitrary"),
            vmem_limit_bytes=VMEM_LIMIT),
        name="stick_breaking_attention",
    )(qkv, qkv, qkv)


def _nsa_compress_kernel(ak_ref, av_ref, posk_ref, posv_ref, w1k_ref, w1v_ref, w2k_ref, w2v_ref,
                         kc_ref, vct_ref):
    half = NSA_CMP_STRIDE * HEAD_DIM
    n_rows = ak_ref.shape[1]

    def compress(a_ref, pos_ref, w1_ref, w2_ref):
        a = a_ref[0].astype(F32)
        first = (a + pos_ref[0:1, :]).astype(BF16)
        second = (a + pos_ref[1:2, :]).astype(BF16)
        pre = _nn(first, w1_ref[:half, :]) + pltpu.roll(_nn(second, w1_ref[half:, :]), n_rows - 1, 0)
        return _nn(jax.nn.gelu(pre).astype(BF16), w2_ref[...])

    kc_ref[0] = compress(ak_ref, posk_ref, w1k_ref, w2k_ref).astype(BF16)
    vct_ref[0] = compress(av_ref, posv_ref, w1v_ref, w2v_ref).T[:HEAD_DIM, :].astype(BF16)


def _nsa_compress(ak, av, posk, posv, w1k, w1v, w2k, w2v):
    bg, rows, feat = ak.shape
    blk = pl.BlockSpec((1, rows, feat), lambda i: (i, 0, 0))
    return pl.pallas_call(
        _nsa_compress_kernel,
        grid=(bg,),
        in_specs=[blk, blk, _const_spec(posk.shape), _const_spec(posv.shape),
                  _const_spec(w1k.shape), _const_spec(w1v.shape),
                  _const_spec(w2k.shape), _const_spec(w2v.shape)],
        out_specs=[pl.BlockSpec((1, rows, LANES), lambda i: (i, 0, 0)),
                   pl.BlockSpec((1, HEAD_DIM, rows), lambda i: (i, 0, 0))],
        out_shape=[jax.ShapeDtypeStruct((bg, rows, LANES), BF16),
                   jax.ShapeDtypeStruct((bg, HEAD_DIM, rows), BF16)],
        compiler_params=pltpu.CompilerParams(dimension_semantics=("arbitrary",),
                                             vmem_limit_bytes=VMEM_LIMIT),
        name="nsa_compress",
    )(ak, av, posk, posv, w1k, w1v, w2k, w2v)


def _nsa_kernel(q_ref, kcmp_ref, vcmpt_ref, ks_ref, kw_ref, vsw_ref, gate_ref, ovl_ref, o_ref,
                vst_ref, vwt_ref, gt_ref, ssel_ref, swin_ref, *, n_tiles, n_sel):
    grp = pl.program_id(1)
    hg = NSA_HEADS_PER_GROUP
    t = Q_TILE
    wq = hg * t
    sub = KEY_TILE // NSA_SLC_BLOCK
    back = NSA_WINDOW // KEY_TILE
    n_top = min(NSA_SLC_TOPN, n_sel)

    for n in range(n_tiles):
        rows = slice(n * KEY_TILE, (n + 1) * KEY_TILE)
        blk = vsw_ref[rows, :].astype(F32).T.astype(BF16)
        vst_ref[n] = blk[:HEAD_DIM, :]
        vwt_ref[n] = blk[HEAD_DIM:, :]
        gt_ref[:, rows] = jax.nn.sigmoid(gate_ref[rows, :]).T
    kloc = lax.broadcasted_iota(jnp.int32, (KEY_TILE, wq), 0)
    tloc = lax.broadcasted_iota(jnp.int32, (KEY_TILE, wq), 1) % t
    causal = kloc <= tloc
    window_tail = kloc > tloc

    def begin(i):
        rows = slice(i * t, (i + 1) * t)
        qcat = jnp.concatenate([_pair_queries(q_ref[rows, :LANES]),
                                _pair_queries(q_ref[rows, LANES:])], axis=0)
        tq = i * t + lax.broadcasted_iota(jnp.int32, (1, wq), 1) % t

        s = _nt(kcmp_ref[0], qcat)
        cmp_end = lax.broadcasted_iota(jnp.int32, s.shape, 0) * NSA_CMP_STRIDE + (NSA_CMP_BLOCK - 1)
        valid = cmp_end <= tq
        m = jnp.max(jnp.where(valid, s, NEG_INF), axis=0, keepdims=True)
        e = jnp.where(valid, jnp.exp2(s - m), 0.0)
        l = jnp.sum(e, axis=0, keepdims=True)
        p_cmp = e * jnp.where(l > 0.0, 1.0 / l, 0.0)
        o_cmp = _nn(vcmpt_ref[0], p_cmp.astype(BF16))

        bias = None
        if sub * (i + 1) > n_top:
            p_grp = p_cmp[:, :t]
            for h in range(1, hg):
                p_grp = p_grp + p_cmp[:, h * t:(h + 1) * t]
            p_hi, p_lo = _split_bf16(p_grp)
            imp = _nn(ovl_ref[...], p_hi) + _nn(ovl_ref[...], p_lo)
            j = lax.broadcasted_iota(jnp.int32, imp.shape, 0)
            q_blk = (i * t + lax.broadcasted_iota(jnp.int32, imp.shape, 1)) // NSA_SLC_BLOCK
            forced = jnp.logical_or(j == 0, jnp.logical_or(j == q_blk, j == q_blk - 1))
            score = jnp.where(j <= q_blk, jnp.where(forced, POS_BIG, imp), NEG_INF)
            sel = jnp.zeros(imp.shape, F32)
            for _ in range(n_top):
                mx = jnp.max(score, axis=0, keepdims=True)
                first = jnp.min(jnp.where(score == mx, j, n_sel), axis=0, keepdims=True)
                hit = j == first
                sel = jnp.where(jnp.logical_and(hit, mx > 0.5 * NEG_INF), 1.0, sel)
                score = jnp.where(hit, -jnp.inf, score)
            bias = jnp.concatenate([jnp.where(sel > 0.5, 0.0, NEG_INF)] * hg, axis=1)
        zeros8 = jnp.zeros((SUBLANES, wq), F32)
        zeros_acc = jnp.zeros((HEAD_DIM, wq), F32)
        return dict(i=i, qcat=qcat, o_cmp=o_cmp, bias=bias,
                    sel=dict(m8=None, l8=zeros8, acc=zeros_acc),
                    win=dict(m8=None, l8=zeros8, acc=zeros_acc))

    def note_max(br, s):
        bm = _fold_rows(s, jnp.max)
        br["m8"] = bm if br["m8"] is None else jnp.maximum(br["m8"], bm)

    def sel_scores(st, n):
        s = _nt(ks_ref[n * KEY_TILE:(n + 1) * KEY_TILE, :], st["qcat"])
        if st["bias"] is not None:
            s = jnp.concatenate(
                [s[jj * NSA_SLC_BLOCK:(jj + 1) * NSA_SLC_BLOCK, :]
                 + st["bias"][n * sub + jj:n * sub + jj + 1, :] for jj in range(sub)], axis=0)
        if n == st["i"]:
            s = jnp.where(causal, s, NEG_INF)
        ssel_ref[n] = s
        note_max(st["sel"], s)

    def win_scores(st, n):
        s = _nt(kw_ref[n * KEY_TILE:(n + 1) * KEY_TILE, :], st["qcat"])
        if n == st["i"]:
            s = jnp.where(causal, s, NEG_INF)
        elif n == st["i"] - back:
            s = jnp.where(window_tail, s, NEG_INF)
        swin_ref[st["i"] - n] = s
        note_max(st["win"], s)

    def weights(br, s_view, v_view):
        p = jnp.exp2(s_view[...] - br["m"])
        br["l8"] = br["l8"] + _fold_rows(p, jnp.sum)
        br["acc"] = br["acc"] + _nn(v_view[...], p.astype(BF16))

    def finish(st):
        i = st["i"]
        o_slc, o_win = [br["acc"] * (1.0 / jnp.sum(br["l8"], axis=0, keepdims=True))
                        for br in (st["sel"], st["win"])]
        gates = gt_ref[pl.ds(pl.multiple_of(grp * 16, 16), 16), i * t:(i + 1) * t]
        heads = []
        for h in range(hg):
            cols = slice(h * t, (h + 1) * t)
            heads.append(gates[3 * h:3 * h + 1, :] * st["o_cmp"][:, cols]
                         + gates[3 * h + 1:3 * h + 2, :] * o_slc[:, cols]
                         + gates[3 * h + 2:3 * h + 3, :] * o_win[:, cols])
        for pair in range(hg // 2):
            both = jnp.concatenate([heads[2 * pair], heads[2 * pair + 1]], axis=0)
            o_ref[i * t:(i + 1) * t, pair * LANES:(pair + 1) * LANES] = both.T.astype(o_ref.dtype)

    def interleave(first, second):
        for k in range(max(len(first), len(second))):
            for steps in (first, second):
                if k < len(steps):
                    steps[k]()

    def win_tiles(i):
        return [n for n in range(i, i - back - 1, -1) if n >= 0]

    cur = begin(0)
    sel_scores(cur, 0)
    for i in range(n_tiles):
        cur["sel"]["m"] = jnp.max(cur["sel"]["m8"], axis=0, keepdims=True)
        interleave([functools.partial(win_scores, cur, n) for n in win_tiles(i)],
                   [functools.partial(weights, cur["sel"], ssel_ref.at[n], vst_ref.at[n])
                    for n in range(i + 1)])
        cur["win"]["m"] = jnp.max(cur["win"]["m8"], axis=0, keepdims=True)
        nxt = begin(i + 1) if i + 1 < n_tiles else None
        interleave([functools.partial(sel_scores, nxt, n) for n in range(i + 2)] if nxt else [],
                   [functools.partial(weights, cur["win"], swin_ref.at[i - n], vwt_ref.at[n])
                    for n in win_tiles(i)])
        finish(cur)
        cur = nxt


def _nsa_attention(proj, gates, kcmp, vcmpt, overlap_t, batch, seq):
    n = proj.shape[0]
    n_tiles = seq // KEY_TILE
    n_sel = seq // NSA_SLC_BLOCK
    g = NSA_KV_GROUPS
    hg = NSA_HEADS_PER_GROUP
    n_cmp_pad = kcmp.shape[1]
    back = NSA_WINDOW // KEY_TILE
    ks0, kw0, vsw0 = 12, 16, 20
    return pl.pallas_call(
        functools.partial(_nsa_kernel, n_tiles=n_tiles, n_sel=n_sel),
        grid=(batch, g),
        in_specs=[pl.BlockSpec((seq, hg * HEAD_DIM), lambda b, k: (b, k)),
                  pl.BlockSpec((1, n_cmp_pad, LANES), lambda b, k: (b * g + k, 0, 0)),
                  pl.BlockSpec((1, HEAD_DIM, n_cmp_pad), lambda b, k: (b * g + k, 0, 0)),
                  pl.BlockSpec((seq, LANES), lambda b, k: (b, ks0 + k)),
                  pl.BlockSpec((seq, LANES), lambda b, k: (b, kw0 + k)),
                  pl.BlockSpec((seq, LANES), lambda b, k: (b, vsw0 + k)),
                  pl.BlockSpec((seq, LANES), lambda b, k: (b, 0)),
                  _const_spec(overlap_t.shape)],
        out_specs=pl.BlockSpec((seq, hg * HEAD_DIM), lambda b, k: (b, k)),
        out_shape=jax.ShapeDtypeStruct((n, N_HEADS * HEAD_DIM), BF16),
        scratch_shapes=[pltpu.VMEM((n_tiles, HEAD_DIM, KEY_TILE), BF16),
                        pltpu.VMEM((n_tiles, HEAD_DIM, KEY_TILE), BF16),
                        pltpu.VMEM((LANES, seq), F32),
                        pltpu.VMEM((n_tiles, KEY_TILE, hg * Q_TILE), F32),
                        pltpu.VMEM((back + 1, KEY_TILE, hg * Q_TILE), F32)],
        compiler_params=pltpu.CompilerParams(
            dimension_semantics=("arbitrary", "arbitrary"),
            vmem_limit_bytes=VMEM_LIMIT),
        name="nsa_attention",
    )(proj, kcmp, vcmpt, proj, proj, proj, gates, overlap_t)


def _rope_tables(seq):
    inv_freq = 1.0 / (ROPE_THETA ** (jnp.arange(0, HEAD_DIM, 2, dtype=F32) / HEAD_DIM))
    ang = jnp.arange(seq, dtype=F32)[:, None] * inv_freq[None, :]
    reps = LANES // (HEAD_DIM // 2)
    cos = jnp.tile(jnp.cos(ang), (1, reps))
    sin = jnp.tile(jnp.sin(ang), (1, reps))
    first_half = (jnp.arange(LANES) % HEAD_DIM) < HEAD_DIM // 2
    return cos, jnp.where(first_half, -sin, 0.0), jnp.where(first_half, 0.0, sin)


def _nsa_weight_layout(w_in):
    qd = N_HEADS * HEAD_DIM
    kv = NSA_KV_COLS
    q, kc, vc, ks, vs, kw, vw, gt = jnp.split(
        w_in, np.cumsum([qd] + [kv] * 6).tolist(), axis=1)

    def dup(w):
        w = w.reshape(D_MODEL, NSA_KV_GROUPS, 1, HEAD_DIM)
        return jnp.broadcast_to(w, (D_MODEL, NSA_KV_GROUPS, 2, HEAD_DIM)).reshape(D_MODEL, -1)

    vsw = jnp.concatenate([vs.reshape(D_MODEL, NSA_KV_GROUPS, 1, HEAD_DIM),
                           vw.reshape(D_MODEL, NSA_KV_GROUPS, 1, HEAD_DIM)], axis=2)
    main = jnp.concatenate([q, kc, vc, dup(ks), dup(kw), vsw.reshape(D_MODEL, -1)], axis=1)
    per_group = 3 * NSA_HEADS_PER_GROUP
    gt = gt.reshape(D_MODEL, NSA_KV_GROUPS, per_group)
    gt = jnp.pad(gt, ((0, 0), (0, 0), (0, 16 - per_group))).reshape(D_MODEL, -1)
    gt = jnp.pad(gt, ((0, 0), (0, LANES - gt.shape[1])))
    return main.astype(BF16), gt.astype(BF16)


def _overlap_t(seq):
    n_cmp = (seq - NSA_CMP_BLOCK) // NSA_CMP_STRIDE + 1
    n_sel = seq // NSA_SLC_BLOCK
    c_start = np.arange(n_cmp) * NSA_CMP_STRIDE
    s_start = np.arange(n_sel) * NSA_SLC_BLOCK
    lo = np.maximum(c_start[:, None], s_start[None, :])
    hi = np.minimum(c_start[:, None] + NSA_CMP_BLOCK, s_start[None, :] + NSA_SLC_BLOCK)
    ov = np.clip(hi - lo, 0, None) / NSA_CMP_BLOCK
    ov = np.pad(ov, ((0, seq // NSA_CMP_STRIDE - n_cmp), (0, 0)))
    return jnp.asarray(ov.T, dtype=BF16)


_QK_ROPE = (True,) * 16 + (False,) * 8
_NO_ROPE = (False,) * 24
_NSA_ROPE = (True,) * 10 + (False,) * 2 + (True,) * 8 + (False,) * 4


def kernel(x, l0_ln_mix_pre, l0_w_in, l0_w_out, l0_ln_mix_post, l0_ln_mlp_pre, l0_w_up, l0_w_down, l0_ln_mlp_post, l1_ln_mix_pre, l1_w_in, l1_w_out, l1_ln_mix_post, l1_ln_mlp_pre, l1_w_up, l1_w_down, l1_ln_mlp_post, l2_ln_mix_pre, l2_w_in, l2_cmp_pos_k, l2_cmp_w1_k, l2_cmp_w2_k, l2_cmp_pos_v, l2_cmp_w1_v, l2_cmp_w2_v, l2_w_out, l2_ln_mix_post, l2_ln_mlp_pre, l2_w_up, l2_w_down, l2_ln_mlp_post, l3_ln_mix_pre, l3_w_in, l3_w_out, l3_ln_mix_post, l3_ln_mlp_pre, l3_w_up, l3_w_down, l3_ln_mlp_post):
    batch, seq, _ = x.shape
    assert seq % ROW_TILE == 0 and seq % KEY_TILE == 0 and Q_TILE == KEY_TILE == MOBA_BLOCK
    tables = _rope_tables(seq)
    x2 = x.reshape(batch * seq, D_MODEL)
    bf = lambda w: w.astype(BF16)

    def tail(o, x2, w_out, g_post, g_pre, w_up, w_down, g_mlp):
        return _layer_tail(o, x2, bf(w_out), g_post, g_pre, bf(w_up), bf(w_down), g_mlp)

    qkv = _norm_proj(x2, l0_ln_mix_pre, bf(l0_w_in), tables, _QK_ROPE, seq)
    x2 = tail(_moba_attention(qkv, batch, seq), x2, l0_w_out, l0_ln_mix_post, l0_ln_mlp_pre,
              l0_w_up, l0_w_down, l0_ln_mlp_post)

    qkv = _norm_proj(x2, l1_ln_mix_pre, bf(l1_w_in), tables, _NO_ROPE, seq)
    x2 = tail(_sb_attention(qkv, batch, seq), x2, l1_w_out, l1_ln_mix_post, l1_ln_mlp_pre,
              l1_w_up, l1_w_down, l1_ln_mlp_post)

    w_main, w_gate = _nsa_weight_layout(l2_w_in)
    proj, gates = _norm_proj(x2, l2_ln_mix_pre, w_main, tables, _NSA_ROPE, seq, wg=w_gate)
    qd = N_HEADS * HEAD_DIM
    rows = seq // NSA_CMP_STRIDE

    def row_groups(cols):
        a = cols.reshape(batch, seq, NSA_KV_GROUPS, HEAD_DIM).transpose(0, 2, 1, 3)
        return a.reshape(batch * NSA_KV_GROUPS, rows, NSA_CMP_STRIDE * HEAD_DIM)

    flat_pos = lambda p: p.reshape(2, NSA_CMP_STRIDE * HEAD_DIM)
    w2_dup = lambda w: bf(jnp.concatenate([w, w], axis=1))
    kcmp, vcmpt = _nsa_compress(
        row_groups(proj[:, qd:qd + NSA_KV_COLS]), row_groups(proj[:, qd + NSA_KV_COLS:qd + 2 * NSA_KV_COLS]),
        flat_pos(l2_cmp_pos_k), flat_pos(l2_cmp_pos_v), bf(l2_cmp_w1_k), bf(l2_cmp_w1_v),
        w2_dup(l2_cmp_w2_k), w2_dup(l2_cmp_w2_v))
    o = _nsa_attention(proj, gates, kcmp, vcmpt, _overlap_t(seq), batch, seq)
    x2 = tail(o, x2, l2_w_out, l2_ln_mix_post, l2_ln_mlp_pre, l2_w_up, l2_w_down, l2_ln_mlp_post)

    qkv = _norm_proj(x2, l3_ln_mix_pre, bf(l3_w_in), tables, _QK_ROPE, seq)
    x2 = tail(_moba_attention(qkv, batch, seq), x2, l3_w_out, l3_ln_mix_post, l3_ln_mlp_pre,
              l3_w_up, l3_w_down, l3_ln_mlp_post)
    return x2.reshape(batch, seq, D_MODEL)
```

```python
import functools
import math

import numpy as np
import jax
import jax.numpy as jnp
from jax import lax
from jax.experimental import pallas as pl
from jax.experimental.pallas import tpu as pltpu

D_MODEL = 1024
N_HEADS = 16
HEAD_DIM = 64
D_FF = 4 * D_MODEL
ROPE_THETA = 10000.0
NORM_EPS = 1e-6
NEG_INF = -1e30
POS_BIG = 1e30

MOBA_BLOCK = 256
MOBA_TOPK = 3

NSA_KV_GROUPS = 4
NSA_HEADS_PER_GROUP = N_HEADS // NSA_KV_GROUPS
NSA_CMP_BLOCK = 32
NSA_CMP_STRIDE = 16
NSA_CMP_HIDDEN = 256
NSA_SLC_BLOCK = 64
NSA_SLC_TOPN = 16
NSA_WINDOW = 512
NSA_KV_COLS = NSA_KV_GROUPS * HEAD_DIM

LANES = 128
SUBLANES = 8
KEY_TILE = 256
Q_TILE = 256
ROW_TILE = 512
FF_CHUNK = 512
PROJ_COLS = 3 * N_HEADS * HEAD_DIM
VMEM_LIMIT = 56 * 1024 * 1024
QUERY_SCALE = HEAD_DIM ** -0.5 * math.log2(math.e)
SB_UNDERFLOW_LOG2 = -150.0

F32 = jnp.float32
BF16 = jnp.bfloat16


def _nt(a, b):
    return lax.dot_general(a, b, (((1,), (1,)), ((), ())), preferred_element_type=F32)


def _nn(a, b):
    return jnp.dot(a, b, preferred_element_type=F32)


def _split_bf16(x):
    hi = x.astype(BF16)
    lo = (x - hi.astype(F32)).astype(BF16)
    return hi, lo


def _rms(x, g):
    ms = jnp.mean(x * x, axis=-1, keepdims=True)
    return x * lax.rsqrt(ms + NORM_EPS) * g


def _const_spec(shape):
    nd = len(shape)
    return pl.BlockSpec(shape, lambda *_: (0,) * nd, pipeline_mode=pl.Buffered(1))


def _proj_kernel(x_ref, g_ref, w_ref, cos_ref, sina_ref, sinb_ref, *rest, rope_slabs, has_gate):
    if has_gate:
        wg_ref, o_ref, og_ref = rest
    else:
        (o_ref,) = rest
    h = _rms(x_ref[...], g_ref[...]).astype(BF16)
    cos = cos_ref[...]
    sina = sina_ref[...]
    sinb = sinb_ref[...]
    n_slabs = len(rope_slabs)
    for c in range(n_slabs // 2):
        y = _nn(h, w_ref[:, c * 2 * LANES:(c + 1) * 2 * LANES])
        for half in range(2):
            s = 2 * c + half
            ys = y[:, half * LANES:(half + 1) * LANES]
            if s < N_HEADS * HEAD_DIM // LANES:
                ys = ys * QUERY_SCALE
            if rope_slabs[s]:
                ys = (ys * cos + pltpu.roll(ys, LANES - HEAD_DIM // 2, 1) * sina
                      + pltpu.roll(ys, HEAD_DIM // 2, 1) * sinb)
            o_ref[:, s * LANES:(s + 1) * LANES] = ys.astype(o_ref.dtype)
    if has_gate:
        og_ref[...] = _nn(h, wg_ref[...])


def _norm_proj(x2, g, w, tables, rope_slabs, seq, wg=None):
    n = x2.shape[0]
    cols = w.shape[1]
    pos_blocks = seq // ROW_TILE
    has_gate = wg is not None
    row = lambda i: (i, 0)
    tab = pl.BlockSpec((ROW_TILE, LANES), lambda i: (i % pos_blocks, 0))
    in_specs = [pl.BlockSpec((ROW_TILE, D_MODEL), row), _const_spec((1, D_MODEL)),
                _const_spec((D_MODEL, cols)), tab, tab, tab]
    args = [x2, g.reshape(1, D_MODEL), w, *tables]
    out_shape = [jax.ShapeDtypeStruct((n, cols), BF16)]
    out_specs = [pl.BlockSpec((ROW_TILE, cols), row)]
    if has_gate:
        in_specs.append(_const_spec((D_MODEL, LANES)))
        args.append(wg)
        out_shape.append(jax.ShapeDtypeStruct((n, LANES), F32))
        out_specs.append(pl.BlockSpec((ROW_TILE, LANES), row))
    outs = pl.pallas_call(
        functools.partial(_proj_kernel, rope_slabs=rope_slabs, has_gate=has_gate),
        grid=(n // ROW_TILE,),
        in_specs=in_specs,
        out_specs=out_specs,
        out_shape=out_shape,
        compiler_params=pltpu.CompilerParams(dimension_semantics=("arbitrary",),
                                             vmem_limit_bytes=VMEM_LIMIT),
        name="norm_proj",
    )(*args)
    return outs if has_gate else outs[0]


def _tail_kernel(o_ref, x_ref, wout_ref, gpost_ref, gpre_ref, wup_ref, wdown_ref, gmlp_ref, out_ref):
    m = _nn(o_ref[...], wout_ref[...])
    x1 = x_ref[...] + _rms(m, gpost_ref[...])
    h = _rms(x1, gpre_ref[...]).astype(BF16)
    acc = jnp.zeros((ROW_TILE, D_MODEL), F32)
    for j in range(D_FF // FF_CHUNK):
        u = _nn(h, wup_ref[:, j * FF_CHUNK:(j + 1) * FF_CHUNK])
        a = jnp.square(jnp.maximum(u, 0.0)).astype(BF16)
        acc = acc + _nn(a, wdown_ref[j * FF_CHUNK:(j + 1) * FF_CHUNK, :])
    out_ref[...] = x1 + _rms(acc, gmlp_ref[...])


def _layer_tail(o2, x2, wout, gpost, gpre, wup, wdown, gmlp):
    n = x2.shape[0]
    row = lambda i: (i, 0)
    vec = lambda v: v.reshape(1, D_MODEL)
    return pl.pallas_call(
        _tail_kernel,
        grid=(n // ROW_TILE,),
        in_specs=[pl.BlockSpec((ROW_TILE, N_HEADS * HEAD_DIM), row),
                  pl.BlockSpec((ROW_TILE, D_MODEL), row),
                  _const_spec((N_HEADS * HEAD_DIM, D_MODEL)), _const_spec((1, D_MODEL)),
                  _const_spec((1, D_MODEL)), _const_spec((D_MODEL, D_FF)),
                  _const_spec((D_FF, D_MODEL)), _const_spec((1, D_MODEL))],
        out_specs=pl.BlockSpec((ROW_TILE, D_MODEL), row),
        out_shape=jax.ShapeDtypeStruct((n, D_MODEL), F32),
        compiler_params=pltpu.CompilerParams(dimension_semantics=("arbitrary",),
                                             vmem_limit_bytes=VMEM_LIMIT),
        name="layer_tail",
    )(o2, x2, wout, vec(gpost), vec(gpre), wup, wdown, vec(gmlp))


def _pair_queries(q):
    lane = lax.broadcasted_iota(jnp.int32, q.shape, 1)
    zero = jnp.zeros_like(q)
    return jnp.concatenate([jnp.where(lane < HEAD_DIM, q, zero),
                            jnp.where(lane >= HEAD_DIM, q, zero)], axis=0)


def _transpose_value_tiles(v_ref, vt_ref, n_tiles):
    for n in range(n_tiles):
        blk = v_ref[n * KEY_TILE:(n + 1) * KEY_TILE, :].astype(F32)
        vt_ref[n] = blk.T.astype(BF16)


def _fold_rows(x, op):
    r, w = x.shape
    return op(x.reshape(r // SUBLANES, SUBLANES, w), axis=0)


def _pair_pv(vt, p):
    t = p.shape[1] // 2
    return _nn(vt[:HEAD_DIM, :], p[:, :t]), _nn(vt[HEAD_DIM:, :], p[:, t:])


def _pair_output(acc_a, acc_b, inv_l=None):
    if inv_l is not None:
        t = acc_a.shape[1]
        acc_a = acc_a * inv_l[:, :t]
        acc_b = acc_b * inv_l[:, t:]
    return jnp.concatenate([acc_a, acc_b], axis=0).T


def _moba_kernel(q_ref, k_ref, v_ref, o_ref, kmean_ref, vt_ref, s_ref, *, n_blocks):
    w2 = 2 * Q_TILE
    for n in range(n_blocks):
        blk = k_ref[n * MOBA_BLOCK:(n + 1) * MOBA_BLOCK, :].astype(F32)
        kmean_ref[n:n + 1, :] = jnp.mean(blk, axis=0, keepdims=True)
    _transpose_value_tiles(v_ref, vt_ref, n_blocks)
    km_hi, km_lo = _split_bf16(kmean_ref[...])
    kpos = lax.broadcasted_iota(jnp.int32, (KEY_TILE, w2), 0)
    tpos = lax.broadcasted_iota(jnp.int32, (KEY_TILE, w2), 1) % Q_TILE
    causal = kpos <= tpos

    def begin(i):
        qcat = _pair_queries(q_ref[i * Q_TILE:(i + 1) * Q_TILE, :])
        chosen = [None] * i
        if i > MOBA_TOPK:
            gate = _nt(km_hi, qcat) + _nt(km_lo, qcat)
            blk_id = lax.broadcasted_iota(jnp.int32, gate.shape, 0)
            g = jnp.where(blk_id < i, gate, NEG_INF)
            sel = jnp.zeros(gate.shape, F32)
            for _ in range(MOBA_TOPK):
                mx = jnp.max(g, axis=0, keepdims=True)
                first = jnp.min(jnp.where(g == mx, blk_id, n_blocks), axis=0, keepdims=True)
                hit = blk_id == first
                sel = jnp.where(hit, 1.0, sel)
                g = jnp.where(hit, -jnp.inf, g)
            chosen = [sel[n:n + 1, :] > 0.5 for n in range(i)]
        return dict(i=i, qcat=qcat, chosen=chosen, m8=None, l8=jnp.zeros((SUBLANES, w2), F32),
                    acc_a=jnp.zeros((HEAD_DIM, Q_TILE), F32), acc_b=jnp.zeros((HEAD_DIM, Q_TILE), F32))

    def score_tile(st, n):
        i = st["i"]
        s = _nt(k_ref[n * KEY_TILE:(n + 1) * KEY_TILE, :], st["qcat"])
        if n == i:
            s = jnp.where(causal, s, NEG_INF)
        s_ref[i % 2, n] = s
        bm = _fold_rows(s, jnp.max)
        if n < i and st["chosen"][n] is not None:
            bm = jnp.where(st["chosen"][n], bm, NEG_INF)
        st["m8"] = bm if st["m8"] is None else jnp.maximum(st["m8"], bm)

    def weight_tile(st, n):
        i = st["i"]
        keep = None if n == i else st["chosen"][n]
        shift = st["m"] if keep is None else jnp.where(keep, st["m"], POS_BIG)
        p = jnp.exp2(s_ref[i % 2, n] - shift)
        st["l8"] = st["l8"] + _fold_rows(p, jnp.sum)
        pv_a, pv_b = _pair_pv(vt_ref[n], p.astype(BF16))
        st["acc_a"] = st["acc_a"] + pv_a
        st["acc_b"] = st["acc_b"] + pv_b

    cur = begin(0)
    score_tile(cur, 0)
    for i in range(n_blocks):
        cur["m"] = jnp.max(cur["m8"], axis=0, keepdims=True)
        nxt = begin(i + 1) if i + 1 < n_blocks else None
        for n in range(i + 2):
            if nxt is not None:
                score_tile(nxt, n)
            if n <= i:
                weight_tile(cur, n)
        l = jnp.sum(cur["l8"], axis=0, keepdims=True)
        o_ref[i * Q_TILE:(i + 1) * Q_TILE, :] = _pair_output(
            cur["acc_a"], cur["acc_b"], 1.0 / l).astype(o_ref.dtype)
        cur = nxt


def _moba_attention(qkv, batch, seq):
    n = qkv.shape[0]
    n_blocks = seq // MOBA_BLOCK
    pairs = N_HEADS // 2
    return pl.pallas_call(
        functools.partial(_moba_kernel, n_blocks=n_blocks),
        grid=(batch, pairs),
        in_specs=[pl.BlockSpec((seq, LANES), lambda b, p: (b, p)),
                  pl.BlockSpec((seq, LANES), lambda b, p: (b, pairs + p)),
                  pl.BlockSpec((seq, LANES), lambda b, p: (b, 2 * pairs + p))],
        out_specs=pl.BlockSpec((seq, LANES), lambda b, p: (b, p)),
        out_shape=jax.ShapeDtypeStruct((n, N_HEADS * HEAD_DIM), BF16),
        scratch_shapes=[pltpu.VMEM((n_blocks, LANES), F32),
                        pltpu.VMEM((n_blocks, LANES, KEY_TILE), BF16),
                        pltpu.VMEM((2, n_blocks, KEY_TILE, 2 * Q_TILE), F32)],
        compiler_params=pltpu.CompilerParams(
            dimension_semantics=("arbitrary", "arbitrary"),
            vmem_limit_bytes=VMEM_LIMIT),
        name="moba_attention",
    )(qkv, qkv, qkv)


def _log2_sigmoid_pair(z):
    log_beta = jnp.minimum(z, 0.0) - jnp.log2(1.0 + jnp.exp2(-jnp.abs(z)))
    return log_beta, log_beta - z


def _sb_kernel(q_ref, k_ref, v_ref, o_ref, vt_ref, acc_ref, carry_ref, *, n_tiles):
    w2 = 2 * Q_TILE
    near = 2
    _transpose_value_tiles(v_ref, vt_ref, n_tiles)
    r = lax.broadcasted_iota(jnp.int32, (KEY_TILE, KEY_TILE), 0)
    c = lax.broadcasted_iota(jnp.int32, (KEY_TILE, KEY_TILE), 1)
    later = jnp.where(c > r, 1.0, 0.0).astype(BF16)
    kpos = lax.broadcasted_iota(jnp.int32, (KEY_TILE, w2), 0)
    tpos = lax.broadcasted_iota(jnp.int32, (KEY_TILE, w2), 1) % Q_TILE
    strict = kpos < tpos

    def scores(state, t):
        st = state[t["i"]]
        t["z"] = _nt(k_ref[t["n"] * KEY_TILE:(t["n"] + 1) * KEY_TILE, :], st["qcat"])

    def log_terms(state, t):
        st = state[t["i"]]
        log_beta, log_1m = _log2_sigmoid_pair(t.pop("z"))
        if t["n"] == t["i"]:
            log_1m = jnp.where(strict, log_1m, 0.0)
        t["carry"] = st["carry"]
        if t["n"] > 0:
            tile_sum = jnp.sum(log_1m, axis=0, keepdims=True)
            st["carry"] = tile_sum if st["carry"] is None else st["carry"] + tile_sum
        t["log_beta"] = log_beta
        t["suffix"] = _nn(later, log_1m.astype(BF16))

    def weights(state, t):
        st = state[t["i"]]
        expo = t.pop("log_beta") + t.pop("suffix")
        if t["carry"] is not None:
            expo = expo + t["carry"]
        wgt = jnp.exp2(expo)
        if t["n"] == t["i"]:
            wgt = jnp.where(strict, wgt, 0.0)
        pv_a, pv_b = _pair_pv(vt_ref[t["n"]], wgt.astype(BF16))
        st["acc_a"] = st["acc_a"] + pv_a
        st["acc_b"] = st["acc_b"] + pv_b

    def run(state, tiles):
        stages = (scores, log_terms, weights)
        for step in range(len(tiles) + len(stages) - 1):
            for lag, stage in enumerate(stages):
                if 0 <= step - lag < len(tiles):
                    stage(state, tiles[step - lag])

    def query_operand(i):
        return _pair_queries(q_ref[i * Q_TILE:(i + 1) * Q_TILE, :])

    def write_out(i, st):
        o_ref[i * Q_TILE:(i + 1) * Q_TILE, :] = _pair_output(st["acc_a"], st["acc_b"]).astype(o_ref.dtype)

    zeros = jnp.zeros((HEAD_DIM, Q_TILE), F32)
    state = {i: dict(qcat=query_operand(i), carry=None, acc_a=zeros, acc_b=zeros) for i in range(n_tiles)}
    run(state, [dict(i=i, n=n) for i in range(n_tiles) for n in range(i, max(i - near, -1), -1)])
    for i in range(n_tiles):
        write_out(i, state[i])
        if i >= near:
            acc_ref[i] = jnp.concatenate([state[i]["acc_a"], state[i]["acc_b"]], axis=0)
            carry_ref[i] = state[i]["carry"]

    live = {i: jnp.max(state[i]["carry"]) > SB_UNDERFLOW_LOG2 for i in range(near, n_tiles)}
    for i in range(near, n_tiles):
        @pl.when(live[i])
        def _(i=i):
            acc = acc_ref[i]
            far = {i: dict(qcat=query_operand(i), carry=carry_ref[i],
                           acc_a=acc[:HEAD_DIM, :], acc_b=acc[HEAD_DIM:, :])}
            run(far, [dict(i=i, n=n) for n in range(i - near, -1, -1)])
            write_out(i, far[i])


def _sb_attention(qkv, batch, seq):
    n = qkv.shape[0]
    n_tiles = seq // KEY_TILE
    pairs = N_HEADS // 2
    return pl.pallas_call(
        functools.partial(_sb_kernel, n_tiles=n_tiles),
        grid=(batch, pairs),
        in_specs=[pl.BlockSpec((seq, LANES), lambda b, p: (b, p)),
                  pl.BlockSpec((seq, LANES), lambda b, p: (b, pairs + p)),
                  pl.BlockSpec((seq, LANES), lambda b, p: (b, 2 * pairs + p))],
        out_specs=pl.BlockSpec((seq, LANES), lambda b, p: (b, p)),
        out_shape=jax.ShapeDtypeStruct((n, N_HEADS * HEAD_DIM), BF16),
        scratch_shapes=[pltpu.VMEM((n_tiles, LANES, KEY_TILE), BF16),
                        pltpu.VMEM((n_tiles, LANES, Q_TILE), F32),
                        pltpu.VMEM((n_tiles, 1, 2 * Q_TILE), F32)],
        compiler_params=pltpu.CompilerParams(
            dimension_semantics=("arbitrary", "arbitrary"),
            vmem_limit_bytes=VMEM_LIMIT),
        name="stick_breaking_attention",
    )(qkv, qkv, qkv)


def _nsa_compress_kernel(ak_ref, av_ref, posk_ref, posv_ref, w1k_ref, w1v_ref, w2k_ref, w2v_ref,
                         kc_ref, vct_ref):
    half = NSA_CMP_STRIDE * HEAD_DIM
    n_rows = ak_ref.shape[1]

    def compress(a_ref, pos_ref, w1_ref, w2_ref):
        a = a_ref[0].astype(F32)
        first = (a + pos_ref[0:1, :]).astype(BF16)
        second = (a + pos_ref[1:2, :]).astype(BF16)
        pre = _nn(first, w1_ref[:half, :]) + pltpu.roll(_nn(second, w1_ref[half:, :]), n_rows - 1, 0)
        return _nn(jax.nn.gelu(pre).astype(BF16), w2_ref[...])

    kc_ref[0] = compress(ak_ref, posk_ref, w1k_ref, w2k_ref).astype(BF16)
    vct_ref[0] = compress(av_ref, posv_ref, w1v_ref, w2v_ref).T[:HEAD_DIM, :].astype(BF16)


def _nsa_compress(ak, av, posk, posv, w1k, w1v, w2k, w2v):
    bg, rows, feat = ak.shape
    blk = pl.BlockSpec((1, rows, feat), lambda i: (i, 0, 0))
    return pl.pallas_call(
        _nsa_compress_kernel,
        grid=(bg,),
        in_specs=[blk, blk, _const_spec(posk.shape), _const_spec(posv.shape),
                  _const_spec(w1k.shape), _const_spec(w1v.shape),
                  _const_spec(w2k.shape), _const_spec(w2v.shape)],
        out_specs=[pl.BlockSpec((1, rows, LANES), lambda i: (i, 0, 0)),
                   pl.BlockSpec((1, HEAD_DIM, rows), lambda i: (i, 0, 0))],
        out_shape=[jax.ShapeDtypeStruct((bg, rows, LANES), BF16),
                   jax.ShapeDtypeStruct((bg, HEAD_DIM, rows), BF16)],
        compiler_params=pltpu.CompilerParams(dimension_semantics=("arbitrary",),
                                             vmem_limit_bytes=VMEM_LIMIT),
        name="nsa_compress",
    )(ak, av, posk, posv, w1k, w1v, w2k, w2v)


def _nsa_kernel(q_ref, kcmp_ref, vcmpt_ref, ks_ref, kw_ref, vsw_ref, gate_ref, ovl_ref, o_ref,
                vst_ref, vwt_ref, gt_ref, ssel_ref, swin_ref, *, n_tiles, n_sel):
    grp = pl.program_id(1)
    hg = NSA_HEADS_PER_GROUP
    t = Q_TILE
    wq = hg * t
    sub = KEY_TILE // NSA_SLC_BLOCK
    back = NSA_WINDOW // KEY_TILE
    n_top = min(NSA_SLC_TOPN, n_sel)

    for n in range(n_tiles):
        rows = slice(n * KEY_TILE, (n + 1) * KEY_TILE)
        blk = vsw_ref[rows, :].astype(F32).T.astype(BF16)
        vst_ref[n] = blk[:HEAD_DIM, :]
        vwt_ref[n] = blk[HEAD_DIM:, :]
        gt_ref[:, rows] = jax.nn.sigmoid(gate_ref[rows, :]).T
    kloc = lax.broadcasted_iota(jnp.int32, (KEY_TILE, wq), 0)
    tloc = lax.broadcasted_iota(jnp.int32, (KEY_TILE, wq), 1) % t
    causal = kloc <= tloc
    window_tail = kloc > tloc

    def begin(i):
        rows = slice(i * t, (i + 1) * t)
        qcat = jnp.concatenate([_pair_queries(q_ref[rows, :LANES]),
                                _pair_queries(q_ref[rows, LANES:])], axis=0)
        tq = i * t + lax.broadcasted_iota(jnp.int32, (1, wq), 1) % t

        s = _nt(kcmp_ref[0], qcat)
        cmp_end = lax.broadcasted_iota(jnp.int32, s.shape, 0) * NSA_CMP_STRIDE + (NSA_CMP_BLOCK - 1)
        valid = cmp_end <= tq
        m = jnp.max(jnp.where(valid, s, NEG_INF), axis=0, keepdims=True)
        e = jnp.where(valid, jnp.exp2(s - m), 0.0)
        l = jnp.sum(e, axis=0, keepdims=True)
        p_cmp = e * jnp.where(l > 0.0, 1.0 / l, 0.0)
        o_cmp = _nn(vcmpt_ref[0], p_cmp.astype(BF16))

        bias = None
        if sub * (i + 1) > n_top:
            p_grp = p_cmp[:, :t]
            for h in range(1, hg):
                p_grp = p_grp + p_cmp[:, h * t:(h + 1) * t]
            p_hi, p_lo = _split_bf16(p_grp)
            imp = _nn(ovl_ref[...], p_hi) + _nn(ovl_ref[...], p_lo)
            j = lax.broadcasted_iota(jnp.int32, imp.shape, 0)
            q_blk = (i * t + lax.broadcasted_iota(jnp.int32, imp.shape, 1)) // NSA_SLC_BLOCK
            forced = jnp.logical_or(j == 0, jnp.logical_or(j == q_blk, j == q_blk - 1))
            score = jnp.where(j <= q_blk, jnp.where(forced, POS_BIG, imp), NEG_INF)
            sel = jnp.zeros(imp.shape, F32)
            for _ in range(n_top):
                mx = jnp.max(score, axis=0, keepdims=True)
                first = jnp.min(jnp.where(score == mx, j, n_sel), axis=0, keepdims=True)
                hit = j == first
                sel = jnp.where(jnp.logical_and(hit, mx > 0.5 * NEG_INF), 1.0, sel)
                score = jnp.where(hit, -jnp.inf, score)
            bias = jnp.concatenate([jnp.where(sel > 0.5, 0.0, NEG_INF)] * hg, axis=1)
        zeros8 = jnp.zeros((SUBLANES, wq), F32)
        zeros_acc = jnp.zeros((HEAD_DIM, wq), F32)
        return dict(i=i, qcat=qcat, o_cmp=o_cmp, bias=bias,
                    sel=dict(m8=None, l8=zeros8, acc=zeros_acc),
                    win=dict(m8=None, l8=zeros8, acc=zeros_acc))

    def note_max(br, s):
        bm = _fold_rows(s, jnp.max)
        br["m8"] = bm if br["m8"] is None else jnp.maximum(br["m8"], bm)

    def sel_scores(st, n):
        s = _nt(ks_ref[n * KEY_TILE:(n + 1) * KEY_TILE, :], st["qcat"])
        if st["bias"] is not None:
            s = jnp.concatenate(
                [s[jj * NSA_SLC_BLOCK:(jj + 1) * NSA_SLC_BLOCK, :]
                 + st["bias"][n * sub + jj:n * sub + jj + 1, :] for jj in range(sub)], axis=0)
        if n == st["i"]:
            s = jnp.where(causal, s, NEG_INF)
        ssel_ref[n] = s
        note_max(st["sel"], s)

    def win_scores(st, n):
        s = _nt(kw_ref[n * KEY_TILE:(n + 1) * KEY_TILE, :], st["qcat"])
        if n == st["i"]:
            s = jnp.where(causal, s, NEG_INF)
        elif n == st["i"] - back:
            s = jnp.where(window_tail, s, NEG_INF)
        swin_ref[st["i"] - n] = s
        note_max(st["win"], s)

    def weights(br, s_view, v_view):
        p = jnp.exp2(s_view[...] - br["m"])
        br["l8"] = br["l8"] + _fold_rows(p, jnp.sum)
        br["acc"] = br["acc"] + _nn(v_view[...], p.astype(BF16))

    def finish(st):
        i = st["i"]
        o_slc, o_win = [br["acc"] * (1.0 / jnp.sum(br["l8"], axis=0, keepdims=True))
                        for br in (st["sel"], st["win"])]
        gates = gt_ref[pl.ds(pl.multiple_of(grp * 16, 16), 16), i * t:(i + 1) * t]
        heads = []
        for h in range(hg):
            cols = slice(h * t, (h + 1) * t)
            heads.append(gates[3 * h:3 * h + 1, :] * st["o_cmp"][:, cols]
                         + gates[3 * h + 1:3 * h + 2, :] * o_slc[:, cols]
                         + gates[3 * h + 2:3 * h + 3, :] * o_win[:, cols])
        for pair in range(hg // 2):
            both = jnp.concatenate([heads[2 * pair], heads[2 * pair + 1]], axis=0)
            o_ref[i * t:(i + 1) * t, pair * LANES:(pair + 1) * LANES] = both.T.astype(o_ref.dtype)

    def interleave(first, second):
        for k in range(max(len(first), len(second))):
            for steps in (first, second):
                if k < len(steps):
                    steps[k]()

    def win_tiles(i):
        return [n for n in range(i, i - back - 1, -1) if n >= 0]

    cur = begin(0)
    sel_scores(cur, 0)
    for i in range(n_tiles):
        cur["sel"]["m"] = jnp.max(cur["sel"]["m8"], axis=0, keepdims=True)
        interleave([functools.partial(win_scores, cur, n) for n in win_tiles(i)],
                   [functools.partial(weights, cur["sel"], ssel_ref.at[n], vst_ref.at[n])
                    for n in range(i + 1)])
        cur["win"]["m"] = jnp.max(cur["win"]["m8"], axis=0, keepdims=True)
        nxt = begin(i + 1) if i + 1 < n_tiles else None
        interleave([functools.partial(sel_scores, nxt, n) for n in range(i + 2)] if nxt else [],
                   [functools.partial(weights, cur["win"], swin_ref.at[i - n], vwt_ref.at[n])
                    for n in win_tiles(i)])
        finish(cur)
        cur = nxt


def _nsa_attention(proj, gates, kcmp, vcmpt, overlap_t, batch, seq):
    n = proj.shape[0]
    n_tiles = seq // KEY_TILE
    n_sel = seq // NSA_SLC_BLOCK
    g = NSA_KV_GROUPS
    hg = NSA_HEADS_PER_GROUP
    n_cmp_pad = kcmp.shape[1]
    back = NSA_WINDOW // KEY_TILE
    ks0, kw0, vsw0 = 12, 16, 20
    return pl.pallas_call(
        functools.partial(_nsa_kernel, n_tiles=n_tiles, n_sel=n_sel),
        grid=(batch, g),
        in_specs=[pl.BlockSpec((seq, hg * HEAD_DIM), lambda b, k: (b, k)),
                  pl.BlockSpec((1, n_cmp_pad, LANES), lambda b, k: (b * g + k, 0, 0)),
                  pl.BlockSpec((1, HEAD_DIM, n_cmp_pad), lambda b, k: (b * g + k, 0, 0)),
                  pl.BlockSpec((seq, LANES), lambda b, k: (b, ks0 + k)),
                  pl.BlockSpec((seq, LANES), lambda b, k: (b, kw0 + k)),
                  pl.BlockSpec((seq, LANES), lambda b, k: (b, vsw0 + k)),
                  pl.BlockSpec((seq, LANES), lambda b, k: (b, 0)),
                  _const_spec(overlap_t.shape)],
        out_specs=pl.BlockSpec((seq, hg * HEAD_DIM), lambda b, k: (b, k)),
        out_shape=jax.ShapeDtypeStruct((n, N_HEADS * HEAD_DIM), BF16),
        scratch_shapes=[pltpu.VMEM((n_tiles, HEAD_DIM, KEY_TILE), BF16),
                        pltpu.VMEM((n_tiles, HEAD_DIM, KEY_TILE), BF16),
                        pltpu.VMEM((LANES, seq), F32),
                        pltpu.VMEM((n_tiles, KEY_TILE, hg * Q_TILE), F32),
                        pltpu.VMEM((back + 1, KEY_TILE, hg * Q_TILE), F32)],
        compiler_params=pltpu.CompilerParams(
            dimension_semantics=("arbitrary", "arbitrary"),
            vmem_limit_bytes=VMEM_LIMIT),
        name="nsa_attention",
    )(proj, kcmp, vcmpt, proj, proj, proj, gates, overlap_t)


def _rope_tables(seq):
    inv_freq = 1.0 / (ROPE_THETA ** (jnp.arange(0, HEAD_DIM, 2, dtype=F32) / HEAD_DIM))
    ang = jnp.arange(seq, dtype=F32)[:, None] * inv_freq[None, :]
    reps = LANES // (HEAD_DIM // 2)
    cos = jnp.tile(jnp.cos(ang), (1, reps))
    sin = jnp.tile(jnp.sin(ang), (1, reps))
    first_half = (jnp.arange(LANES) % HEAD_DIM) < HEAD_DIM // 2
    return cos, jnp.where(first_half, -sin, 0.0), jnp.where(first_half, 0.0, sin)


def _nsa_weight_layout(w_in):
    qd = N_HEADS * HEAD_DIM
    kv = NSA_KV_COLS
    q, kc, vc, ks, vs, kw, vw, gt = jnp.split(
        w_in, np.cumsum([qd] + [kv] * 6).tolist(), axis=1)

    def dup(w):
        w = w.reshape(D_MODEL, NSA_KV_GROUPS, 1, HEAD_DIM)
        return jnp.broadcast_to(w, (D_MODEL, NSA_KV_GROUPS, 2, HEAD_DIM)).reshape(D_MODEL, -1)

    vsw = jnp.concatenate([vs.reshape(D_MODEL, NSA_KV_GROUPS, 1, HEAD_DIM),
                           vw.reshape(D_MODEL, NSA_KV_GROUPS, 1, HEAD_DIM)], axis=2)
    main = jnp.concatenate([q, kc, vc, dup(ks), dup(kw), vsw.reshape(D_MODEL, -1)], axis=1)
    per_group = 3 * NSA_HEADS_PER_GROUP
    gt = gt.reshape(D_MODEL, NSA_KV_GROUPS, per_group)
    gt = jnp.pad(gt, ((0, 0), (0, 0), (0, 16 - per_group))).reshape(D_MODEL, -1)
    gt = jnp.pad(gt, ((0, 0), (0, LANES - gt.shape[1])))
    return main.astype(BF16), gt.astype(BF16)


def _overlap_t(seq):
    n_cmp = (seq - NSA_CMP_BLOCK) // NSA_CMP_STRIDE + 1
    n_sel = seq // NSA_SLC_BLOCK
    c_start = np.arange(n_cmp) * NSA_CMP_STRIDE
    s_start = np.arange(n_sel) * NSA_SLC_BLOCK
    lo = np.maximum(c_start[:, None], s_start[None, :])
    hi = np.minimum(c_start[:, None] + NSA_CMP_BLOCK, s_start[None, :] + NSA_SLC_BLOCK)
    ov = np.clip(hi - lo, 0, None) / NSA_CMP_BLOCK
    ov = np.pad(ov, ((0, seq // NSA_CMP_STRIDE - n_cmp), (0, 0)))
    return jnp.asarray(ov.T, dtype=BF16)


_QK_ROPE = (True,) * 16 + (False,) * 8
_NO_ROPE = (False,) * 24
_NSA_ROPE = (True,) * 10 + (False,) * 2 + (True,) * 8 + (False,) * 4


def kernel(x, l0_ln_mix_pre, l0_w_in, l0_w_out, l0_ln_mix_post, l0_ln_mlp_pre, l0_w_up, l0_w_down, l0_ln_mlp_post, l1_ln_mix_pre, l1_w_in, l1_w_out, l1_ln_mix_post, l1_ln_mlp_pre, l1_w_up, l1_w_down, l1_ln_mlp_post, l2_ln_mix_pre, l2_w_in, l2_cmp_pos_k, l2_cmp_w1_k, l2_cmp_w2_k, l2_cmp_pos_v, l2_cmp_w1_v, l2_cmp_w2_v, l2_w_out, l2_ln_mix_post, l2_ln_mlp_pre, l2_w_up, l2_w_down, l2_ln_mlp_post, l3_ln_mix_pre, l3_w_in, l3_w_out, l3_ln_mix_post, l3_ln_mlp_pre, l3_w_up, l3_w_down, l3_ln_mlp_post):
    batch, seq, _ = x.shape
    assert seq % ROW_TILE == 0 and seq % KEY_TILE == 0 and Q_TILE == KEY_TILE == MOBA_BLOCK
    tables = _rope_tables(seq)
    x2 = x.reshape(batch * seq, D_MODEL)
    bf = lambda w: w.astype(BF16)

    def tail(o, x2, w_out, g_post, g_pre, w_up, w_down, g_mlp):
        return _layer_tail(o, x2, bf(w_out), g_post, g_pre, bf(w_up), bf(w_down), g_mlp)

    qkv = _norm_proj(x2, l0_ln_mix_pre, bf(l0_w_in), tables, _QK_ROPE, seq)
    x2 = tail(_moba_attention(qkv, batch, seq), x2, l0_w_out, l0_ln_mix_post, l0_ln_mlp_pre,
              l0_w_up, l0_w_down, l0_ln_mlp_post)

    qkv = _norm_proj(x2, l1_ln_mix_pre, bf(l1_w_in), tables, _NO_ROPE, seq)
    x2 = tail(_sb_attention(qkv, batch, seq), x2, l1_w_out, l1_ln_mix_post, l1_ln_mlp_pre,
              l1_w_up, l1_w_down, l1_ln_mlp_post)

    w_main, w_gate = _nsa_weight_layout(l2_w_in)
    proj, gates = _norm_proj(x2, l2_ln_mix_pre, w_main, tables, _NSA_ROPE, seq, wg=w_gate)
    qd = N_HEADS * HEAD_DIM
    rows = seq // NSA_CMP_STRIDE

    def row_groups(cols):
        a = cols.reshape(batch, seq, NSA_KV_GROUPS, HEAD_DIM).transpose(0, 2, 1, 3)
        return a.reshape(batch * NSA_KV_GROUPS, rows, NSA_CMP_STRIDE * HEAD_DIM)

    flat_pos = lambda p: p.reshape(2, NSA_CMP_STRIDE * HEAD_DIM)
    w2_dup = lambda w: bf(jnp.concatenate([w, w], axis=1))
    kcmp, vcmpt = _nsa_compress(
        row_groups(proj[:, qd:qd + NSA_KV_COLS]), row_groups(proj[:, qd + NSA_KV_COLS:qd + 2 * NSA_KV_COLS]),
        flat_pos(l2_cmp_pos_k), flat_pos(l2_cmp_pos_v), bf(l2_cmp_w1_k), bf(l2_cmp_w1_v),
        w2_dup(l2_cmp_w2_k), w2_dup(l2_cmp_w2_v))
    o = _nsa_attention(proj, gates, kcmp, vcmpt, _overlap_t(seq), batch, seq)
    x2 = tail(o, x2, l2_w_out, l2_ln_mix_post, l2_ln_mlp_pre, l2_w_up, l2_w_down, l2_ln_mlp_post)

    qkv = _norm_proj(x2, l3_ln_mix_pre, bf(l3_w_in), tables, _QK_ROPE, seq)
    x2 = tail(_moba_attention(qkv, batch, seq), x2, l3_w_out, l3_ln_mix_post, l3_ln_mlp_pre,
              l3_w_up, l3_w_down, l3_ln_mlp_post)
    return x2.reshape(batch, seq, D_MODEL)
```

```python
import functools
import math

import numpy as np
import jax
import jax.numpy as jnp
from jax import lax
from jax.experimental import pallas as pl
from jax.experimental.pallas import tpu as pltpu

D_MODEL = 1024
N_HEADS = 16
HEAD_DIM = 64
D_FF = 4 * D_MODEL
ROPE_THETA = 10000.0
NORM_EPS = 1e-6
NEG_INF = -1e30
POS_BIG = 1e30

MOBA_BLOCK = 256
MOBA_TOPK = 3

NSA_KV_GROUPS = 4
NSA_HEADS_PER_GROUP = N_HEADS // NSA_KV_GROUPS
NSA_CMP_BLOCK = 32
NSA_CMP_STRIDE = 16
NSA_CMP_HIDDEN = 256
NSA_SLC_BLOCK = 64
NSA_SLC_TOPN = 16
NSA_WINDOW = 512
NSA_KV_COLS = NSA_KV_GROUPS * HEAD_DIM

LANES = 128
SUBLANES = 8
KEY_TILE = 256
Q_TILE = 256
ROW_TILE = 512
FF_CHUNK = 512
PROJ_COLS = 3 * N_HEADS * HEAD_DIM
VMEM_LIMIT = 56 * 1024 * 1024
QUERY_SCALE = HEAD_DIM ** -0.5 * math.log2(math.e)
SB_UNDERFLOW_LOG2 = -150.0

F32 = jnp.float32
BF16 = jnp.bfloat16


def _nt(a, b):
    return lax.dot_general(a, b, (((1,), (1,)), ((), ())), preferred_element_type=F32)


def _nn(a, b):
    return jnp.dot(a, b, preferred_element_type=F32)


def _split_bf16(x):
    hi = x.astype(BF16)
    lo = (x - hi.astype(F32)).astype(BF16)
    return hi, lo


def _rms(x, g):
    ms = jnp.mean(x * x, axis=-1, keepdims=True)
    return x * lax.rsqrt(ms + NORM_EPS) * g


def _const_spec(shape):
    nd = len(shape)
    return pl.BlockSpec(shape, lambda *_: (0,) * nd, pipeline_mode=pl.Buffered(1))


def _proj_kernel(x_ref, g_ref, w_ref, cos_ref, sina_ref, sinb_ref, *rest, rope_slabs, has_gate):
    if has_gate:
        wg_ref, o_ref, og_ref = rest
    else:
        (o_ref,) = rest
    h = _rms(x_ref[...], g_ref[...]).astype(BF16)
    cos = cos_ref[...]
    sina = sina_ref[...]
    sinb = sinb_ref[...]
    n_slabs = len(rope_slabs)
    for c in range(n_slabs // 2):
        y = _nn(h, w_ref[:, c * 2 * LANES:(c + 1) * 2 * LANES])
        for half in range(2):
            s = 2 * c + half
            ys = y[:, half * LANES:(half + 1) * LANES]
            if s < N_HEADS * HEAD_DIM // LANES:
                ys = ys * QUERY_SCALE
            if rope_slabs[s]:
                ys = (ys * cos + pltpu.roll(ys, LANES - HEAD_DIM // 2, 1) * sina
                      + pltpu.roll(ys, HEAD_DIM // 2, 1) * sinb)
            o_ref[:, s * LANES:(s + 1) * LANES] = ys.astype(o_ref.dtype)
    if has_gate:
        og_ref[...] = _nn(h, wg_ref[...])


def _norm_proj(x2, g, w, tables, rope_slabs, seq, wg=None):
    n = x2.shape[0]
    cols = w.shape[1]
    pos_blocks = seq // ROW_TILE
    has_gate = wg is not None
    row = lambda i: (i, 0)
    tab = pl.BlockSpec((ROW_TILE, LANES), lambda i: (i % pos_blocks, 0))
    in_specs = [pl.BlockSpec((ROW_TILE, D_MODEL), row), _const_spec((1, D_MODEL)),
                _const_spec((D_MODEL, cols)), tab, tab, tab]
    args = [x2, g.reshape(1, D_MODEL), w, *tables]
    out_shape = [jax.ShapeDtypeStruct((n, cols), BF16)]
    out_specs = [pl.BlockSpec((ROW_TILE, cols), row)]
    if has_gate:
        in_specs.append(_const_spec((D_MODEL, LANES)))
        args.append(wg)
        out_shape.append(jax.ShapeDtypeStruct((n, LANES), F32))
        out_specs.append(pl.BlockSpec((ROW_TILE, LANES), row))
    outs = pl.pallas_call(
        functools.partial(_proj_kernel, rope_slabs=rope_slabs, has_gate=has_gate),
        grid=(n // ROW_TILE,),
        in_specs=in_specs,
        out_specs=out_specs,
        out_shape=out_shape,
        compiler_params=pltpu.CompilerParams(dimension_semantics=("arbitrary",),
                                             vmem_limit_bytes=VMEM_LIMIT),
        name="norm_proj",
    )(*args)
    return outs if has_gate else outs[0]


def _tail_kernel(o_ref, x_ref, wout_ref, gpost_ref, gpre_ref, wup_ref, wdown_ref, gmlp_ref, out_ref):
    m = _nn(o_ref[...], wout_ref[...])
    x1 = x_ref[...] + _rms(m, gpost_ref[...])
    h = _rms(x1, gpre_ref[...]).astype(BF16)
    acc = jnp.zeros((ROW_TILE, D_MODEL), F32)
    for j in range(D_FF // FF_CHUNK):
        u = _nn(h, wup_ref[:, j * FF_CHUNK:(j + 1) * FF_CHUNK])
        a = jnp.square(jnp.maximum(u, 0.0)).astype(BF16)
        acc = acc + _nn(a, wdown_ref[j * FF_CHUNK:(j + 1) * FF_CHUNK, :])
    out_ref[...] = x1 + _rms(acc, gmlp_ref[...])


def _layer_tail(o2, x2, wout, gpost, gpre, wup, wdown, gmlp):
    n = x2.shape[0]
    row = lambda i: (i, 0)
    vec = lambda v: v.reshape(1, D_MODEL)
    return pl.pallas_call(
        _tail_kernel,
        grid=(n // ROW_TILE,),
        in_specs=[pl.BlockSpec((ROW_TILE, N_HEADS * HEAD_DIM), row),
                  pl.BlockSpec((ROW_TILE, D_MODEL), row),
                  _const_spec((N_HEADS * HEAD_DIM, D_MODEL)), _const_spec((1, D_MODEL)),
                  _const_spec((1, D_MODEL)), _const_spec((D_MODEL, D_FF)),
                  _const_spec((D_FF, D_MODEL)), _const_spec((1, D_MODEL))],
        out_specs=pl.BlockSpec((ROW_TILE, D_MODEL), row),
        out_shape=jax.ShapeDtypeStruct((n, D_MODEL), F32),
        compiler_params=pltpu.CompilerParams(dimension_semantics=("arbitrary",),
                                             vmem_limit_bytes=VMEM_LIMIT),
        name="layer_tail",
    )(o2, x2, wout, vec(gpost), vec(gpre), wup, wdown, vec(gmlp))


def _pair_queries(q):
    lane = lax.broadcasted_iota(jnp.int32, q.shape, 1)
    zero = jnp.zeros_like(q)
    return jnp.concatenate([jnp.where(lane < HEAD_DIM, q, zero),
                            jnp.where(lane >= HEAD_DIM, q, zero)], axis=0)


def _transpose_value_tiles(v_ref, vt_ref, n_tiles):
    for n in range(n_tiles):
        blk = v_ref[n * KEY_TILE:(n + 1) * KEY_TILE, :].astype(F32)
        vt_ref[n] = blk.T.astype(BF16)


def _fold_rows(x, op):
    r, w = x.shape
    return op(x.reshape(r // SUBLANES, SUBLANES, w), axis=0)


def _pair_pv(vt, p):
    t = p.shape[1] // 2
    return _nn(vt[:HEAD_DIM, :], p[:, :t]), _nn(vt[HEAD_DIM:, :], p[:, t:])


def _pair_output(acc_a, acc_b, inv_l=None):
    if inv_l is not None:
        t = acc_a.shape[1]
        acc_a = acc_a * inv_l[:, :t]
        acc_b = acc_b * inv_l[:, t:]
    return jnp.concatenate([acc_a, acc_b], axis=0).T


def _moba_kernel(q_ref, k_ref, v_ref, o_ref, kmean_ref, vt_ref, s_ref, *, n_blocks):
    for n in range(n_blocks):
        blk = k_ref[n * MOBA_BLOCK:(n + 1) * MOBA_BLOCK, :].astype(F32)
        kmean_ref[n:n + 1, :] = jnp.mean(blk, axis=0, keepdims=True)
    _transpose_value_tiles(v_ref, vt_ref, n_blocks)
    km_hi, km_lo = _split_bf16(kmean_ref[...])
    kpos = lax.broadcasted_iota(jnp.int32, (KEY_TILE, Q_TILE), 0)
    tpos = lax.broadcasted_iota(jnp.int32, (KEY_TILE, Q_TILE), 1)
    causal = kpos <= tpos

    def begin(i):
        qcat = _pair_queries(q_ref[i * Q_TILE:(i + 1) * Q_TILE, :])
        chosen = [None] * i
        if i > MOBA_TOPK:
            gate = _nt(km_hi, qcat) + _nt(km_lo, qcat)
            blk_id = lax.broadcasted_iota(jnp.int32, gate.shape, 0)
            g = jnp.where(blk_id < i, gate, NEG_INF)
            sel = jnp.zeros(gate.shape, F32)
            for _ in range(MOBA_TOPK):
                mx = jnp.max(g, axis=0, keepdims=True)
                first = jnp.min(jnp.where(g == mx, blk_id, n_blocks), axis=0, keepdims=True)
                hit = blk_id == first
                sel = jnp.where(hit, 1.0, sel)
                g = jnp.where(hit, -jnp.inf, g)
            chosen = [sel[n:n + 1, :] > 0.5 for n in range(i)]
        heads = [dict(m8=None, l8=jnp.zeros((SUBLANES, Q_TILE), F32), acc=jnp.zeros((HEAD_DIM, Q_TILE), F32))
                 for _ in range(2)]
        return dict(i=i, qcat=qcat, chosen=chosen, heads=heads)

    def score_tile(st, n, h):
        i = st["i"]
        hd = st["heads"][h]
        lanes = slice(h * Q_TILE, (h + 1) * Q_TILE)
        s = _nt(k_ref[n * KEY_TILE:(n + 1) * KEY_TILE, :], st["qcat"][lanes, :])
        if n == i:
            s = jnp.where(causal, s, NEG_INF)
        s_ref[i % 2, n, :, lanes] = s
        bm = _fold_rows(s, jnp.max)
        if n < i and st["chosen"][n] is not None:
            bm = jnp.where(st["chosen"][n][:, lanes], bm, NEG_INF)
        hd["m8"] = bm if hd["m8"] is None else jnp.maximum(hd["m8"], bm)

    def weight_tile(st, n, h):
        i = st["i"]
        hd = st["heads"][h]
        lanes = slice(h * Q_TILE, (h + 1) * Q_TILE)
        keep = None if (n == i or st["chosen"][n] is None) else st["chosen"][n][:, lanes]
        shift = hd["m"] if keep is None else jnp.where(keep, hd["m"], POS_BIG)
        p = jnp.exp2(s_ref[i % 2, n, :, lanes] - shift)
        hd["l8"] = hd["l8"] + _fold_rows(p, jnp.sum)
        hd["acc"] = hd["acc"] + _nn(vt_ref[n, h * HEAD_DIM:(h + 1) * HEAD_DIM, :], p.astype(BF16))

    cur = begin(0)
    for h in range(2):
        score_tile(cur, 0, h)
    for i in range(n_blocks):
        for hd in cur["heads"]:
            hd["m"] = jnp.max(hd["m8"], axis=0, keepdims=True)
        nxt = begin(i + 1) if i + 1 < n_blocks else None
        for n in range(i + 2):
            for h in range(2):
                if nxt is not None:
                    score_tile(nxt, n, h)
                if n <= i:
                    weight_tile(cur, n, h)
        acc = [hd["acc"] * (1.0 / jnp.sum(hd["l8"], axis=0, keepdims=True)) for hd in cur["heads"]]
        o_ref[i * Q_TILE:(i + 1) * Q_TILE, :] = _pair_output(acc[0], acc[1]).astype(o_ref.dtype)
        cur = nxt


def _moba_attention(qkv, batch, seq):
    n = qkv.shape[0]
    n_blocks = seq // MOBA_BLOCK
    pairs = N_HEADS // 2
    return pl.pallas_call(
        functools.partial(_moba_kernel, n_blocks=n_blocks),
        grid=(batch, pairs),
        in_specs=[pl.BlockSpec((seq, LANES), lambda b, p: (b, p)),
                  pl.BlockSpec((seq, LANES), lambda b, p: (b, pairs + p)),
                  pl.BlockSpec((seq, LANES), lambda b, p: (b, 2 * pairs + p))],
        out_specs=pl.BlockSpec((seq, LANES), lambda b, p: (b, p)),
        out_shape=jax.ShapeDtypeStruct((n, N_HEADS * HEAD_DIM), BF16),
        scratch_shapes=[pltpu.VMEM((n_blocks, LANES), F32),
                        pltpu.VMEM((n_blocks, LANES, KEY_TILE), BF16),
                        pltpu.VMEM((2, n_blocks, KEY_TILE, 2 * Q_TILE), F32)],
        compiler_params=pltpu.CompilerParams(
            dimension_semantics=("arbitrary", "arbitrary"),
            vmem_limit_bytes=VMEM_LIMIT),
        name="moba_attention",
    )(qkv, qkv, qkv)


def _log2_sigmoid_pair(z):
    log_beta = jnp.minimum(z, 0.0) - jnp.log2(1.0 + jnp.exp2(-jnp.abs(z)))
    return log_beta, log_beta - z


def _sb_kernel(q_ref, k_ref, v_ref, o_ref, vt_ref, acc_ref, carry_ref, *, n_tiles):
    w2 = 2 * Q_TILE
    near = 3
    _transpose_value_tiles(v_ref, vt_ref, n_tiles)
    r = lax.broadcasted_iota(jnp.int32, (KEY_TILE, KEY_TILE), 0)
    c = lax.broadcasted_iota(jnp.int32, (KEY_TILE, KEY_TILE), 1)
    later = jnp.where(c > r, 1.0, 0.0).astype(BF16)
    kpos = lax.broadcasted_iota(jnp.int32, (KEY_TILE, w2), 0)
    tpos = lax.broadcasted_iota(jnp.int32, (KEY_TILE, w2), 1) % Q_TILE
    strict = kpos < tpos

    def scores(state, t):
        st = state[t["i"]]
        t["z"] = _nt(k_ref[t["n"] * KEY_TILE:(t["n"] + 1) * KEY_TILE, :], st["qcat"])

    def log_terms(state, t):
        st = state[t["i"]]
        log_beta, log_1m = _log2_sigmoid_pair(t.pop("z"))
        if t["n"] == t["i"]:
            log_1m = jnp.where(strict, log_1m, 0.0)
        t["carry"] = st["carry"]
        if t["n"] > 0:
            tile_sum = jnp.sum(log_1m, axis=0, keepdims=True)
            st["carry"] = tile_sum if st["carry"] is None else st["carry"] + tile_sum
        t["log_beta"] = log_beta
        t["suffix"] = _nn(later, log_1m.astype(BF16))

    def weights(state, t):
        st = state[t["i"]]
        expo = t.pop("log_beta") + t.pop("suffix")
        if t["carry"] is not None:
            expo = expo + t["carry"]
        wgt = jnp.exp2(expo)
        if t["n"] == t["i"]:
            wgt = jnp.where(strict, wgt, 0.0)
        pv_a, pv_b = _pair_pv(vt_ref[t["n"]], wgt.astype(BF16))
        st["acc_a"] = st["acc_a"] + pv_a
        st["acc_b"] = st["acc_b"] + pv_b

    def run(state, tiles):
        stages = (scores, log_terms, weights)
        for step in range(len(tiles) + len(stages) - 1):
            for lag, stage in enumerate(stages):
                if 0 <= step - lag < len(tiles):
                    stage(state, tiles[step - lag])

    def query_operand(i):
        return _pair_queries(q_ref[i * Q_TILE:(i + 1) * Q_TILE, :])

    def write_out(i, st):
        o_ref[i * Q_TILE:(i + 1) * Q_TILE, :] = _pair_output(st["acc_a"], st["acc_b"]).astype(o_ref.dtype)

    zeros = jnp.zeros((HEAD_DIM, Q_TILE), F32)
    state = {i: dict(qcat=query_operand(i), carry=None, acc_a=zeros, acc_b=zeros) for i in range(n_tiles)}
    run(state, [dict(i=i, n=n) for i in range(n_tiles) for n in range(i, max(i - near, -1), -1)])
    for i in range(n_tiles):
        write_out(i, state[i])
        if i >= near:
            acc_ref[i] = jnp.concatenate([state[i]["acc_a"], state[i]["acc_b"]], axis=0)
            carry_ref[i] = state[i]["carry"]

    live = {i: jnp.max(state[i]["carry"]) > SB_UNDERFLOW_LOG2 for i in range(near, n_tiles)}
    for i in range(near, n_tiles):
        @pl.when(live[i])
        def _(i=i):
            acc = acc_ref[i]
            far = {i: dict(qcat=query_operand(i), carry=carry_ref[i],
                           acc_a=acc[:HEAD_DIM, :], acc_b=acc[HEAD_DIM:, :])}
            run(far, [dict(i=i, n=n) for n in range(i - near, -1, -1)])
            write_out(i, far[i])


def _sb_attention(qkv, batch, seq):
    n = qkv.shape[0]
    n_tiles = seq // KEY_TILE
    pairs = N_HEADS // 2
    return pl.pallas_call(
        functools.partial(_sb_kernel, n_tiles=n_tiles),
        grid=(batch, pairs),
        in_specs=[pl.BlockSpec((seq, LANES), lambda b, p: (b, p)),
                  pl.BlockSpec((seq, LANES), lambda b, p: (b, pairs + p)),
                  pl.BlockSpec((seq, LANES), lambda b, p: (b, 2 * pairs + p))],
        out_specs=pl.BlockSpec((seq, LANES), lambda b, p: (b, p)),
        out_shape=jax.ShapeDtypeStruct((n, N_HEADS * HEAD_DIM), BF16),
        scratch_shapes=[pltpu.VMEM((n_tiles, LANES, KEY_TILE), BF16),
                        pltpu.VMEM((n_tiles, LANES, Q_TILE), F32),
                        pltpu.VMEM((n_tiles, 1, 2 * Q_TILE), F32)],
        compiler_params=pltpu.CompilerParams(
            dimension_semantics=("arbitrary", "arbitrary"),
            vmem_limit_bytes=VMEM_LIMIT),
        name="stick_breaking_attention",
    )(qkv, qkv, qkv)


def _nsa_compress_kernel(ak_ref, av_ref, posk_ref, posv_ref, w1k_ref, w1v_ref, w2k_ref, w2v_ref,
                         kc_ref, vct_ref):
    half = NSA_CMP_STRIDE * HEAD_DIM
    n_rows = ak_ref.shape[1]

    def compress(a_ref, pos_ref, w1_ref, w2_ref):
        a = a_ref[0].astype(F32)
        first = (a + pos_ref[0:1, :]).astype(BF16)
        second = (a + pos_ref[1:2, :]).astype(BF16)
        pre = _nn(first, w1_ref[:half, :]) + pltpu.roll(_nn(second, w1_ref[half:, :]), n_rows - 1, 0)
        return _nn(jax.nn.gelu(pre).astype(BF16), w2_ref[...])

    kc_ref[0] = compress(ak_ref, posk_ref, w1k_ref, w2k_ref).astype(BF16)
    vct_ref[0] = compress(av_ref, posv_ref, w1v_ref, w2v_ref).T[:HEAD_DIM, :].astype(BF16)


def _nsa_compress(ak, av, posk, posv, w1k, w1v, w2k, w2v):
    bg, rows, feat = ak.shape
    blk = pl.BlockSpec((1, rows, feat), lambda i: (i, 0, 0))
    return pl.pallas_call(
        _nsa_compress_kernel,
        grid=(bg,),
        in_specs=[blk, blk, _const_spec(posk.shape), _const_spec(posv.shape),
                  _const_spec(w1k.shape), _const_spec(w1v.shape),
                  _const_spec(w2k.shape), _const_spec(w2v.shape)],
        out_specs=[pl.BlockSpec((1, rows, LANES), lambda i: (i, 0, 0)),
                   pl.BlockSpec((1, HEAD_DIM, rows), lambda i: (i, 0, 0))],
        out_shape=[jax.ShapeDtypeStruct((bg, rows, LANES), BF16),
                   jax.ShapeDtypeStruct((bg, HEAD_DIM, rows), BF16)],
        compiler_params=pltpu.CompilerParams(dimension_semantics=("arbitrary",),
                                             vmem_limit_bytes=VMEM_LIMIT),
        name="nsa_compress",
    )(ak, av, posk, posv, w1k, w1v, w2k, w2v)


def _nsa_kernel(q_ref, kcmp_ref, vcmpt_ref, ks_ref, kw_ref, vsw_ref, gate_ref, ovl_ref, o_ref,
                vst_ref, vwt_ref, gt_ref, ssel_ref, swin_ref, *, n_tiles, n_sel):
    grp = pl.program_id(1)
    hg = NSA_HEADS_PER_GROUP
    t = Q_TILE
    wq = hg * t
    sub = KEY_TILE // NSA_SLC_BLOCK
    back = NSA_WINDOW // KEY_TILE
    n_top = min(NSA_SLC_TOPN, n_sel)

    for n in range(n_tiles):
        rows = slice(n * KEY_TILE, (n + 1) * KEY_TILE)
        blk = vsw_ref[rows, :].astype(F32).T.astype(BF16)
        vst_ref[n] = blk[:HEAD_DIM, :]
        vwt_ref[n] = blk[HEAD_DIM:, :]
        gt_ref[:, rows] = jax.nn.sigmoid(gate_ref[rows, :]).T
    kloc = lax.broadcasted_iota(jnp.int32, (KEY_TILE, wq), 0)
    tloc = lax.broadcasted_iota(jnp.int32, (KEY_TILE, wq), 1) % t
    causal = kloc <= tloc
    window_tail = kloc > tloc

    def begin(i):
        rows = slice(i * t, (i + 1) * t)
        qcat = jnp.concatenate([_pair_queries(q_ref[rows, :LANES]),
                                _pair_queries(q_ref[rows, LANES:])], axis=0)
        tq = i * t + lax.broadcasted_iota(jnp.int32, (1, wq), 1) % t

        s = _nt(kcmp_ref[0], qcat)
        cmp_end = lax.broadcasted_iota(jnp.int32, s.shape, 0) * NSA_CMP_STRIDE + (NSA_CMP_BLOCK - 1)
        valid = cmp_end <= tq
        m = jnp.max(jnp.where(valid, s, NEG_INF), axis=0, keepdims=True)
        e = jnp.where(valid, jnp.exp2(s - m), 0.0)
        l = jnp.sum(e, axis=0, keepdims=True)
        p_cmp = e * jnp.where(l > 0.0, 1.0 / l, 0.0)
        o_cmp = _nn(vcmpt_ref[0], p_cmp.astype(BF16))

        bias = None
        if sub * (i + 1) > n_top:
            p_grp = p_cmp[:, :t]
            for h in range(1, hg):
                p_grp = p_grp + p_cmp[:, h * t:(h + 1) * t]
            p_hi, p_lo = _split_bf16(p_grp)
            imp = _nn(ovl_ref[...], p_hi) + _nn(ovl_ref[...], p_lo)
            j = lax.broadcasted_iota(jnp.int32, imp.shape, 0)
            q_blk = (i * t + lax.broadcasted_iota(jnp.int32, imp.shape, 1)) // NSA_SLC_BLOCK
            forced = jnp.logical_or(j == 0, jnp.logical_or(j == q_blk, j == q_blk - 1))
            score = jnp.where(j <= q_blk, jnp.where(forced, POS_BIG, imp), NEG_INF)
            sel = jnp.zeros(imp.shape, F32)
            for _ in range(n_top):
                mx = jnp.max(score, axis=0, keepdims=True)
                first = jnp.min(jnp.where(score == mx, j, n_sel), axis=0, keepdims=True)
                hit = j == first
                sel = jnp.where(jnp.logical_and(hit, mx > 0.5 * NEG_INF), 1.0, sel)
                score = jnp.where(hit, -jnp.inf, score)
            bias = jnp.concatenate([jnp.where(sel > 0.5, 0.0, NEG_INF)] * hg, axis=1)
        zeros8 = jnp.zeros((SUBLANES, wq), F32)
        zeros_acc = jnp.zeros((HEAD_DIM, wq), F32)
        return dict(i=i, qcat=qcat, o_cmp=o_cmp, bias=bias,
                    sel=dict(m8=None, l8=zeros8, acc=zeros_acc),
                    win=dict(m8=None, l8=zeros8, acc=zeros_acc))

    def note_max(br, s):
        bm = _fold_rows(s, jnp.max)
        br["m8"] = bm if br["m8"] is None else jnp.maximum(br["m8"], bm)

    def sel_scores(st, n):
        s = _nt(ks_ref[n * KEY_TILE:(n + 1) * KEY_TILE, :], st["qcat"])
        if st["bias"] is not None:
            s = jnp.concatenate(
                [s[jj * NSA_SLC_BLOCK:(jj + 1) * NSA_SLC_BLOCK, :]
                 + st["bias"][n * sub + jj:n * sub + jj + 1, :] for jj in range(sub)], axis=0)
        if n == st["i"]:
            s = jnp.where(causal, s, NEG_INF)
        ssel_ref[n] = s
        note_max(st["sel"], s)

    def win_scores(st, n):
        s = _nt(kw_ref[n * KEY_TILE:(n + 1) * KEY_TILE, :], st["qcat"])
        if n == st["i"]:
            s = jnp.where(causal, s, NEG_INF)
        elif n == st["i"] - back:
            s = jnp.where(window_tail, s, NEG_INF)
        swin_ref[st["i"] - n] = s
        note_max(st["win"], s)

    def weights(br, s_view, v_view):
        p = jnp.exp2(s_view[...] - br["m"])
        br["l8"] = br["l8"] + _fold_rows(p, jnp.sum)
        br["acc"] = br["acc"] + _nn(v_view[...], p.astype(BF16))

    def finish(st):
        i = st["i"]
        o_slc, o_win = [br["acc"] * (1.0 / jnp.sum(br["l8"], axis=0, keepdims=True))
                        for br in (st["sel"], st["win"])]
        gates = gt_ref[pl.ds(pl.multiple_of(grp * 16, 16), 16), i * t:(i + 1) * t]
        heads = []
        for h in range(hg):
            cols = slice(h * t, (h + 1) * t)
            heads.append(gates[3 * h:3 * h + 1, :] * st["o_cmp"][:, cols]
                         + gates[3 * h + 1:3 * h + 2, :] * o_slc[:, cols]
                         + gates[3 * h + 2:3 * h + 3, :] * o_win[:, cols])
        for pair in range(hg // 2):
            both = jnp.concatenate([heads[2 * pair], heads[2 * pair + 1]], axis=0)
            o_ref[i * t:(i + 1) * t, pair * LANES:(pair + 1) * LANES] = both.T.astype(o_ref.dtype)

    def interleave(first, second):
        for k in range(max(len(first), len(second))):
            for steps in (first, second):
                if k < len(steps):
                    steps[k]()

    def win_tiles(i):
        return [n for n in range(i, i - back - 1, -1) if n >= 0]

    cur = begin(0)
    sel_scores(cur, 0)
    for i in range(n_tiles):
        cur["sel"]["m"] = jnp.max(cur["sel"]["m8"], axis=0, keepdims=True)
        interleave([functools.partial(win_scores, cur, n) for n in win_tiles(i)],
                   [functools.partial(weights, cur["sel"], ssel_ref.at[n], vst_ref.at[n])
                    for n in range(i + 1)])
        cur["win"]["m"] = jnp.max(cur["win"]["m8"], axis=0, keepdims=True)
        nxt = begin(i + 1) if i + 1 < n_tiles else None
        interleave([functools.partial(sel_scores, nxt, n) for n in range(i + 2)] if nxt else [],
                   [functools.partial(weights, cur["win"], swin_ref.at[i - n], vwt_ref.at[n])
                    for n in win_tiles(i)])
        finish(cur)
        cur = nxt


def _nsa_attention(proj, gates, kcmp, vcmpt, overlap_t, batch, seq):
    n = proj.shape[0]
    n_tiles = seq // KEY_TILE
    n_sel = seq // NSA_SLC_BLOCK
    g = NSA_KV_GROUPS
    hg = NSA_HEADS_PER_GROUP
    n_cmp_pad = kcmp.shape[1]
    back = NSA_WINDOW // KEY_TILE
    ks0, kw0, vsw0 = 12, 16, 20
    return pl.pallas_call(
        functools.partial(_nsa_kernel, n_tiles=n_tiles, n_sel=n_sel),
        grid=(batch, g),
        in_specs=[pl.BlockSpec((seq, hg * HEAD_DIM), lambda b, k: (b, k)),
                  pl.BlockSpec((1, n_cmp_pad, LANES), lambda b, k: (b * g + k, 0, 0)),
                  pl.BlockSpec((1, HEAD_DIM, n_cmp_pad), lambda b, k: (b * g + k, 0, 0)),
                  pl.BlockSpec((seq, LANES), lambda b, k: (b, ks0 + k)),
                  pl.BlockSpec((seq, LANES), lambda b, k: (b, kw0 + k)),
                  pl.BlockSpec((seq, LANES), lambda b, k: (b, vsw0 + k)),
                  pl.BlockSpec((seq, LANES), lambda b, k: (b, 0)),
                  _const_spec(overlap_t.shape)],
        out_specs=pl.BlockSpec((seq, hg * HEAD_DIM), lambda b, k: (b, k)),
        out_shape=jax.ShapeDtypeStruct((n, N_HEADS * HEAD_DIM), BF16),
        scratch_shapes=[pltpu.VMEM((n_tiles, HEAD_DIM, KEY_TILE), BF16),
                        pltpu.VMEM((n_tiles, HEAD_DIM, KEY_TILE), BF16),
                        pltpu.VMEM((LANES, seq), F32),
                        pltpu.VMEM((n_tiles, KEY_TILE, hg * Q_TILE), F32),
                        pltpu.VMEM((back + 1, KEY_TILE, hg * Q_TILE), F32)],
        compiler_params=pltpu.CompilerParams(
            dimension_semantics=("arbitrary", "arbitrary"),
            vmem_limit_bytes=VMEM_LIMIT),
        name="nsa_attention",
    )(proj, kcmp, vcmpt, proj, proj, proj, gates, overlap_t)


def _rope_tables(seq):
    inv_freq = 1.0 / (ROPE_THETA ** (jnp.arange(0, HEAD_DIM, 2, dtype=F32) / HEAD_DIM))
    ang = jnp.arange(seq, dtype=F32)[:, None] * inv_freq[None, :]
    reps = LANES // (HEAD_DIM // 2)
    cos = jnp.tile(jnp.cos(ang), (1, reps))
    sin = jnp.tile(jnp.sin(ang), (1, reps))
    first_half = (jnp.arange(LANES) % HEAD_DIM) < HEAD_DIM // 2
    return cos, jnp.where(first_half, -sin, 0.0), jnp.where(first_half, 0.0, sin)


def _nsa_weight_layout(w_in):
    qd = N_HEADS * HEAD_DIM
    kv = NSA_KV_COLS
    q, kc, vc, ks, vs, kw, vw, gt = jnp.split(
        w_in, np.cumsum([qd] + [kv] * 6).tolist(), axis=1)

    def dup(w):
        w = w.reshape(D_MODEL, NSA_KV_GROUPS, 1, HEAD_DIM)
        return jnp.broadcast_to(w, (D_MODEL, NSA_KV_GROUPS, 2, HEAD_DIM)).reshape(D_MODEL, -1)

    vsw = jnp.concatenate([vs.reshape(D_MODEL, NSA_KV_GROUPS, 1, HEAD_DIM),
                           vw.reshape(D_MODEL, NSA_KV_GROUPS, 1, HEAD_DIM)], axis=2)
    main = jnp.concatenate([q, kc, vc, dup(ks), dup(kw), vsw.reshape(D_MODEL, -1)], axis=1)
    per_group = 3 * NSA_HEADS_PER_GROUP
    gt = gt.reshape(D_MODEL, NSA_KV_GROUPS, per_group)
    gt = jnp.pad(gt, ((0, 0), (0, 0), (0, 16 - per_group))).reshape(D_MODEL, -1)
    gt = jnp.pad(gt, ((0, 0), (0, LANES - gt.shape[1])))
    return main.astype(BF16), gt.astype(BF16)


def _overlap_t(seq):
    n_cmp = (seq - NSA_CMP_BLOCK) // NSA_CMP_STRIDE + 1
    n_sel = seq // NSA_SLC_BLOCK
    c_start = np.arange(n_cmp) * NSA_CMP_STRIDE
    s_start = np.arange(n_sel) * NSA_SLC_BLOCK
    lo = np.maximum(c_start[:, None], s_start[None, :])
    hi = np.minimum(c_start[:, None] + NSA_CMP_BLOCK, s_start[None, :] + NSA_SLC_BLOCK)
    ov = np.clip(hi - lo, 0, None) / NSA_CMP_BLOCK
    ov = np.pad(ov, ((0, seq // NSA_CMP_STRIDE - n_cmp), (0, 0)))
    return jnp.asarray(ov.T, dtype=BF16)


_QK_ROPE = (True,) * 16 + (False,) * 8
_NO_ROPE = (False,) * 24
_NSA_ROPE = (True,) * 10 + (False,) * 2 + (True,) * 8 + (False,) * 4


def kernel(x, l0_ln_mix_pre, l0_w_in, l0_w_out, l0_ln_mix_post, l0_ln_mlp_pre, l0_w_up, l0_w_down, l0_ln_mlp_post, l1_ln_mix_pre, l1_w_in, l1_w_out, l1_ln_mix_post, l1_ln_mlp_pre, l1_w_up, l1_w_down, l1_ln_mlp_post, l2_ln_mix_pre, l2_w_in, l2_cmp_pos_k, l2_cmp_w1_k, l2_cmp_w2_k, l2_cmp_pos_v, l2_cmp_w1_v, l2_cmp_w2_v, l2_w_out, l2_ln_mix_post, l2_ln_mlp_pre, l2_w_up, l2_w_down, l2_ln_mlp_post, l3_ln_mix_pre, l3_w_in, l3_w_out, l3_ln_mix_post, l3_ln_mlp_pre, l3_w_up, l3_w_down, l3_ln_mlp_post):
    batch, seq, _ = x.shape
    assert seq % ROW_TILE == 0 and seq % KEY_TILE == 0 and Q_TILE == KEY_TILE == MOBA_BLOCK
    tables = _rope_tables(seq)
    x2 = x.reshape(batch * seq, D_MODEL)
    bf = lambda w: w.astype(BF16)

    def tail(o, x2, w_out, g_post, g_pre, w_up, w_down, g_mlp):
        return _layer_tail(o, x2, bf(w_out), g_post, g_pre, bf(w_up), bf(w_down), g_mlp)

    qkv = _norm_proj(x2, l0_ln_mix_pre, bf(l0_w_in), tables, _QK_ROPE, seq)
    x2 = tail(_moba_attention(qkv, batch, seq), x2, l0_w_out, l0_ln_mix_post, l0_ln_mlp_pre,
              l0_w_up, l0_w_down, l0_ln_mlp_post)

    qkv = _norm_proj(x2, l1_ln_mix_pre, bf(l1_w_in), tables, _NO_ROPE, seq)
    x2 = tail(_sb_attention(qkv, batch, seq), x2, l1_w_out, l1_ln_mix_post, l1_ln_mlp_pre,
              l1_w_up, l1_w_down, l1_ln_mlp_post)

    w_main, w_gate = _nsa_weight_layout(l2_w_in)
    proj, gates = _norm_proj(x2, l2_ln_mix_pre, w_main, tables, _NSA_ROPE, seq, wg=w_gate)
    qd = N_HEADS * HEAD_DIM
    rows = seq // NSA_CMP_STRIDE

    def row_groups(cols):
        a = cols.reshape(batch, seq, NSA_KV_GROUPS, HEAD_DIM).transpose(0, 2, 1, 3)
        return a.reshape(batch * NSA_KV_GROUPS, rows, NSA_CMP_STRIDE * HEAD_DIM)

    flat_pos = lambda p: p.reshape(2, NSA_CMP_STRIDE * HEAD_DIM)
    w2_dup = lambda w: bf(jnp.concatenate([w, w], axis=1))
    kcmp, vcmpt = _nsa_compress(
        row_groups(proj[:, qd:qd + NSA_KV_COLS]), row_groups(proj[:, qd + NSA_KV_COLS:qd + 2 * NSA_KV_COLS]),
        flat_pos(l2_cmp_pos_k), flat_pos(l2_cmp_pos_v), bf(l2_cmp_w1_k), bf(l2_cmp_w1_v),
        w2_dup(l2_cmp_w2_k), w2_dup(l2_cmp_w2_v))
    o = _nsa_attention(proj, gates, kcmp, vcmpt, _overlap_t(seq), batch, seq)
    x2 = tail(o, x2, l2_w_out, l2_ln_mix_post, l2_ln_mlp_pre, l2_w_up, l2_w_down, l2_ln_mlp_post)

    qkv = _norm_proj(x2, l3_ln_mix_pre, bf(l3_w_in), tables, _QK_ROPE, seq)
    x2 = tail(_moba_attention(qkv, batch, seq), x2, l3_w_out, l3_ln_mix_post, l3_ln_mlp_pre,
              l3_w_up, l3_w_down, l3_ln_mlp_post)
    return x2.reshape(batch, seq, D_MODEL)
```

```python
import functools
import math

import numpy as np
import jax
import jax.numpy as jnp
from jax import lax
from jax.experimental import pallas as pl
from jax.experimental.pallas import tpu as pltpu

D_MODEL = 1024
N_HEADS = 16
HEAD_DIM = 64
D_FF = 4 * D_MODEL
ROPE_THETA = 10000.0
NORM_EPS = 1e-6
NEG_INF = -1e30
POS_BIG = 1e30

MOBA_BLOCK = 256
MOBA_TOPK = 3

NSA_KV_GROUPS = 4
NSA_HEADS_PER_GROUP = N_HEADS // NSA_KV_GROUPS
NSA_CMP_BLOCK = 32
NSA_CMP_STRIDE = 16
NSA_CMP_HIDDEN = 256
NSA_SLC_BLOCK = 64
NSA_SLC_TOPN = 16
NSA_WINDOW = 512
NSA_KV_COLS = NSA_KV_GROUPS * HEAD_DIM

LANES = 128
SUBLANES = 8
KEY_TILE = 256
Q_TILE = 256
ROW_TILE = 512
FF_CHUNK = 512
PROJ_COLS = 3 * N_HEADS * HEAD_DIM
VMEM_LIMIT = 56 * 1024 * 1024
QUERY_SCALE = HEAD_DIM ** -0.5 * math.log2(math.e)
SB_UNDERFLOW_LOG2 = -150.0

F32 = jnp.float32
BF16 = jnp.bfloat16


def _nt(a, b):
    return lax.dot_general(a, b, (((1,), (1,)), ((), ())), preferred_element_type=F32)


def _nn(a, b):
    return jnp.dot(a, b, preferred_element_type=F32)


def _split_bf16(x):
    hi = x.astype(BF16)
    lo = (x - hi.astype(F32)).astype(BF16)
    return hi, lo


def _rms(x, g):
    ms = jnp.mean(x * x, axis=-1, keepdims=True)
    return x * lax.rsqrt(ms + NORM_EPS) * g


def _const_spec(shape):
    nd = len(shape)
    return pl.BlockSpec(shape, lambda *_: (0,) * nd, pipeline_mode=pl.Buffered(1))


def _proj_kernel(x_ref, g_ref, w_ref, cos_ref, sina_ref, sinb_ref, *rest, rope_slabs, has_gate):
    if has_gate:
        wg_ref, o_ref, og_ref = rest
    else:
        (o_ref,) = rest
    h = _rms(x_ref[...], g_ref[...]).astype(BF16)
    cos = cos_ref[...]
    sina = sina_ref[...]
    sinb = sinb_ref[...]
    n_slabs = len(rope_slabs)
    for c in range(n_slabs // 2):
        y = _nn(h, w_ref[:, c * 2 * LANES:(c + 1) * 2 * LANES])
        for half in range(2):
            s = 2 * c + half
            ys = y[:, half * LANES:(half + 1) * LANES]
            if s < N_HEADS * HEAD_DIM // LANES:
                ys = ys * QUERY_SCALE
            if rope_slabs[s]:
                ys = (ys * cos + pltpu.roll(ys, LANES - HEAD_DIM // 2, 1) * sina
                      + pltpu.roll(ys, HEAD_DIM // 2, 1) * sinb)
            o_ref[:, s * LANES:(s + 1) * LANES] = ys.astype(o_ref.dtype)
    if has_gate:
        og_ref[...] = _nn(h, wg_ref[...])


def _norm_proj(x2, g, w, tables, rope_slabs, seq, wg=None):
    n = x2.shape[0]
    cols = w.shape[1]
    pos_blocks = seq // ROW_TILE
    has_gate = wg is not None
    row = lambda i: (i, 0)
    tab = pl.BlockSpec((ROW_TILE, LANES), lambda i: (i % pos_blocks, 0))
    in_specs = [pl.BlockSpec((ROW_TILE, D_MODEL), row), _const_spec((1, D_MODEL)),
                _const_spec((D_MODEL, cols)), tab, tab, tab]
    args = [x2, g.reshape(1, D_MODEL), w, *tables]
    out_shape = [jax.ShapeDtypeStruct((n, cols), BF16)]
    out_specs = [pl.BlockSpec((ROW_TILE, cols), row)]
    if has_gate:
        in_specs.append(_const_spec((D_MODEL, LANES)))
        args.append(wg)
        out_shape.append(jax.ShapeDtypeStruct((n, LANES), F32))
        out_specs.append(pl.BlockSpec((ROW_TILE, LANES), row))
    outs = pl.pallas_call(
        functools.partial(_proj_kernel, rope_slabs=rope_slabs, has_gate=has_gate),
        grid=(n // ROW_TILE,),
        in_specs=in_specs,
        out_specs=out_specs,
        out_shape=out_shape,
        compiler_params=pltpu.CompilerParams(dimension_semantics=("arbitrary",),
                                             vmem_limit_bytes=VMEM_LIMIT),
        name="norm_proj",
    )(*args)
    return outs if has_gate else outs[0]


def _tail_kernel(o_ref, x_ref, wout_ref, gpost_ref, gpre_ref, wup_ref, wdown_ref, gmlp_ref, out_ref):
    m = _nn(o_ref[...], wout_ref[...])
    x1 = x_ref[...] + _rms(m, gpost_ref[...])
    h = _rms(x1, gpre_ref[...]).astype(BF16)
    acc = jnp.zeros((ROW_TILE, D_MODEL), F32)
    for j in range(D_FF // FF_CHUNK):
        u = _nn(h, wup_ref[:, j * FF_CHUNK:(j + 1) * FF_CHUNK])
        a = jnp.square(jnp.maximum(u, 0.0)).astype(BF16)
        acc = acc + _nn(a, wdown_ref[j * FF_CHUNK:(j + 1) * FF_CHUNK, :])
    out_ref[...] = x1 + _rms(acc, gmlp_ref[...])


def _layer_tail(o2, x2, wout, gpost, gpre, wup, wdown, gmlp):
    n = x2.shape[0]
    row = lambda i: (i, 0)
    vec = lambda v: v.reshape(1, D_MODEL)
    return pl.pallas_call(
        _tail_kernel,
        grid=(n // ROW_TILE,),
        in_specs=[pl.BlockSpec((ROW_TILE, N_HEADS * HEAD_DIM), row),
                  pl.BlockSpec((ROW_TILE, D_MODEL), row),
                  _const_spec((N_HEADS * HEAD_DIM, D_MODEL)), _const_spec((1, D_MODEL)),
                  _const_spec((1, D_MODEL)), _const_spec((D_MODEL, D_FF)),
                  _const_spec((D_FF, D_MODEL)), _const_spec((1, D_MODEL))],
        out_specs=pl.BlockSpec((ROW_TILE, D_MODEL), row),
        out_shape=jax.ShapeDtypeStruct((n, D_MODEL), F32),
        compiler_params=pltpu.CompilerParams(dimension_semantics=("arbitrary",),
                                             vmem_limit_bytes=VMEM_LIMIT),
        name="layer_tail",
    )(o2, x2, wout, vec(gpost), vec(gpre), wup, wdown, vec(gmlp))


def _pair_queries(q):
    lane = lax.broadcasted_iota(jnp.int32, q.shape, 1)
    zero = jnp.zeros_like(q)
    return jnp.concatenate([jnp.where(lane < HEAD_DIM, q, zero),
                            jnp.where(lane >= HEAD_DIM, q, zero)], axis=0)


def _transpose_value_tiles(v_ref, vt_ref, n_tiles):
    for n in range(n_tiles):
        blk = v_ref[n * KEY_TILE:(n + 1) * KEY_TILE, :].astype(F32)
        vt_ref[n] = blk.T.astype(BF16)


def _fold_rows(x, op):
    r, w = x.shape
    return op(x.reshape(r // SUBLANES, SUBLANES, w), axis=0)


def _pair_pv(vt, p):
    t = p.shape[1] // 2
    return _nn(vt[:HEAD_DIM, :], p[:, :t]), _nn(vt[HEAD_DIM:, :], p[:, t:])


def _pair_output(acc_a, acc_b, inv_l=None):
    if inv_l is not None:
        t = acc_a.shape[1]
        acc_a = acc_a * inv_l[:, :t]
        acc_b = acc_b * inv_l[:, t:]
    return jnp.concatenate([acc_a, acc_b], axis=0).T


def _moba_kernel(q_ref, k_ref, v_ref, o_ref, kmean_ref, vt_ref, s_ref, *, n_blocks):
    for n in range(n_blocks):
        blk = k_ref[n * MOBA_BLOCK:(n + 1) * MOBA_BLOCK, :].astype(F32)
        kmean_ref[n:n + 1, :] = jnp.mean(blk, axis=0, keepdims=True)
    _transpose_value_tiles(v_ref, vt_ref, n_blocks)
    km_hi, km_lo = _split_bf16(kmean_ref[...])
    kpos = lax.broadcasted_iota(jnp.int32, (KEY_TILE, Q_TILE), 0)
    tpos = lax.broadcasted_iota(jnp.int32, (KEY_TILE, Q_TILE), 1)
    causal = kpos <= tpos

    def begin(i):
        qcat = _pair_queries(q_ref[i * Q_TILE:(i + 1) * Q_TILE, :])
        chosen = [None] * i
        if i > MOBA_TOPK:
            gate = _nt(km_hi, qcat) + _nt(km_lo, qcat)
            blk_id = lax.broadcasted_iota(jnp.int32, gate.shape, 0)
            g = jnp.where(blk_id < i, gate, NEG_INF)
            sel = jnp.zeros(gate.shape, F32)
            for _ in range(MOBA_TOPK):
                mx = jnp.max(g, axis=0, keepdims=True)
                first = jnp.min(jnp.where(g == mx, blk_id, n_blocks), axis=0, keepdims=True)
                hit = blk_id == first
                sel = jnp.where(hit, 1.0, sel)
                g = jnp.where(hit, -jnp.inf, g)
            chosen = [sel[n:n + 1, :] > 0.5 for n in range(i)]
        heads = [dict(m8=None, l8=jnp.zeros((SUBLANES, Q_TILE), F32), acc=jnp.zeros((HEAD_DIM, Q_TILE), F32))
                 for _ in range(2)]
        return dict(i=i, qcat=qcat, chosen=chosen, heads=heads)

    def score_tile(st, n, h):
        i = st["i"]
        hd = st["heads"][h]
        lanes = slice(h * Q_TILE, (h + 1) * Q_TILE)
        s = _nt(k_ref[n * KEY_TILE:(n + 1) * KEY_TILE, :], st["qcat"][lanes, :])
        if n == i:
            s = jnp.where(causal, s, NEG_INF)
        s_ref[i % 2, n, :, lanes] = s
        bm = _fold_rows(s, jnp.max)
        if n < i and st["chosen"][n] is not None:
            bm = jnp.where(st["chosen"][n][:, lanes], bm, NEG_INF)
        hd["m8"] = bm if hd["m8"] is None else jnp.maximum(hd["m8"], bm)

    def weight_tile(st, n, h):
        i = st["i"]
        hd = st["heads"][h]
        lanes = slice(h * Q_TILE, (h + 1) * Q_TILE)
        keep = None if (n == i or st["chosen"][n] is None) else st["chosen"][n][:, lanes]
        shift = hd["m"] if keep is None else jnp.where(keep, hd["m"], POS_BIG)
        p = jnp.exp2(s_ref[i % 2, n, :, lanes] - shift)
        hd["l8"] = hd["l8"] + _fold_rows(p, jnp.sum)
        hd["acc"] = hd["acc"] + _nn(vt_ref[n, h * HEAD_DIM:(h + 1) * HEAD_DIM, :], p.astype(BF16))

    cur = begin(0)
    for h in range(2):
        score_tile(cur, 0, h)
    for i in range(n_blocks):
        for hd in cur["heads"]:
            hd["m"] = jnp.max(hd["m8"], axis=0, keepdims=True)
        nxt = begin(i + 1) if i + 1 < n_blocks else None
        for n in range(i + 2):
            for h in range(2):
                if nxt is not None:
                    score_tile(nxt, n, h)
                if n <= i:
                    weight_tile(cur, n, h)
        acc = [hd["acc"] * (1.0 / jnp.sum(hd["l8"], axis=0, keepdims=True)) for hd in cur["heads"]]
        o_ref[i * Q_TILE:(i + 1) * Q_TILE, :] = _pair_output(acc[0], acc[1]).astype(o_ref.dtype)
        cur = nxt


def _moba_attention(qkv, batch, seq):
    n = qkv.shape[0]
    n_blocks = seq // MOBA_BLOCK
    pairs = N_HEADS // 2
    return pl.pallas_call(
        functools.partial(_moba_kernel, n_blocks=n_blocks),
        grid=(batch, pairs),
        in_specs=[pl.BlockSpec((seq, LANES), lambda b, p: (b, p)),
                  pl.BlockSpec((seq, LANES), lambda b, p: (b, pairs + p)),
                  pl.BlockSpec((seq, LANES), lambda b, p: (b, 2 * pairs + p))],
        out_specs=pl.BlockSpec((seq, LANES), lambda b, p: (b, p)),
        out_shape=jax.ShapeDtypeStruct((n, N_HEADS * HEAD_DIM), BF16),
        scratch_shapes=[pltpu.VMEM((n_blocks, LANES), F32),
                        pltpu.VMEM((n_blocks, LANES, KEY_TILE), BF16),
                        pltpu.VMEM((2, n_blocks, KEY_TILE, 2 * Q_TILE), F32)],
        compiler_params=pltpu.CompilerParams(
            dimension_semantics=("arbitrary", "arbitrary"),
            vmem_limit_bytes=VMEM_LIMIT),
        name="moba_attention",
    )(qkv, qkv, qkv)


def _log2_sigmoid_pair(z):
    log_beta = jnp.minimum(z, 0.0) - jnp.log2(1.0 + jnp.exp2(-jnp.abs(z)))
    return log_beta, log_beta - z


def _sb_kernel(q_ref, k_ref, v_ref, o_ref, vt_ref, acc_ref, carry_ref, *, n_tiles):
    near = 3
    heads = (0, 1)
    _transpose_value_tiles(v_ref, vt_ref, n_tiles)
    r = lax.broadcasted_iota(jnp.int32, (KEY_TILE, KEY_TILE), 0)
    c = lax.broadcasted_iota(jnp.int32, (KEY_TILE, KEY_TILE), 1)
    later = jnp.where(c > r, 1.0, 0.0).astype(BF16)
    strict = r < c

    def scores(state, t):
        st = state[t["i"], t["h"]]
        t["z"] = _nt(k_ref[t["n"] * KEY_TILE:(t["n"] + 1) * KEY_TILE, :], st["q"])

    def log_terms(state, t):
        st = state[t["i"], t["h"]]
        log_beta, log_1m = _log2_sigmoid_pair(t.pop("z"))
        if t["n"] == t["i"]:
            log_1m = jnp.where(strict, log_1m, 0.0)
        t["carry"] = st["carry"]
        if t["n"] > 0:
            tile_sum = jnp.sum(log_1m, axis=0, keepdims=True)
            st["carry"] = tile_sum if st["carry"] is None else st["carry"] + tile_sum
        t["log_beta"] = log_beta
        t["suffix"] = _nn(later, log_1m.astype(BF16))

    def weights(state, t):
        st, h = state[t["i"], t["h"]], t["h"]
        expo = t.pop("log_beta") + t.pop("suffix")
        if t["carry"] is not None:
            expo = expo + t["carry"]
        wgt = jnp.exp2(expo)
        if t["n"] == t["i"]:
            wgt = jnp.where(strict, wgt, 0.0)
        st["acc"] = st["acc"] + _nn(vt_ref[t["n"], h * HEAD_DIM:(h + 1) * HEAD_DIM, :], wgt.astype(BF16))

    def run(state, tiles):
        stages = (scores, log_terms, weights)
        for step in range(len(tiles) + len(stages) - 1):
            for lag, stage in enumerate(stages):
                if 0 <= step - lag < len(tiles):
                    stage(state, tiles[step - lag])

    def query_operands(i):
        qcat = _pair_queries(q_ref[i * Q_TILE:(i + 1) * Q_TILE, :])
        return [qcat[h * Q_TILE:(h + 1) * Q_TILE, :] for h in heads]

    def write_out(i, state):
        o_ref[i * Q_TILE:(i + 1) * Q_TILE, :] = _pair_output(
            state[i, 0]["acc"], state[i, 1]["acc"]).astype(o_ref.dtype)

    zeros = jnp.zeros((HEAD_DIM, Q_TILE), F32)
    state = {}
    for i in range(n_tiles):
        for h, q in zip(heads, query_operands(i)):
            state[i, h] = dict(q=q, carry=None, acc=zeros)
    run(state, [dict(i=i, n=n, h=h) for i in range(n_tiles)
                for n in range(i, max(i - near, -1), -1) for h in heads])
    for i in range(n_tiles):
        write_out(i, state)
        if i >= near:
            acc_ref[i] = jnp.concatenate([state[i, h]["acc"] for h in heads], axis=0)
            carry_ref[i] = jnp.concatenate([state[i, h]["carry"] for h in heads], axis=1)

    live = {i: jnp.max(jnp.maximum(state[i, 0]["carry"], state[i, 1]["carry"])) > SB_UNDERFLOW_LOG2
            for i in range(near, n_tiles)}
    for i in range(near, n_tiles):
        @pl.when(live[i])
        def _(i=i):
            acc, carry = acc_ref[i], carry_ref[i]
            far = {}
            for h, q in zip(heads, query_operands(i)):
                far[i, h] = dict(q=q, carry=carry[:, h * Q_TILE:(h + 1) * Q_TILE],
                                 acc=acc[h * HEAD_DIM:(h + 1) * HEAD_DIM, :])
            run(far, [dict(i=i, n=n, h=h) for n in range(i - near, -1, -1) for h in heads])
            write_out(i, far)


def _sb_attention(qkv, batch, seq):
    n = qkv.shape[0]
    n_tiles = seq // KEY_TILE
    pairs = N_HEADS // 2
    return pl.pallas_call(
        functools.partial(_sb_kernel, n_tiles=n_tiles),
        grid=(batch, pairs),
        in_specs=[pl.BlockSpec((seq, LANES), lambda b, p: (b, p)),
                  pl.BlockSpec((seq, LANES), lambda b, p: (b, pairs + p)),
                  pl.BlockSpec((seq, LANES), lambda b, p: (b, 2 * pairs + p))],
        out_specs=pl.BlockSpec((seq, LANES), lambda b, p: (b, p)),
        out_shape=jax.ShapeDtypeStruct((n, N_HEADS * HEAD_DIM), BF16),
        scratch_shapes=[pltpu.VMEM((n_tiles, LANES, KEY_TILE), BF16),
                        pltpu.VMEM((n_tiles, LANES, Q_TILE), F32),
                        pltpu.VMEM((n_tiles, 1, 2 * Q_TILE), F32)],
        compiler_params=pltpu.CompilerParams(
            dimension_semantics=("arbitrary", "arbitrary"),
            vmem_limit_bytes=VMEM_LIMIT),
        name="stick_breaking_attention",
    )(qkv, qkv, qkv)


def _nsa_compress_kernel(ak_ref, av_ref, posk_ref, posv_ref, w1k_ref, w1v_ref, w2k_ref, w2v_ref,
                         kc_ref, vct_ref):
    half = NSA_CMP_STRIDE * HEAD_DIM
    n_rows = ak_ref.shape[1]

    def compress(a_ref, pos_ref, w1_ref, w2_ref):
        a = a_ref[0].astype(F32)
        first = (a + pos_ref[0:1, :]).astype(BF16)
        second = (a + pos_ref[1:2, :]).astype(BF16)
        pre = _nn(first, w1_ref[:half, :]) + pltpu.roll(_nn(second, w1_ref[half:, :]), n_rows - 1, 0)
        return _nn(jax.nn.gelu(pre).astype(BF16), w2_ref[...])

    kc_ref[0] = compress(ak_ref, posk_ref, w1k_ref, w2k_ref).astype(BF16)
    vct_ref[0] = compress(av_ref, posv_ref, w1v_ref, w2v_ref).T[:HEAD_DIM, :].astype(BF16)


def _nsa_compress(ak, av, posk, posv, w1k, w1v, w2k, w2v):
    bg, rows, feat = ak.shape
    blk = pl.BlockSpec((1, rows, feat), lambda i: (i, 0, 0))
    return pl.pallas_call(
        _nsa_compress_kernel,
        grid=(bg,),
        in_specs=[blk, blk, _const_spec(posk.shape), _const_spec(posv.shape),
                  _const_spec(w1k.shape), _const_spec(w1v.shape),
                  _const_spec(w2k.shape), _const_spec(w2v.shape)],
        out_specs=[pl.BlockSpec((1, rows, LANES), lambda i: (i, 0, 0)),
                   pl.BlockSpec((1, HEAD_DIM, rows), lambda i: (i, 0, 0))],
        out_shape=[jax.ShapeDtypeStruct((bg, rows, LANES), BF16),
                   jax.ShapeDtypeStruct((bg, HEAD_DIM, rows), BF16)],
        compiler_params=pltpu.CompilerParams(dimension_semantics=("arbitrary",),
                                             vmem_limit_bytes=VMEM_LIMIT),
        name="nsa_compress",
    )(ak, av, posk, posv, w1k, w1v, w2k, w2v)


def _nsa_kernel(q_ref, kcmp_ref, vcmpt_ref, ks_ref, kw_ref, vsw_ref, gate_ref, ovl_ref, o_ref,
                vst_ref, vwt_ref, gt_ref, ssel_ref, swin_ref, *, n_tiles, n_sel):
    grp = pl.program_id(1)
    hg = NSA_HEADS_PER_GROUP
    t = Q_TILE
    wq = hg * t
    sub = KEY_TILE // NSA_SLC_BLOCK
    back = NSA_WINDOW // KEY_TILE
    n_top = min(NSA_SLC_TOPN, n_sel)

    for n in range(n_tiles):
        rows = slice(n * KEY_TILE, (n + 1) * KEY_TILE)
        blk = vsw_ref[rows, :].astype(F32).T.astype(BF16)
        vst_ref[n] = blk[:HEAD_DIM, :]
        vwt_ref[n] = blk[HEAD_DIM:, :]
        gt_ref[:, rows] = jax.nn.sigmoid(gate_ref[rows, :]).T
    n_pairs = hg // 2
    wp = 2 * t
    kloc = lax.broadcasted_iota(jnp.int32, (KEY_TILE, wp), 0)
    tloc = lax.broadcasted_iota(jnp.int32, (KEY_TILE, wp), 1) % t
    causal = kloc <= tloc
    window_tail = kloc > tloc

    def begin(i):
        rows = slice(i * t, (i + 1) * t)
        qcat = jnp.concatenate([_pair_queries(q_ref[rows, :LANES]),
                                _pair_queries(q_ref[rows, LANES:])], axis=0)
        tq = i * t + lax.broadcasted_iota(jnp.int32, (1, wq), 1) % t

        s = _nt(kcmp_ref[0], qcat)
        cmp_end = lax.broadcasted_iota(jnp.int32, s.shape, 0) * NSA_CMP_STRIDE + (NSA_CMP_BLOCK - 1)
        valid = cmp_end <= tq
        m = jnp.max(jnp.where(valid, s, NEG_INF), axis=0, keepdims=True)
        e = jnp.where(valid, jnp.exp2(s - m), 0.0)
        l = jnp.sum(e, axis=0, keepdims=True)
        p_cmp = e * jnp.where(l > 0.0, 1.0 / l, 0.0)
        o_cmp = _nn(vcmpt_ref[0], p_cmp.astype(BF16))

        bias = None
        if sub * (i + 1) > n_top:
            p_grp = p_cmp[:, :t]
            for h in range(1, hg):
                p_grp = p_grp + p_cmp[:, h * t:(h + 1) * t]
            p_hi, p_lo = _split_bf16(p_grp)
            imp = _nn(ovl_ref[...], p_hi) + _nn(ovl_ref[...], p_lo)
            j = lax.broadcasted_iota(jnp.int32, imp.shape, 0)
            q_blk = (i * t + lax.broadcasted_iota(jnp.int32, imp.shape, 1)) // NSA_SLC_BLOCK
            forced = jnp.logical_or(j == 0, jnp.logical_or(j == q_blk, j == q_blk - 1))
            score = jnp.where(j <= q_blk, jnp.where(forced, POS_BIG, imp), NEG_INF)
            sel = jnp.zeros(imp.shape, F32)
            for _ in range(n_top):
                mx = jnp.max(score, axis=0, keepdims=True)
                first = jnp.min(jnp.where(score == mx, j, n_sel), axis=0, keepdims=True)
                hit = j == first
                sel = jnp.where(jnp.logical_and(hit, mx > 0.5 * NEG_INF), 1.0, sel)
                score = jnp.where(hit, -jnp.inf, score)
            bias = jnp.concatenate([jnp.where(sel > 0.5, 0.0, NEG_INF)] * hg, axis=1)
        def fresh():
            return [dict(m8=None, l8=jnp.zeros((SUBLANES, wp), F32), acc=jnp.zeros((HEAD_DIM, wp), F32))
                    for _ in range(n_pairs)]
        return dict(i=i, qcat=qcat, o_cmp=o_cmp, bias=bias, sel=fresh(), win=fresh())

    def note_max(br, s):
        bm = _fold_rows(s, jnp.max)
        br["m8"] = bm if br["m8"] is None else jnp.maximum(br["m8"], bm)

    def pair_lanes(pr):
        return slice(pr * wp, (pr + 1) * wp)

    def sel_scores(st, n, pr):
        s = _nt(ks_ref[n * KEY_TILE:(n + 1) * KEY_TILE, :], st["qcat"][pair_lanes(pr), :])
        if st["bias"] is not None:
            s = jnp.concatenate(
                [s[jj * NSA_SLC_BLOCK:(jj + 1) * NSA_SLC_BLOCK, :]
                 + st["bias"][n * sub + jj:n * sub + jj + 1, pair_lanes(pr)] for jj in range(sub)], axis=0)
        if n == st["i"]:
            s = jnp.where(causal, s, NEG_INF)
        ssel_ref[n, :, pair_lanes(pr)] = s
        note_max(st["sel"][pr], s)

    def win_scores(st, n, pr):
        s = _nt(kw_ref[n * KEY_TILE:(n + 1) * KEY_TILE, :], st["qcat"][pair_lanes(pr), :])
        if n == st["i"]:
            s = jnp.where(causal, s, NEG_INF)
        elif n == st["i"] - back:
            s = jnp.where(window_tail, s, NEG_INF)
        swin_ref[st["i"] - n, :, pair_lanes(pr)] = s
        note_max(st["win"][pr], s)

    def weights(br, s_view, pr, v_view):
        p = jnp.exp2(s_view[:, pair_lanes(pr)] - br["m"])
        br["l8"] = br["l8"] + _fold_rows(p, jnp.sum)
        br["acc"] = br["acc"] + _nn(v_view[...], p.astype(BF16))

    def finish(st):
        i = st["i"]
        gates = gt_ref[pl.ds(pl.multiple_of(grp * 16, 16), 16), i * t:(i + 1) * t]
        for pr in range(n_pairs):
            o_slc, o_win = [br[pr]["acc"] * (1.0 / jnp.sum(br[pr]["l8"], axis=0, keepdims=True))
                            for br in (st["sel"], st["win"])]
            heads = []
            for hp in range(2):
                h = 2 * pr + hp
                cols = slice(hp * t, (hp + 1) * t)
                heads.append(gates[3 * h:3 * h + 1, :] * st["o_cmp"][:, h * t:(h + 1) * t]
                             + gates[3 * h + 1:3 * h + 2, :] * o_slc[:, cols]
                             + gates[3 * h + 2:3 * h + 3, :] * o_win[:, cols])
            both = jnp.concatenate(heads, axis=0)
            o_ref[i * t:(i + 1) * t, pr * LANES:(pr + 1) * LANES] = both.T.astype(o_ref.dtype)

    def interleave(first, second):
        for k in range(max(len(first), len(second))):
            for steps in (first, second):
                if k < len(steps):
                    steps[k]()

    def win_tiles(i):
        return [n for n in range(i, i - back - 1, -1) if n >= 0]

    def set_max(branches):
        for br in branches:
            br["m"] = jnp.max(br["m8"], axis=0, keepdims=True)

    pairs = range(n_pairs)
    cur = begin(0)
    for pr in pairs:
        sel_scores(cur, 0, pr)
    for i in range(n_tiles):
        set_max(cur["sel"])
        interleave([functools.partial(win_scores, cur, n, pr) for n in win_tiles(i) for pr in pairs],
                   [functools.partial(weights, cur["sel"][pr], ssel_ref.at[n], pr, vst_ref.at[n])
                    for n in range(i + 1) for pr in pairs])
        set_max(cur["win"])
        nxt = begin(i + 1) if i + 1 < n_tiles else None
        interleave([functools.partial(sel_scores, nxt, n, pr) for n in range(i + 2) for pr in pairs]
                   if nxt else [],
                   [functools.partial(weights, cur["win"][pr], swin_ref.at[i - n], pr, vwt_ref.at[n])
                    for n in win_tiles(i) for pr in pairs])
        finish(cur)
        cur = nxt


def _nsa_attention(proj, gates, kcmp, vcmpt, overlap_t, batch, seq):
    n = proj.shape[0]
    n_tiles = seq // KEY_TILE
    n_sel = seq // NSA_SLC_BLOCK
    g = NSA_KV_GROUPS
    hg = NSA_HEADS_PER_GROUP
    n_cmp_pad = kcmp.shape[1]
    back = NSA_WINDOW // KEY_TILE
    ks0, kw0, vsw0 = 12, 16, 20
    return pl.pallas_call(
        functools.partial(_nsa_kernel, n_tiles=n_tiles, n_sel=n_sel),
        grid=(batch, g),
        in_specs=[pl.BlockSpec((seq, hg * HEAD_DIM), lambda b, k: (b, k)),
                  pl.BlockSpec((1, n_cmp_pad, LANES), lambda b, k: (b * g + k, 0, 0)),
                  pl.BlockSpec((1, HEAD_DIM, n_cmp_pad), lambda b, k: (b * g + k, 0, 0)),
                  pl.BlockSpec((seq, LANES), lambda b, k: (b, ks0 + k)),
                  pl.BlockSpec((seq, LANES), lambda b, k: (b, kw0 + k)),
                  pl.BlockSpec((seq, LANES), lambda b, k: (b, vsw0 + k)),
                  pl.BlockSpec((seq, LANES), lambda b, k: (b, 0)),
                  _const_spec(overlap_t.shape)],
        out_specs=pl.BlockSpec((seq, hg * HEAD_DIM), lambda b, k: (b, k)),
        out_shape=jax.ShapeDtypeStruct((n, N_HEADS * HEAD_DIM), BF16),
        scratch_shapes=[pltpu.VMEM((n_tiles, HEAD_DIM, KEY_TILE), BF16),
                        pltpu.VMEM((n_tiles, HEAD_DIM, KEY_TILE), BF16),
                        pltpu.VMEM((LANES, seq), F32),
                        pltpu.VMEM((n_tiles, KEY_TILE, hg * Q_TILE), F32),
                        pltpu.VMEM((back + 1, KEY_TILE, hg * Q_TILE), F32)],
        compiler_params=pltpu.CompilerParams(
            dimension_semantics=("arbitrary", "arbitrary"),
            vmem_limit_bytes=VMEM_LIMIT),
        name="nsa_attention",
    )(proj, kcmp, vcmpt, proj, proj, proj, gates, overlap_t)


def _rope_tables(seq):
    inv_freq = 1.0 / (ROPE_THETA ** (jnp.arange(0, HEAD_DIM, 2, dtype=F32) / HEAD_DIM))
    ang = jnp.arange(seq, dtype=F32)[:, None] * inv_freq[None, :]
    reps = LANES // (HEAD_DIM // 2)
    cos = jnp.tile(jnp.cos(ang), (1, reps))
    sin = jnp.tile(jnp.sin(ang), (1, reps))
    first_half = (jnp.arange(LANES) % HEAD_DIM) < HEAD_DIM // 2
    return cos, jnp.where(first_half, -sin, 0.0), jnp.where(first_half, 0.0, sin)


def _nsa_weight_layout(w_in):
    qd = N_HEADS * HEAD_DIM
    kv = NSA_KV_COLS
    q, kc, vc, ks, vs, kw, vw, gt = jnp.split(
        w_in, np.cumsum([qd] + [kv] * 6).tolist(), axis=1)

    def dup(w):
        w = w.reshape(D_MODEL, NSA_KV_GROUPS, 1, HEAD_DIM)
        return jnp.broadcast_to(w, (D_MODEL, NSA_KV_GROUPS, 2, HEAD_DIM)).reshape(D_MODEL, -1)

    vsw = jnp.concatenate([vs.reshape(D_MODEL, NSA_KV_GROUPS, 1, HEAD_DIM),
                           vw.reshape(D_MODEL, NSA_KV_GROUPS, 1, HEAD_DIM)], axis=2)
    main = jnp.concatenate([q, kc, vc, dup(ks), dup(kw), vsw.reshape(D_MODEL, -1)], axis=1)
    per_group = 3 * NSA_HEADS_PER_GROUP
    gt = gt.reshape(D_MODEL, NSA_KV_GROUPS, per_group)
    gt = jnp.pad(gt, ((0, 0), (0, 0), (0, 16 - per_group))).reshape(D_MODEL, -1)
    gt = jnp.pad(gt, ((0, 0), (0, LANES - gt.shape[1])))
    return main.astype(BF16), gt.astype(BF16)


def _overlap_t(seq):
    n_cmp = (seq - NSA_CMP_BLOCK) // NSA_CMP_STRIDE + 1
    n_sel = seq // NSA_SLC_BLOCK
    c_start = np.arange(n_cmp) * NSA_CMP_STRIDE
    s_start = np.arange(n_sel) * NSA_SLC_BLOCK
    lo = np.maximum(c_start[:, None], s_start[None, :])
    hi = np.minimum(c_start[:, None] + NSA_CMP_BLOCK, s_start[None, :] + NSA_SLC_BLOCK)
    ov = np.clip(hi - lo, 0, None) / NSA_CMP_BLOCK
    ov = np.pad(ov, ((0, seq // NSA_CMP_STRIDE - n_cmp), (0, 0)))
    return jnp.asarray(ov.T, dtype=BF16)


_QK_ROPE = (True,) * 16 + (False,) * 8
_NO_ROPE = (False,) * 24
_NSA_ROPE = (True,) * 10 + (False,) * 2 + (True,) * 8 + (False,) * 4


def kernel(x, l0_ln_mix_pre, l0_w_in, l0_w_out, l0_ln_mix_post, l0_ln_mlp_pre, l0_w_up, l0_w_down, l0_ln_mlp_post, l1_ln_mix_pre, l1_w_in, l1_w_out, l1_ln_mix_post, l1_ln_mlp_pre, l1_w_up, l1_w_down, l1_ln_mlp_post, l2_ln_mix_pre, l2_w_in, l2_cmp_pos_k, l2_cmp_w1_k, l2_cmp_w2_k, l2_cmp_pos_v, l2_cmp_w1_v, l2_cmp_w2_v, l2_w_out, l2_ln_mix_post, l2_ln_mlp_pre, l2_w_up, l2_w_down, l2_ln_mlp_post, l3_ln_mix_pre, l3_w_in, l3_w_out, l3_ln_mix_post, l3_ln_mlp_pre, l3_w_up, l3_w_down, l3_ln_mlp_post):
    batch, seq, _ = x.shape
    assert seq % ROW_TILE == 0 and seq % KEY_TILE == 0 and Q_TILE == KEY_TILE == MOBA_BLOCK
    tables = _rope_tables(seq)
    x2 = x.reshape(batch * seq, D_MODEL)
    bf = lambda w: w.astype(BF16)

    def tail(o, x2, w_out, g_post, g_pre, w_up, w_down, g_mlp):
        return _layer_tail(o, x2, bf(w_out), g_post, g_pre, bf(w_up), bf(w_down), g_mlp)

    qkv = _norm_proj(x2, l0_ln_mix_pre, bf(l0_w_in), tables, _QK_ROPE, seq)
    x2 = tail(_moba_attention(qkv, batch, seq), x2, l0_w_out, l0_ln_mix_post, l0_ln_mlp_pre,
              l0_w_up, l0_w_down, l0_ln_mlp_post)

    qkv = _norm_proj(x2, l1_ln_mix_pre, bf(l1_w_in), tables, _NO_ROPE, seq)
    x2 = tail(_sb_attention(qkv, batch, seq), x2, l1_w_out, l1_ln_mix_post, l1_ln_mlp_pre,
              l1_w_up, l1_w_down, l1_ln_mlp_post)

    w_main, w_gate = _nsa_weight_layout(l2_w_in)
    proj, gates = _norm_proj(x2, l2_ln_mix_pre, w_main, tables, _NSA_ROPE, seq, wg=w_gate)
    qd = N_HEADS * HEAD_DIM
    rows = seq // NSA_CMP_STRIDE

    def row_groups(cols):
        a = cols.reshape(batch, seq, NSA_KV_GROUPS, HEAD_DIM).transpose(0, 2, 1, 3)
        return a.reshape(batch * NSA_KV_GROUPS, rows, NSA_CMP_STRIDE * HEAD_DIM)

    flat_pos = lambda p: p.reshape(2, NSA_CMP_STRIDE * HEAD_DIM)
    w2_dup = lambda w: bf(jnp.concatenate([w, w], axis=1))
    kcmp, vcmpt = _nsa_compress(
        row_groups(proj[:, qd:qd + NSA_KV_COLS]), row_groups(proj[:, qd + NSA_KV_COLS:qd + 2 * NSA_KV_COLS]),
        flat_pos(l2_cmp_pos_k), flat_pos(l2_cmp_pos_v), bf(l2_cmp_w1_k), bf(l2_cmp_w1_v),
        w2_dup(l2_cmp_w2_k), w2_dup(l2_cmp_w2_v))
    o = _nsa_attention(proj, gates, kcmp, vcmpt, _overlap_t(seq), batch, seq)
    x2 = tail(o, x2, l2_w_out, l2_ln_mix_post, l2_ln_mlp_pre, l2_w_up, l2_w_down, l2_ln_mlp_post)

    qkv = _norm_proj(x2, l3_ln_mix_pre, bf(l3_w_in), tables, _QK_ROPE, seq)
    x2 = tail(_moba_attention(qkv, batch, seq), x2, l3_w_out, l3_ln_mix_post, l3_ln_mlp_pre,
              l3_w_up, l3_w_down, l3_ln_mlp_post)
    return x2.reshape(batch, seq, D_MODEL)
```

```python
import functools
import math

import numpy as np
import jax
import jax.numpy as jnp
from jax import lax
from jax.experimental import pallas as pl
from jax.experimental.pallas import tpu as pltpu

D_MODEL = 1024
N_HEADS = 16
HEAD_DIM = 64
D_FF = 4 * D_MODEL
ROPE_THETA = 10000.0
NORM_EPS = 1e-6
NEG_INF = -1e30
POS_BIG = 1e30

MOBA_BLOCK = 256
MOBA_TOPK = 3

NSA_KV_GROUPS = 4
NSA_HEADS_PER_GROUP = N_HEADS // NSA_KV_GROUPS
NSA_CMP_BLOCK = 32
NSA_CMP_STRIDE = 16
NSA_CMP_HIDDEN = 256
NSA_SLC_BLOCK = 64
NSA_SLC_TOPN = 16
NSA_WINDOW = 512
NSA_KV_COLS = NSA_KV_GROUPS * HEAD_DIM

LANES = 128
SUBLANES = 8
KEY_TILE = 256
ONES_ROWS = 16
Q_TILE = 256
ROW_TILE = 512
FF_CHUNK = 512
PROJ_COLS = 3 * N_HEADS * HEAD_DIM
VMEM_LIMIT = 56 * 1024 * 1024
QUERY_SCALE = HEAD_DIM ** -0.5 * math.log2(math.e)
SB_UNDERFLOW_LOG2 = -150.0

F32 = jnp.float32
BF16 = jnp.bfloat16


def _nt(a, b):
    return lax.dot_general(a, b, (((1,), (1,)), ((), ())), preferred_element_type=F32)


def _nn(a, b):
    return jnp.dot(a, b, preferred_element_type=F32)


def _split_bf16(x):
    hi = x.astype(BF16)
    lo = (x - hi.astype(F32)).astype(BF16)
    return hi, lo


def _rms(x, g):
    ms = jnp.mean(x * x, axis=-1, keepdims=True)
    return x * lax.rsqrt(ms + NORM_EPS) * g


def _const_spec(shape):
    nd = len(shape)
    return pl.BlockSpec(shape, lambda *_: (0,) * nd, pipeline_mode=pl.Buffered(1))


def _proj_kernel(x_ref, g_ref, w_ref, cos_ref, sina_ref, sinb_ref, *rest, rope_slabs, has_gate):
    if has_gate:
        wg_ref, o_ref, og_ref = rest
    else:
        (o_ref,) = rest
    h = _rms(x_ref[...], g_ref[...]).astype(BF16)
    cos = cos_ref[...]
    sina = sina_ref[...]
    sinb = sinb_ref[...]
    n_slabs = len(rope_slabs)
    for c in range(n_slabs // 2):
        y = _nn(h, w_ref[:, c * 2 * LANES:(c + 1) * 2 * LANES])
        for half in range(2):
            s = 2 * c + half
            ys = y[:, half * LANES:(half + 1) * LANES]
            if s < N_HEADS * HEAD_DIM // LANES:
                ys = ys * QUERY_SCALE
            if rope_slabs[s]:
                ys = (ys * cos + pltpu.roll(ys, LANES - HEAD_DIM // 2, 1) * sina
                      + pltpu.roll(ys, HEAD_DIM // 2, 1) * sinb)
            o_ref[:, s * LANES:(s + 1) * LANES] = ys.astype(o_ref.dtype)
    if has_gate:
        og_ref[...] = _nn(h, wg_ref[...])


def _norm_proj(x2, g, w, tables, rope_slabs, seq, wg=None):
    n = x2.shape[0]
    cols = w.shape[1]
    pos_blocks = seq // ROW_TILE
    has_gate = wg is not None
    row = lambda i: (i, 0)
    tab = pl.BlockSpec((ROW_TILE, LANES), lambda i: (i % pos_blocks, 0))
    in_specs = [pl.BlockSpec((ROW_TILE, D_MODEL), row), _const_spec((1, D_MODEL)),
                _const_spec((D_MODEL, cols)), tab, tab, tab]
    args = [x2, g.reshape(1, D_MODEL), w, *tables]
    out_shape = [jax.ShapeDtypeStruct((n, cols), BF16)]
    out_specs = [pl.BlockSpec((ROW_TILE, cols), row)]
    if has_gate:
        in_specs.append(_const_spec((D_MODEL, LANES)))
        args.append(wg)
        out_shape.append(jax.ShapeDtypeStruct((n, LANES), F32))
        out_specs.append(pl.BlockSpec((ROW_TILE, LANES), row))
    outs = pl.pallas_call(
        functools.partial(_proj_kernel, rope_slabs=rope_slabs, has_gate=has_gate),
        grid=(n // ROW_TILE,),
        in_specs=in_specs,
        out_specs=out_specs,
        out_shape=out_shape,
        compiler_params=pltpu.CompilerParams(dimension_semantics=("arbitrary",),
                                             vmem_limit_bytes=VMEM_LIMIT),
        name="norm_proj",
    )(*args)
    return outs if has_gate else outs[0]


def _tail_kernel(o_ref, x_ref, wout_ref, gpost_ref, gpre_ref, wup_ref, wdown_ref, gmlp_ref, out_ref):
    m = _nn(o_ref[...], wout_ref[...])
    x1 = x_ref[...] + _rms(m, gpost_ref[...])
    h = _rms(x1, gpre_ref[...]).astype(BF16)
    acc = jnp.zeros((ROW_TILE, D_MODEL), F32)
    for j in range(D_FF // FF_CHUNK):
        u = _nn(h, wup_ref[:, j * FF_CHUNK:(j + 1) * FF_CHUNK])
        a = jnp.square(jnp.maximum(u, 0.0)).astype(BF16)
        acc = acc + _nn(a, wdown_ref[j * FF_CHUNK:(j + 1) * FF_CHUNK, :])
    out_ref[...] = x1 + _rms(acc, gmlp_ref[...])


def _layer_tail(o2, x2, wout, gpost, gpre, wup, wdown, gmlp):
    n = x2.shape[0]
    row = lambda i: (i, 0)
    vec = lambda v: v.reshape(1, D_MODEL)
    return pl.pallas_call(
        _tail_kernel,
        grid=(n // ROW_TILE,),
        in_specs=[pl.BlockSpec((ROW_TILE, N_HEADS * HEAD_DIM), row),
                  pl.BlockSpec((ROW_TILE, D_MODEL), row),
                  _const_spec((N_HEADS * HEAD_DIM, D_MODEL)), _const_spec((1, D_MODEL)),
                  _const_spec((1, D_MODEL)), _const_spec((D_MODEL, D_FF)),
                  _const_spec((D_FF, D_MODEL)), _const_spec((1, D_MODEL))],
        out_specs=pl.BlockSpec((ROW_TILE, D_MODEL), row),
        out_shape=jax.ShapeDtypeStruct((n, D_MODEL), F32),
        compiler_params=pltpu.CompilerParams(dimension_semantics=("arbitrary",),
                                             vmem_limit_bytes=VMEM_LIMIT),
        name="layer_tail",
    )(o2, x2, wout, vec(gpost), vec(gpre), wup, wdown, vec(gmlp))


def _pair_queries(q):
    lane = lax.broadcasted_iota(jnp.int32, q.shape, 1)
    zero = jnp.zeros_like(q)
    return jnp.concatenate([jnp.where(lane < HEAD_DIM, q, zero),
                            jnp.where(lane >= HEAD_DIM, q, zero)], axis=0)


def _transpose_value_tiles(v_ref, vt_ref, n_tiles):
    for n in range(n_tiles):
        blk = v_ref[n * KEY_TILE:(n + 1) * KEY_TILE, :].astype(F32)
        vt_ref[n] = blk.T.astype(BF16)


def _fold_rows(x, op):
    r, w = x.shape
    return op(x.reshape(r // SUBLANES, SUBLANES, w), axis=0)


def _pair_pv(vt, p):
    t = p.shape[1] // 2
    return _nn(vt[:HEAD_DIM, :], p[:, :t]), _nn(vt[HEAD_DIM:, :], p[:, t:])


def _pair_output(acc_a, acc_b, inv_l=None):
    if inv_l is not None:
        t = acc_a.shape[1]
        acc_a = acc_a * inv_l[:, :t]
        acc_b = acc_b * inv_l[:, t:]
    return jnp.concatenate([acc_a, acc_b], axis=0).T


def _moba_kernel(q_ref, k_ref, v_ref, o_ref, kmean_ref, vt_ref, s_ref, *, n_blocks):
    for n in range(n_blocks):
        blk = k_ref[n * MOBA_BLOCK:(n + 1) * MOBA_BLOCK, :].astype(F32)
        kmean_ref[n:n + 1, :] = jnp.mean(blk, axis=0, keepdims=True)
    _transpose_value_tiles(v_ref, vt_ref, n_blocks)
    km_hi, km_lo = _split_bf16(kmean_ref[...])
    kpos = lax.broadcasted_iota(jnp.int32, (KEY_TILE, Q_TILE), 0)
    tpos = lax.broadcasted_iota(jnp.int32, (KEY_TILE, Q_TILE), 1)
    causal = kpos <= tpos

    def begin(i):
        qcat = _pair_queries(q_ref[i * Q_TILE:(i + 1) * Q_TILE, :])
        chosen = [None] * i
        if i > MOBA_TOPK:
            gate = _nt(km_hi, qcat) + _nt(km_lo, qcat)
            blk_id = lax.broadcasted_iota(jnp.int32, gate.shape, 0)
            g = jnp.where(blk_id < i, gate, NEG_INF)
            sel = jnp.zeros(gate.shape, F32)
            for _ in range(MOBA_TOPK):
                mx = jnp.max(g, axis=0, keepdims=True)
                first = jnp.min(jnp.where(g == mx, blk_id, n_blocks), axis=0, keepdims=True)
                hit = blk_id == first
                sel = jnp.where(hit, 1.0, sel)
                g = jnp.where(hit, -jnp.inf, g)
            chosen = [sel[n:n + 1, :] > 0.5 for n in range(i)]
        heads = [dict(m8=None, l8=jnp.zeros((SUBLANES, Q_TILE), F32), acc=jnp.zeros((HEAD_DIM, Q_TILE), F32))
                 for _ in range(2)]
        return dict(i=i, qcat=qcat, chosen=chosen, heads=heads)

    def score_tile(st, n, h):
        i = st["i"]
        hd = st["heads"][h]
        lanes = slice(h * Q_TILE, (h + 1) * Q_TILE)
        s = _nt(k_ref[n * KEY_TILE:(n + 1) * KEY_TILE, :], st["qcat"][lanes, :])
        if n == i:
            s = jnp.where(causal, s, NEG_INF)
        s_ref[i % 2, n, :, lanes] = s
        bm = _fold_rows(s, jnp.max)
        if n < i and st["chosen"][n] is not None:
            bm = jnp.where(st["chosen"][n][:, lanes], bm, NEG_INF)
        hd["m8"] = bm if hd["m8"] is None else jnp.maximum(hd["m8"], bm)

    def weight_tile(st, n, h):
        i = st["i"]
        hd = st["heads"][h]
        lanes = slice(h * Q_TILE, (h + 1) * Q_TILE)
        keep = None if (n == i or st["chosen"][n] is None) else st["chosen"][n][:, lanes]
        shift = hd["m"] if keep is None else jnp.where(keep, hd["m"], POS_BIG)
        p = jnp.exp2(s_ref[i % 2, n, :, lanes] - shift)
        hd["l8"] = hd["l8"] + _fold_rows(p, jnp.sum)
        hd["acc"] = hd["acc"] + _nn(vt_ref[n, h * HEAD_DIM:(h + 1) * HEAD_DIM, :], p.astype(BF16))

    cur = begin(0)
    for h in range(2):
        score_tile(cur, 0, h)
    for i in range(n_blocks):
        for hd in cur["heads"]:
            hd["m"] = jnp.max(hd["m8"], axis=0, keepdims=True)
        nxt = begin(i + 1) if i + 1 < n_blocks else None
        for n in range(i + 2):
            for h in range(2):
                if nxt is not None:
                    score_tile(nxt, n, h)
                if n <= i:
                    weight_tile(cur, n, h)
        acc = [hd["acc"] * (1.0 / jnp.sum(hd["l8"], axis=0, keepdims=True)) for hd in cur["heads"]]
        o_ref[i * Q_TILE:(i + 1) * Q_TILE, :] = _pair_output(acc[0], acc[1]).astype(o_ref.dtype)
        cur = nxt


def _moba_attention(qkv, batch, seq):
    n = qkv.shape[0]
    n_blocks = seq // MOBA_BLOCK
    pairs = N_HEADS // 2
    return pl.pallas_call(
        functools.partial(_moba_kernel, n_blocks=n_blocks),
        grid=(batch, pairs),
        in_specs=[pl.BlockSpec((seq, LANES), lambda b, p: (b, p)),
                  pl.BlockSpec((seq, LANES), lambda b, p: (b, pairs + p)),
                  pl.BlockSpec((seq, LANES), lambda b, p: (b, 2 * pairs + p))],
        out_specs=pl.BlockSpec((seq, LANES), lambda b, p: (b, p)),
        out_shape=jax.ShapeDtypeStruct((n, N_HEADS * HEAD_DIM), BF16),
        scratch_shapes=[pltpu.VMEM((n_blocks, LANES), F32),
                        pltpu.VMEM((n_blocks, LANES, KEY_TILE), BF16),
                        pltpu.VMEM((2, n_blocks, KEY_TILE, 2 * Q_TILE), F32)],
        compiler_params=pltpu.CompilerParams(
            dimension_semantics=("arbitrary", "arbitrary"),
            vmem_limit_bytes=VMEM_LIMIT),
        name="moba_attention",
    )(qkv, qkv, qkv)


def _log2_sigmoid_pair(z):
    log_beta = jnp.minimum(z, 0.0) - jnp.log2(1.0 + jnp.exp2(-jnp.abs(z)))
    return log_beta, log_beta - z


def _sb_kernel(q_ref, k_ref, v_ref, o_ref, vt_ref, acc_ref, carry_ref, *, n_tiles):
    w2 = 2 * Q_TILE
    near = 3
    _transpose_value_tiles(v_ref, vt_ref, n_tiles)
    r = lax.broadcasted_iota(jnp.int32, (KEY_TILE, KEY_TILE), 0)
    c = lax.broadcasted_iota(jnp.int32, (KEY_TILE, KEY_TILE), 1)
    later = jnp.where(c > r, 1.0, 0.0).astype(BF16)
    kpos = lax.broadcasted_iota(jnp.int32, (KEY_TILE, w2), 0)
    tpos = lax.broadcasted_iota(jnp.int32, (KEY_TILE, w2), 1) % Q_TILE
    strict = kpos < tpos

    def scores(state, t):
        st = state[t["i"]]
        t["z"] = _nt(k_ref[t["n"] * KEY_TILE:(t["n"] + 1) * KEY_TILE, :], st["qcat"])

    def log_terms(state, t):
        st = state[t["i"]]
        log_beta, log_1m = _log2_sigmoid_pair(t.pop("z"))
        if t["n"] == t["i"]:
            log_1m = jnp.where(strict, log_1m, 0.0)
        t["carry"] = st["carry"]
        if t["n"] > 0:
            tile_sum = jnp.sum(log_1m, axis=0, keepdims=True)
            st["carry"] = tile_sum if st["carry"] is None else st["carry"] + tile_sum
        t["log_beta"] = log_beta
        t["suffix"] = _nn(later, log_1m.astype(BF16))

    def weights(state, t):
        st = state[t["i"]]
        expo = t.pop("log_beta") + t.pop("suffix")
        if t["carry"] is not None:
            expo = expo + t["carry"]
        wgt = jnp.exp2(expo)
        if t["n"] == t["i"]:
            wgt = jnp.where(strict, wgt, 0.0)
        pv_a, pv_b = _pair_pv(vt_ref[t["n"]], wgt.astype(BF16))
        st["acc_a"] = st["acc_a"] + pv_a
        st["acc_b"] = st["acc_b"] + pv_b

    def run(state, tiles):
        stages = (scores, log_terms, weights)
        for step in range(len(tiles) + len(stages) - 1):
            for lag, stage in enumerate(stages):
                if 0 <= step - lag < len(tiles):
                    stage(state, tiles[step - lag])

    def query_operand(i):
        return _pair_queries(q_ref[i * Q_TILE:(i + 1) * Q_TILE, :])

    def write_out(i, st):
        o_ref[i * Q_TILE:(i + 1) * Q_TILE, :] = _pair_output(st["acc_a"], st["acc_b"]).astype(o_ref.dtype)

    zeros = jnp.zeros((HEAD_DIM, Q_TILE), F32)
    state = {i: dict(qcat=query_operand(i), carry=None, acc_a=zeros, acc_b=zeros) for i in range(n_tiles)}
    run(state, [dict(i=i, n=n) for i in range(n_tiles) for n in range(i, max(i - near, -1), -1)])
    for i in range(n_tiles):
        write_out(i, state[i])
        if i >= near:
            acc_ref[i] = jnp.concatenate([state[i]["acc_a"], state[i]["acc_b"]], axis=0)
            carry_ref[i] = state[i]["carry"]

    live = {i: jnp.max(state[i]["carry"]) > SB_UNDERFLOW_LOG2 for i in range(near, n_tiles)}
    for i in range(near, n_tiles):
        @pl.when(live[i])
        def _(i=i):
            acc = acc_ref[i]
            far = {i: dict(qcat=query_operand(i), carry=carry_ref[i],
                           acc_a=acc[:HEAD_DIM, :], acc_b=acc[HEAD_DIM:, :])}
            run(far, [dict(i=i, n=n) for n in range(i - near, -1, -1)])
            write_out(i, far[i])


def _sb_attention(qkv, batch, seq):
    n = qkv.shape[0]
    n_tiles = seq // KEY_TILE
    pairs = N_HEADS // 2
    return pl.pallas_call(
        functools.partial(_sb_kernel, n_tiles=n_tiles),
        grid=(batch, pairs),
        in_specs=[pl.BlockSpec((seq, LANES), lambda b, p: (b, p)),
                  pl.BlockSpec((seq, LANES), lambda b, p: (b, pairs + p)),
                  pl.BlockSpec((seq, LANES), lambda b, p: (b, 2 * pairs + p))],
        out_specs=pl.BlockSpec((seq, LANES), lambda b, p: (b, p)),
        out_shape=jax.ShapeDtypeStruct((n, N_HEADS * HEAD_DIM), BF16),
        scratch_shapes=[pltpu.VMEM((n_tiles, LANES, KEY_TILE), BF16),
                        pltpu.VMEM((n_tiles, LANES, Q_TILE), F32),
                        pltpu.VMEM((n_tiles, 1, 2 * Q_TILE), F32)],
        compiler_params=pltpu.CompilerParams(
            dimension_semantics=("arbitrary", "arbitrary"),
            vmem_limit_bytes=VMEM_LIMIT),
        name="stick_breaking_attention",
    )(qkv, qkv, qkv)


def _nsa_compress_kernel(ak_ref, av_ref, posk_ref, posv_ref, w1k_ref, w1v_ref, w2k_ref, w2v_ref,
                         kc_ref, vct_ref):
    half = NSA_CMP_STRIDE * HEAD_DIM
    n_rows = ak_ref.shape[1]

    def compress(a_ref, pos_ref, w1_ref, w2_ref):
        a = a_ref[0].astype(F32)
        first = (a + pos_ref[0:1, :]).astype(BF16)
        second = (a + pos_ref[1:2, :]).astype(BF16)
        pre = _nn(first, w1_ref[:half, :]) + pltpu.roll(_nn(second, w1_ref[half:, :]), n_rows - 1, 0)
        return _nn(jax.nn.gelu(pre).astype(BF16), w2_ref[...])

    kc_ref[0] = compress(ak_ref, posk_ref, w1k_ref, w2k_ref).astype(BF16)
    vct_ref[0] = compress(av_ref, posv_ref, w1v_ref, w2v_ref).T[:HEAD_DIM, :].astype(BF16)


def _nsa_compress(ak, av, posk, posv, w1k, w1v, w2k, w2v):
    bg, rows, feat = ak.shape
    blk = pl.BlockSpec((1, rows, feat), lambda i: (i, 0, 0))
    return pl.pallas_call(
        _nsa_compress_kernel,
        grid=(bg,),
        in_specs=[blk, blk, _const_spec(posk.shape), _const_spec(posv.shape),
                  _const_spec(w1k.shape), _const_spec(w1v.shape),
                  _const_spec(w2k.shape), _const_spec(w2v.shape)],
        out_specs=[pl.BlockSpec((1, rows, LANES), lambda i: (i, 0, 0)),
                   pl.BlockSpec((1, HEAD_DIM, rows), lambda i: (i, 0, 0))],
        out_shape=[jax.ShapeDtypeStruct((bg, rows, LANES), BF16),
                   jax.ShapeDtypeStruct((bg, HEAD_DIM, rows), BF16)],
        compiler_params=pltpu.CompilerParams(dimension_semantics=("arbitrary",),
                                             vmem_limit_bytes=VMEM_LIMIT),
        name="nsa_compress",
    )(ak, av, posk, posv, w1k, w1v, w2k, w2v)


def _nsa_kernel(q_ref, kcmp_ref, vcmpt_ref, ks_ref, kw_ref, vsw_ref, gate_ref, ovl_ref, o_ref,
                vst_ref, vwt_ref, gt_ref, ssel_ref, swin_ref, *, n_tiles, n_sel):
    grp = pl.program_id(1)
    hg = NSA_HEADS_PER_GROUP
    t = Q_TILE
    wq = hg * t
    sub = KEY_TILE // NSA_SLC_BLOCK
    back = NSA_WINDOW // KEY_TILE
    n_top = min(NSA_SLC_TOPN, n_sel)

    for n in range(n_tiles):
        rows = slice(n * KEY_TILE, (n + 1) * KEY_TILE)
        blk = vsw_ref[rows, :].astype(F32).T.astype(BF16)
        for vt_ref, part in ((vst_ref, blk[:HEAD_DIM, :]), (vwt_ref, blk[HEAD_DIM:, :])):
            vt_ref[n, :HEAD_DIM, :] = part
            vt_ref[n, HEAD_DIM:, :] = jnp.ones((ONES_ROWS, KEY_TILE), BF16)
        gt_ref[:, rows] = jax.nn.sigmoid(gate_ref[rows, :]).T
    kloc = lax.broadcasted_iota(jnp.int32, (KEY_TILE, wq), 0)
    tloc = lax.broadcasted_iota(jnp.int32, (KEY_TILE, wq), 1) % t
    causal = kloc <= tloc
    window_tail = kloc > tloc

    def begin(i):
        rows = slice(i * t, (i + 1) * t)
        qcat = jnp.concatenate([_pair_queries(q_ref[rows, :LANES]),
                                _pair_queries(q_ref[rows, LANES:])], axis=0)
        tq = i * t + lax.broadcasted_iota(jnp.int32, (1, wq), 1) % t

        n_cmp_pad = kcmp_ref.shape[1]
        reach = min(n_cmp_pad, (i + 1) * t // NSA_CMP_STRIDE)
        s = _nt(kcmp_ref[0, :reach, :], qcat)
        cmp_end = lax.broadcasted_iota(jnp.int32, s.shape, 0) * NSA_CMP_STRIDE + (NSA_CMP_BLOCK - 1)
        valid = cmp_end <= tq
        m = jnp.max(jnp.where(valid, s, NEG_INF), axis=0, keepdims=True)
        e = jnp.where(valid, jnp.exp2(s - m), 0.0)
        l = jnp.sum(e, axis=0, keepdims=True)
        p_cmp = e * jnp.where(l > 0.0, 1.0 / l, 0.0)
        if reach < n_cmp_pad:
            p_cmp = jnp.concatenate([p_cmp, jnp.zeros((n_cmp_pad - reach, wq), F32)], axis=0)
        o_cmp = _nn(vcmpt_ref[0], p_cmp.astype(BF16))

        bias = None
        if sub * (i + 1) > n_top:
            p_grp = p_cmp[:, :t]
            for h in range(1, hg):
                p_grp = p_grp + p_cmp[:, h * t:(h + 1) * t]
            p_hi, p_lo = _split_bf16(p_grp)
            imp = _nn(ovl_ref[...], p_hi) + _nn(ovl_ref[...], p_lo)
            j = lax.broadcasted_iota(jnp.int32, imp.shape, 0)
            q_blk = (i * t + lax.broadcasted_iota(jnp.int32, imp.shape, 1)) // NSA_SLC_BLOCK
            forced = jnp.logical_or(j == 0, jnp.logical_or(j == q_blk, j == q_blk - 1))
            score = jnp.where(j <= q_blk, jnp.where(forced, POS_BIG, imp), NEG_INF)
            sel = jnp.zeros(imp.shape, F32)
            for _ in range(n_top):
                mx = jnp.max(score, axis=0, keepdims=True)
                first = jnp.min(jnp.where(score == mx, j, n_sel), axis=0, keepdims=True)
                hit = j == first
                sel = jnp.where(jnp.logical_and(hit, mx > 0.5 * NEG_INF), 1.0, sel)
                score = jnp.where(hit, -jnp.inf, score)
            bias = jnp.concatenate([jnp.where(sel > 0.5, 0.0, NEG_INF)] * hg, axis=1)
        zeros_acc = jnp.zeros((HEAD_DIM + ONES_ROWS, wq), F32)
        return dict(i=i, qcat=qcat, o_cmp=o_cmp, bias=bias,
                    sel=dict(m8=None, acc=zeros_acc), win=dict(m8=None, acc=zeros_acc))

    def note_max(br, s):
        bm = _fold_rows(s, jnp.max)
        br["m8"] = bm if br["m8"] is None else jnp.maximum(br["m8"], bm)

    def sel_scores(st, n):
        s = _nt(ks_ref[n * KEY_TILE:(n + 1) * KEY_TILE, :], st["qcat"])
        if st["bias"] is not None:
            s = jnp.concatenate(
                [s[jj * NSA_SLC_BLOCK:(jj + 1) * NSA_SLC_BLOCK, :]
                 + st["bias"][n * sub + jj:n * sub + jj + 1, :] for jj in range(sub)], axis=0)
        if n == st["i"]:
            s = jnp.where(causal, s, NEG_INF)
        ssel_ref[n] = s
        note_max(st["sel"], s)

    def win_scores(st, n):
        s = _nt(kw_ref[n * KEY_TILE:(n + 1) * KEY_TILE, :], st["qcat"])
        if n == st["i"]:
            s = jnp.where(causal, s, NEG_INF)
        elif n == st["i"] - back:
            s = jnp.where(window_tail, s, NEG_INF)
        swin_ref[st["i"] - n] = s
        note_max(st["win"], s)

    def weights(br, s_view, v_view):
        p = jnp.exp2(s_view[...] - br["m"])
        br["acc"] = br["acc"] + _nn(v_view[...], p.astype(BF16))

    def finish(st):
        i = st["i"]
        o_slc, o_win = [br["acc"][:HEAD_DIM, :] * (1.0 / br["acc"][HEAD_DIM:HEAD_DIM + 1, :])
                        for br in (st["sel"], st["win"])]
        gates = gt_ref[pl.ds(pl.multiple_of(grp * 16, 16), 16), i * t:(i + 1) * t]
        heads = []
        for h in range(hg):
            cols = slice(h * t, (h + 1) * t)
            heads.append(gates[3 * h:3 * h + 1, :] * st["o_cmp"][:, cols]
                         + gates[3 * h + 1:3 * h + 2, :] * o_slc[:, cols]
                         + gates[3 * h + 2:3 * h + 3, :] * o_win[:, cols])
        for pair in range(hg // 2):
            both = jnp.concatenate([heads[2 * pair], heads[2 * pair + 1]], axis=0)
            o_ref[i * t:(i + 1) * t, pair * LANES:(pair + 1) * LANES] = both.T.astype(o_ref.dtype)

    def interleave(first, second):
        for k in range(max(len(first), len(second))):
            for steps in (first, second):
                if k < len(steps):
                    steps[k]()

    def win_tiles(i):
        return [n for n in range(i, i - back - 1, -1) if n >= 0]

    cur = begin(0)
    sel_scores(cur, 0)
    for i in range(n_tiles):
        cur["sel"]["m"] = jnp.max(cur["sel"]["m8"], axis=0, keepdims=True)
        interleave([functools.partial(win_scores, cur, n) for n in win_tiles(i)],
                   [functools.partial(weights, cur["sel"], ssel_ref.at[n], vst_ref.at[n])
                    for n in range(i + 1)])
        cur["win"]["m"] = jnp.max(cur["win"]["m8"], axis=0, keepdims=True)
        nxt = begin(i + 1) if i + 1 < n_tiles else None
        interleave([functools.partial(sel_scores, nxt, n) for n in range(i + 2)] if nxt else [],
                   [functools.partial(weights, cur["win"], swin_ref.at[i - n], vwt_ref.at[n])
                    for n in win_tiles(i)])
        finish(cur)
        cur = nxt


def _nsa_attention(proj, gates, kcmp, vcmpt, overlap_t, batch, seq):
    n = proj.shape[0]
    n_tiles = seq // KEY_TILE
    n_sel = seq // NSA_SLC_BLOCK
    g = NSA_KV_GROUPS
    hg = NSA_HEADS_PER_GROUP
    n_cmp_pad = kcmp.shape[1]
    back = NSA_WINDOW // KEY_TILE
    ks0, kw0, vsw0 = 12, 16, 20
    return pl.pallas_call(
        functools.partial(_nsa_kernel, n_tiles=n_tiles, n_sel=n_sel),
        grid=(batch, g),
        in_specs=[pl.BlockSpec((seq, hg * HEAD_DIM), lambda b, k: (b, k)),
                  pl.BlockSpec((1, n_cmp_pad, LANES), lambda b, k: (b * g + k, 0, 0)),
                  pl.BlockSpec((1, HEAD_DIM, n_cmp_pad), lambda b, k: (b * g + k, 0, 0)),
                  pl.BlockSpec((seq, LANES), lambda b, k: (b, ks0 + k)),
                  pl.BlockSpec((seq, LANES), lambda b, k: (b, kw0 + k)),
                  pl.BlockSpec((seq, LANES), lambda b, k: (b, vsw0 + k)),
                  pl.BlockSpec((seq, LANES), lambda b, k: (b, 0)),
                  _const_spec(overlap_t.shape)],
        out_specs=pl.BlockSpec((seq, hg * HEAD_DIM), lambda b, k: (b, k)),
        out_shape=jax.ShapeDtypeStruct((n, N_HEADS * HEAD_DIM), BF16),
        scratch_shapes=[pltpu.VMEM((n_tiles, HEAD_DIM + ONES_ROWS, KEY_TILE), BF16),
                        pltpu.VMEM((n_tiles, HEAD_DIM + ONES_ROWS, KEY_TILE), BF16),
                        pltpu.VMEM((LANES, seq), F32),
                        pltpu.VMEM((n_tiles, KEY_TILE, hg * Q_TILE), F32),
                        pltpu.VMEM((back + 1, KEY_TILE, hg * Q_TILE), F32)],
        compiler_params=pltpu.CompilerParams(
            dimension_semantics=("arbitrary", "arbitrary"),
            vmem_limit_bytes=VMEM_LIMIT),
        name="nsa_attention",
    )(proj, kcmp, vcmpt, proj, proj, proj, gates, overlap_t)


def _rope_tables(seq):
    inv_freq = 1.0 / (ROPE_THETA ** (jnp.arange(0, HEAD_DIM, 2, dtype=F32) / HEAD_DIM))
    ang = jnp.arange(seq, dtype=F32)[:, None] * inv_freq[None, :]
    reps = LANES // (HEAD_DIM // 2)
    cos = jnp.tile(jnp.cos(ang), (1, reps))
    sin = jnp.tile(jnp.sin(ang), (1, reps))
    first_half = (jnp.arange(LANES) % HEAD_DIM) < HEAD_DIM // 2
    return cos, jnp.where(first_half, -sin, 0.0), jnp.where(first_half, 0.0, sin)


def _nsa_weight_layout(w_in):
    qd = N_HEADS * HEAD_DIM
    kv = NSA_KV_COLS
    q, kc, vc, ks, vs, kw, vw, gt = jnp.split(
        w_in, np.cumsum([qd] + [kv] * 6).tolist(), axis=1)

    def dup(w):
        w = w.reshape(D_MODEL, NSA_KV_GROUPS, 1, HEAD_DIM)
        return jnp.broadcast_to(w, (D_MODEL, NSA_KV_GROUPS, 2, HEAD_DIM)).reshape(D_MODEL, -1)

    vsw = jnp.concatenate([vs.reshape(D_MODEL, NSA_KV_GROUPS, 1, HEAD_DIM),
                           vw.reshape(D_MODEL, NSA_KV_GROUPS, 1, HEAD_DIM)], axis=2)
    main = jnp.concatenate([q, kc, vc, dup(ks), dup(kw), vsw.reshape(D_MODEL, -1)], axis=1)
    per_group = 3 * NSA_HEADS_PER_GROUP
    gt = gt.reshape(D_MODEL, NSA_KV_GROUPS, per_group)
    gt = jnp.pad(gt, ((0, 0), (0, 0), (0, 16 - per_group))).reshape(D_MODEL, -1)
    gt = jnp.pad(gt, ((0, 0), (0, LANES - gt.shape[1])))
    return main.astype(BF16), gt.astype(BF16)


def _overlap_t(seq):
    n_cmp = (seq - NSA_CMP_BLOCK) // NSA_CMP_STRIDE + 1
    n_sel = seq // NSA_SLC_BLOCK
    c_start = np.arange(n_cmp) * NSA_CMP_STRIDE
    s_start = np.arange(n_sel) * NSA_SLC_BLOCK
    lo = np.maximum(c_start[:, None], s_start[None, :])
    hi = np.minimum(c_start[:, None] + NSA_CMP_BLOCK, s_start[None, :] + NSA_SLC_BLOCK)
    ov = np.clip(hi - lo, 0, None) / NSA_CMP_BLOCK
    ov = np.pad(ov, ((0, seq // NSA_CMP_STRIDE - n_cmp), (0, 0)))
    return jnp.asarray(ov.T, dtype=BF16)


_QK_ROPE = (True,) * 16 + (False,) * 8
_NO_ROPE = (False,) * 24
_NSA_ROPE = (True,) * 10 + (False,) * 2 + (True,) * 8 + (False,) * 4


def kernel(x, l0_ln_mix_pre, l0_w_in, l0_w_out, l0_ln_mix_post, l0_ln_mlp_pre, l0_w_up, l0_w_down, l0_ln_mlp_post, l1_ln_mix_pre, l1_w_in, l1_w_out, l1_ln_mix_post, l1_ln_mlp_pre, l1_w_up, l1_w_down, l1_ln_mlp_post, l2_ln_mix_pre, l2_w_in, l2_cmp_pos_k, l2_cmp_w1_k, l2_cmp_w2_k, l2_cmp_pos_v, l2_cmp_w1_v, l2_cmp_w2_v, l2_w_out, l2_ln_mix_post, l2_ln_mlp_pre, l2_w_up, l2_w_down, l2_ln_mlp_post, l3_ln_mix_pre, l3_w_in, l3_w_out, l3_ln_mix_post, l3_ln_mlp_pre, l3_w_up, l3_w_down, l3_ln_mlp_post):
    batch, seq, _ = x.shape
    assert seq % ROW_TILE == 0 and seq % KEY_TILE == 0 and Q_TILE == KEY_TILE == MOBA_BLOCK
    tables = _rope_tables(seq)
    x2 = x.reshape(batch * seq, D_MODEL)
    bf = lambda w: w.astype(BF16)

    def tail(o, x2, w_out, g_post, g_pre, w_up, w_down, g_mlp):
        return _layer_tail(o, x2, bf(w_out), g_post, g_pre, bf(w_up), bf(w_down), g_mlp)

    qkv = _norm_proj(x2, l0_ln_mix_pre, bf(l0_w_in), tables, _QK_ROPE, seq)
    x2 = tail(_moba_attention(qkv, batch, seq), x2, l0_w_out, l0_ln_mix_post, l0_ln_mlp_pre,
              l0_w_up, l0_w_down, l0_ln_mlp_post)

    qkv = _norm_proj(x2, l1_ln_mix_pre, bf(l1_w_in), tables, _NO_ROPE, seq)
    x2 = tail(_sb_attention(qkv, batch, seq), x2, l1_w_out, l1_ln_mix_post, l1_ln_mlp_pre,
              l1_w_up, l1_w_down, l1_ln_mlp_post)

    w_main, w_gate = _nsa_weight_layout(l2_w_in)
    proj, gates = _norm_proj(x2, l2_ln_mix_pre, w_main, tables, _NSA_ROPE, seq, wg=w_gate)
    qd = N_HEADS * HEAD_DIM
    rows = seq // NSA_CMP_STRIDE

    def row_groups(cols):
        a = cols.reshape(batch, seq, NSA_KV_GROUPS, HEAD_DIM).transpose(0, 2, 1, 3)
        return a.reshape(batch * NSA_KV_GROUPS, rows, NSA_CMP_STRIDE * HEAD_DIM)

    flat_pos = lambda p: p.reshape(2, NSA_CMP_STRIDE * HEAD_DIM)
    w2_dup = lambda w: bf(jnp.concatenate([w, w], axis=1))
    kcmp, vcmpt = _nsa_compress(
        row_groups(proj[:, qd:qd + NSA_KV_COLS]), row_groups(proj[:, qd + NSA_KV_COLS:qd + 2 * NSA_KV_COLS]),
        flat_pos(l2_cmp_pos_k), flat_pos(l2_cmp_pos_v), bf(l2_cmp_w1_k), bf(l2_cmp_w1_v),
        w2_dup(l2_cmp_w2_k), w2_dup(l2_cmp_w2_v))
    o = _nsa_attention(proj, gates, kcmp, vcmpt, _overlap_t(seq), batch, seq)
    x2 = tail(o, x2, l2_w_out, l2_ln_mix_post, l2_ln_mlp_pre, l2_w_up, l2_w_down, l2_ln_mlp_post)

    qkv = _norm_proj(x2, l3_ln_mix_pre, bf(l3_w_in), tables, _QK_ROPE, seq)
    x2 = tail(_moba_attention(qkv, batch, seq), x2, l3_w_out, l3_ln_mix_post, l3_ln_mlp_pre,
              l3_w_up, l3_w_down, l3_ln_mlp_post)
    return x2.reshape(batch, seq, D_MODEL)
```

```python
import functools
import math

import numpy as np
import jax
import jax.numpy as jnp
from jax import lax
from jax.experimental import pallas as pl
from jax.experimental.pallas import tpu as pltpu

D_MODEL = 1024
N_HEADS = 16
HEAD_DIM = 64
D_FF = 4 * D_MODEL
ROPE_THETA = 10000.0
NORM_EPS = 1e-6
NEG_INF = -1e30
POS_BIG = 1e30

MOBA_BLOCK = 256
MOBA_TOPK = 3

NSA_KV_GROUPS = 4
NSA_HEADS_PER_GROUP = N_HEADS // NSA_KV_GROUPS
NSA_CMP_BLOCK = 32
NSA_CMP_STRIDE = 16
NSA_CMP_HIDDEN = 256
NSA_SLC_BLOCK = 64
NSA_SLC_TOPN = 16
NSA_WINDOW = 512
NSA_KV_COLS = NSA_KV_GROUPS * HEAD_DIM

LANES = 128
SUBLANES = 8
KEY_TILE = 256
ONES_ROWS = 16
Q_TILE = 256
ROW_TILE = 512
FF_CHUNK = 512
PROJ_COLS = 3 * N_HEADS * HEAD_DIM
VMEM_LIMIT = 56 * 1024 * 1024
QUERY_SCALE = HEAD_DIM ** -0.5 * math.log2(math.e)
SB_UNDERFLOW_LOG2 = -150.0

F32 = jnp.float32
BF16 = jnp.bfloat16


def _nt(a, b):
    return lax.dot_general(a, b, (((1,), (1,)), ((), ())), preferred_element_type=F32)


def _nn(a, b):
    return jnp.dot(a, b, preferred_element_type=F32)


def _split_bf16(x):
    hi = x.astype(BF16)
    lo = (x - hi.astype(F32)).astype(BF16)
    return hi, lo


def _rms(x, g):
    ms = jnp.mean(x * x, axis=-1, keepdims=True)
    return x * lax.rsqrt(ms + NORM_EPS) * g


def _const_spec(shape):
    nd = len(shape)
    return pl.BlockSpec(shape, lambda *_: (0,) * nd, pipeline_mode=pl.Buffered(1))


def _proj_kernel(x_ref, g_ref, w_ref, cos_ref, sina_ref, sinb_ref, *rest, rope_slabs, has_gate):
    if has_gate:
        wg_ref, o_ref, og_ref = rest
    else:
        (o_ref,) = rest
    h = _rms(x_ref[...], g_ref[...]).astype(BF16)
    cos = cos_ref[...]
    sina = sina_ref[...]
    sinb = sinb_ref[...]
    n_slabs = len(rope_slabs)
    for c in range(n_slabs // 2):
        y = _nn(h, w_ref[:, c * 2 * LANES:(c + 1) * 2 * LANES])
        for half in range(2):
            s = 2 * c + half
            ys = y[:, half * LANES:(half + 1) * LANES]
            if s < N_HEADS * HEAD_DIM // LANES:
                ys = ys * QUERY_SCALE
            if rope_slabs[s]:
                ys = (ys * cos + pltpu.roll(ys, LANES - HEAD_DIM // 2, 1) * sina
                      + pltpu.roll(ys, HEAD_DIM // 2, 1) * sinb)
            o_ref[:, s * LANES:(s + 1) * LANES] = ys.astype(o_ref.dtype)
    if has_gate:
        og_ref[...] = _nn(h, wg_ref[...])


def _norm_proj(x2, g, w, tables, rope_slabs, seq, wg=None):
    n = x2.shape[0]
    cols = w.shape[1]
    pos_blocks = seq // ROW_TILE
    has_gate = wg is not None
    row = lambda i: (i, 0)
    tab = pl.BlockSpec((ROW_TILE, LANES), lambda i: (i % pos_blocks, 0))
    in_specs = [pl.BlockSpec((ROW_TILE, D_MODEL), row), _const_spec((1, D_MODEL)),
                _const_spec((D_MODEL, cols)), tab, tab, tab]
    args = [x2, g.reshape(1, D_MODEL), w, *tables]
    out_shape = [jax.ShapeDtypeStruct((n, cols), BF16)]
    out_specs = [pl.BlockSpec((ROW_TILE, cols), row)]
    if has_gate:
        in_specs.append(_const_spec((D_MODEL, LANES)))
        args.append(wg)
        out_shape.append(jax.ShapeDtypeStruct((n, LANES), F32))
        out_specs.append(pl.BlockSpec((ROW_TILE, LANES), row))
    outs = pl.pallas_call(
        functools.partial(_proj_kernel, rope_slabs=rope_slabs, has_gate=has_gate),
        grid=(n // ROW_TILE,),
        in_specs=in_specs,
        out_specs=out_specs,
        out_shape=out_shape,
        compiler_params=pltpu.CompilerParams(dimension_semantics=("arbitrary",),
                                             vmem_limit_bytes=VMEM_LIMIT),
        name="norm_proj",
    )(*args)
    return outs if has_gate else outs[0]


def _tail_kernel(o_ref, x_ref, wout_ref, gpost_ref, gpre_ref, wup_ref, wdown_ref, gmlp_ref, out_ref):
    m = _nn(o_ref[...], wout_ref[...])
    x1 = x_ref[...] + _rms(m, gpost_ref[...])
    h = _rms(x1, gpre_ref[...]).astype(BF16)
    acc = jnp.zeros((ROW_TILE, D_MODEL), F32)
    for j in range(D_FF // FF_CHUNK):
        u = _nn(h, wup_ref[:, j * FF_CHUNK:(j + 1) * FF_CHUNK].astype(BF16))
        a = jnp.square(jnp.maximum(u, 0.0)).astype(BF16)
        acc = acc + _nn(a, wdown_ref[j * FF_CHUNK:(j + 1) * FF_CHUNK, :].astype(BF16))
    out_ref[...] = x1 + _rms(acc, gmlp_ref[...])


def _layer_tail(o2, x2, wout, gpost, gpre, wup, wdown, gmlp):
    n = x2.shape[0]
    row = lambda i: (i, 0)
    vec = lambda v: v.reshape(1, D_MODEL)
    return pl.pallas_call(
        _tail_kernel,
        grid=(n // ROW_TILE,),
        in_specs=[pl.BlockSpec((ROW_TILE, N_HEADS * HEAD_DIM), row),
                  pl.BlockSpec((ROW_TILE, D_MODEL), row),
                  _const_spec((N_HEADS * HEAD_DIM, D_MODEL)), _const_spec((1, D_MODEL)),
                  _const_spec((1, D_MODEL)), _const_spec((D_MODEL, D_FF)),
                  _const_spec((D_FF, D_MODEL)), _const_spec((1, D_MODEL))],
        out_specs=pl.BlockSpec((ROW_TILE, D_MODEL), row),
        out_shape=jax.ShapeDtypeStruct((n, D_MODEL), F32),
        compiler_params=pltpu.CompilerParams(dimension_semantics=("arbitrary",),
                                             vmem_limit_bytes=VMEM_LIMIT),
        name="layer_tail",
    )(o2, x2, wout, vec(gpost), vec(gpre), wup, wdown, vec(gmlp))


def _pair_queries(q):
    lane = lax.broadcasted_iota(jnp.int32, q.shape, 1)
    zero = jnp.zeros_like(q)
    return jnp.concatenate([jnp.where(lane < HEAD_DIM, q, zero),
                            jnp.where(lane >= HEAD_DIM, q, zero)], axis=0)


def _transpose_value_tiles(v_ref, vt_ref, n_tiles):
    for n in range(n_tiles):
        blk = v_ref[n * KEY_TILE:(n + 1) * KEY_TILE, :].astype(F32)
        vt_ref[n] = blk.T.astype(BF16)


def _fold_rows(x, op):
    r, w = x.shape
    return op(x.reshape(r // SUBLANES, SUBLANES, w), axis=0)


def _pair_pv(vt, p):
    t = p.shape[1] // 2
    return _nn(vt[:HEAD_DIM, :], p[:, :t]), _nn(vt[HEAD_DIM:, :], p[:, t:])


def _pair_output(acc_a, acc_b, inv_l=None):
    if inv_l is not None:
        t = acc_a.shape[1]
        acc_a = acc_a * inv_l[:, :t]
        acc_b = acc_b * inv_l[:, t:]
    return jnp.concatenate([acc_a, acc_b], axis=0).T


def _moba_kernel(q_ref, k_ref, v_ref, o_ref, kmean_ref, vt_ref, s_ref, *, n_blocks):
    for n in range(n_blocks):
        blk = k_ref[n * MOBA_BLOCK:(n + 1) * MOBA_BLOCK, :].astype(F32)
        kmean_ref[n:n + 1, :] = jnp.mean(blk, axis=0, keepdims=True)
    _transpose_value_tiles(v_ref, vt_ref, n_blocks)
    km_hi, km_lo = _split_bf16(kmean_ref[...])
    kpos = lax.broadcasted_iota(jnp.int32, (KEY_TILE, Q_TILE), 0)
    tpos = lax.broadcasted_iota(jnp.int32, (KEY_TILE, Q_TILE), 1)
    causal = kpos <= tpos

    def begin(i):
        qcat = _pair_queries(q_ref[i * Q_TILE:(i + 1) * Q_TILE, :])
        chosen = [None] * i
        if i > MOBA_TOPK:
            gate = _nt(km_hi, qcat) + _nt(km_lo, qcat)
            blk_id = lax.broadcasted_iota(jnp.int32, gate.shape, 0)
            g = jnp.where(blk_id < i, gate, NEG_INF)
            sel = jnp.zeros(gate.shape, F32)
            for _ in range(MOBA_TOPK):
                mx = jnp.max(g, axis=0, keepdims=True)
                first = jnp.min(jnp.where(g == mx, blk_id, n_blocks), axis=0, keepdims=True)
                hit = blk_id == first
                sel = jnp.where(hit, 1.0, sel)
                g = jnp.where(hit, -jnp.inf, g)
            chosen = [sel[n:n + 1, :] > 0.5 for n in range(i)]
        heads = [dict(m8=None, l8=jnp.zeros((SUBLANES, Q_TILE), F32), acc=jnp.zeros((HEAD_DIM, Q_TILE), F32))
                 for _ in range(2)]
        return dict(i=i, qcat=qcat, chosen=chosen, heads=heads)

    def score_tile(st, n, h):
        i = st["i"]
        hd = st["heads"][h]
        lanes = slice(h * Q_TILE, (h + 1) * Q_TILE)
        s = _nt(k_ref[n * KEY_TILE:(n + 1) * KEY_TILE, :], st["qcat"][lanes, :])
        if n == i:
            s = jnp.where(causal, s, NEG_INF)
        s_ref[i % 2, n, :, lanes] = s
        bm = _fold_rows(s, jnp.max)
        if n < i and st["chosen"][n] is not None:
            bm = jnp.where(st["chosen"][n][:, lanes], bm, NEG_INF)
        hd["m8"] = bm if hd["m8"] is None else jnp.maximum(hd["m8"], bm)

    def weight_tile(st, n, h):
        i = st["i"]
        hd = st["heads"][h]
        lanes = slice(h * Q_TILE, (h + 1) * Q_TILE)
        keep = None if (n == i or st["chosen"][n] is None) else st["chosen"][n][:, lanes]
        shift = hd["m"] if keep is None else jnp.where(keep, hd["m"], POS_BIG)
        p = jnp.exp2(s_ref[i % 2, n, :, lanes] - shift)
        hd["l8"] = hd["l8"] + _fold_rows(p, jnp.sum)
        hd["acc"] = hd["acc"] + _nn(vt_ref[n, h * HEAD_DIM:(h + 1) * HEAD_DIM, :], p.astype(BF16))

    cur = begin(0)
    for h in range(2):
        score_tile(cur, 0, h)
    for i in range(n_blocks):
        for hd in cur["heads"]:
            hd["m"] = jnp.max(hd["m8"], axis=0, keepdims=True)
        nxt = begin(i + 1) if i + 1 < n_blocks else None
        for n in range(i + 2):
            for h in range(2):
                if nxt is not None:
                    score_tile(nxt, n, h)
                if n <= i:
                    weight_tile(cur, n, h)
        acc = [hd["acc"] * (1.0 / jnp.sum(hd["l8"], axis=0, keepdims=True)) for hd in cur["heads"]]
        o_ref[i * Q_TILE:(i + 1) * Q_TILE, :] = _pair_output(acc[0], acc[1]).astype(o_ref.dtype)
        cur = nxt


def _moba_attention(qkv, batch, seq):
    n = qkv.shape[0]
    n_blocks = seq // MOBA_BLOCK
    pairs = N_HEADS // 2
    return pl.pallas_call(
        functools.partial(_moba_kernel, n_blocks=n_blocks),
        grid=(batch, pairs),
        in_specs=[pl.BlockSpec((seq, LANES), lambda b, p: (b, p)),
                  pl.BlockSpec((seq, LANES), lambda b, p: (b, pairs + p)),
                  pl.BlockSpec((seq, LANES), lambda b, p: (b, 2 * pairs + p))],
        out_specs=pl.BlockSpec((seq, LANES), lambda b, p: (b, p)),
        out_shape=jax.ShapeDtypeStruct((n, N_HEADS * HEAD_DIM), BF16),
        scratch_shapes=[pltpu.VMEM((n_blocks, LANES), F32),
                        pltpu.VMEM((n_blocks, LANES, KEY_TILE), BF16),
                        pltpu.VMEM((2, n_blocks, KEY_TILE, 2 * Q_TILE), F32)],
        compiler_params=pltpu.CompilerParams(
            dimension_semantics=("arbitrary", "arbitrary"),
            vmem_limit_bytes=VMEM_LIMIT),
        name="moba_attention",
    )(qkv, qkv, qkv)


def _log2_sigmoid_pair(z):
    log_beta = jnp.minimum(z, 0.0) - jnp.log2(1.0 + jnp.exp2(-jnp.abs(z)))
    return log_beta, log_beta - z


def _sb_kernel(q_ref, k_ref, v_ref, o_ref, vt_ref, acc_ref, carry_ref, *, n_tiles):
    w2 = 2 * Q_TILE
    near = 3
    _transpose_value_tiles(v_ref, vt_ref, n_tiles)
    r = lax.broadcasted_iota(jnp.int32, (KEY_TILE, KEY_TILE), 0)
    c = lax.broadcasted_iota(jnp.int32, (KEY_TILE, KEY_TILE), 1)
    later = jnp.where(c > r, 1.0, 0.0).astype(BF16)
    kpos = lax.broadcasted_iota(jnp.int32, (KEY_TILE, w2), 0)
    tpos = lax.broadcasted_iota(jnp.int32, (KEY_TILE, w2), 1) % Q_TILE
    strict = kpos < tpos

    def scores(state, t):
        st = state[t["i"]]
        t["z"] = _nt(k_ref[t["n"] * KEY_TILE:(t["n"] + 1) * KEY_TILE, :], st["qcat"])

    def log_terms(state, t):
        st = state[t["i"]]
        log_beta, log_1m = _log2_sigmoid_pair(t.pop("z"))
        if t["n"] == t["i"]:
            log_1m = jnp.where(strict, log_1m, 0.0)
        t["carry"] = st["carry"]
        if t["n"] > 0:
            tile_sum = jnp.sum(log_1m, axis=0, keepdims=True)
            st["carry"] = tile_sum if st["carry"] is None else st["carry"] + tile_sum
        t["log_beta"] = log_beta
        t["suffix"] = _nn(later, log_1m.astype(BF16))

    def weights(state, t):
        st = state[t["i"]]
        expo = t.pop("log_beta") + t.pop("suffix")
        if t["carry"] is not None:
            expo = expo + t["carry"]
        wgt = jnp.exp2(expo)
        if t["n"] == t["i"]:
            wgt = jnp.where(strict, wgt, 0.0)
        pv_a, pv_b = _pair_pv(vt_ref[t["n"]], wgt.astype(BF16))
        st["acc_a"] = st["acc_a"] + pv_a
        st["acc_b"] = st["acc_b"] + pv_b

    def run(state, tiles):
        stages = (scores, log_terms, weights)
        for step in range(len(tiles) + len(stages) - 1):
            for lag, stage in enumerate(stages):
                if 0 <= step - lag < len(tiles):
                    stage(state, tiles[step - lag])

    def query_operand(i):
        return _pair_queries(q_ref[i * Q_TILE:(i + 1) * Q_TILE, :])

    def write_out(i, st):
        o_ref[i * Q_TILE:(i + 1) * Q_TILE, :] = _pair_output(st["acc_a"], st["acc_b"]).astype(o_ref.dtype)

    zeros = jnp.zeros((HEAD_DIM, Q_TILE), F32)
    state = {i: dict(qcat=query_operand(i), carry=None, acc_a=zeros, acc_b=zeros) for i in range(n_tiles)}
    run(state, [dict(i=i, n=n) for i in range(n_tiles) for n in range(i, max(i - near, -1), -1)])
    for i in range(n_tiles):
        write_out(i, state[i])
        if i >= near:
            acc_ref[i] = jnp.concatenate([state[i]["acc_a"], state[i]["acc_b"]], axis=0)
            carry_ref[i] = state[i]["carry"]

    live = {i: jnp.max(state[i]["carry"]) > SB_UNDERFLOW_LOG2 for i in range(near, n_tiles)}
    for i in range(near, n_tiles):
        @pl.when(live[i])
        def _(i=i):
            acc = acc_ref[i]
            far = {i: dict(qcat=query_operand(i), carry=carry_ref[i],
                           acc_a=acc[:HEAD_DIM, :], acc_b=acc[HEAD_DIM:, :])}
            run(far, [dict(i=i, n=n) for n in range(i - near, -1, -1)])
            write_out(i, far[i])


def _sb_attention(qkv, batch, seq):
    n = qkv.shape[0]
    n_tiles = seq // KEY_TILE
    pairs = N_HEADS // 2
    return pl.pallas_call(
        functools.partial(_sb_kernel, n_tiles=n_tiles),
        grid=(batch, pairs),
        in_specs=[pl.BlockSpec((seq, LANES), lambda b, p: (b, p)),
                  pl.BlockSpec((seq, LANES), lambda b, p: (b, pairs + p)),
                  pl.BlockSpec((seq, LANES), lambda b, p: (b, 2 * pairs + p))],
        out_specs=pl.BlockSpec((seq, LANES), lambda b, p: (b, p)),
        out_shape=jax.ShapeDtypeStruct((n, N_HEADS * HEAD_DIM), BF16),
        scratch_shapes=[pltpu.VMEM((n_tiles, LANES, KEY_TILE), BF16),
                        pltpu.VMEM((n_tiles, LANES, Q_TILE), F32),
                        pltpu.VMEM((n_tiles, 1, 2 * Q_TILE), F32)],
        compiler_params=pltpu.CompilerParams(
            dimension_semantics=("arbitrary", "arbitrary"),
            vmem_limit_bytes=VMEM_LIMIT),
        name="stick_breaking_attention",
    )(qkv, qkv, qkv)


def _nsa_compress_kernel(ak_ref, av_ref, posk_ref, posv_ref, w1k_ref, w1v_ref, w2k_ref, w2v_ref,
                         kc_ref, vct_ref):
    half = NSA_CMP_STRIDE * HEAD_DIM
    n_rows = ak_ref.shape[1]

    def compress(a_ref, pos_ref, w1_ref, w2_ref):
        a = a_ref[0].astype(F32)
        first = (a + pos_ref[0:1, :]).astype(BF16)
        second = (a + pos_ref[1:2, :]).astype(BF16)
        pre = _nn(first, w1_ref[:half, :]) + pltpu.roll(_nn(second, w1_ref[half:, :]), n_rows - 1, 0)
        return _nn(jax.nn.gelu(pre).astype(BF16), w2_ref[...])

    kc_ref[0] = compress(ak_ref, posk_ref, w1k_ref, w2k_ref).astype(BF16)
    vct_ref[0] = compress(av_ref, posv_ref, w1v_ref, w2v_ref).T[:HEAD_DIM, :].astype(BF16)


def _nsa_compress(ak, av, posk, posv, w1k, w1v, w2k, w2v):
    bg, rows, feat = ak.shape
    blk = pl.BlockSpec((1, rows, feat), lambda i: (i, 0, 0))
    return pl.pallas_call(
        _nsa_compress_kernel,
        grid=(bg,),
        in_specs=[blk, blk, _const_spec(posk.shape), _const_spec(posv.shape),
                  _const_spec(w1k.shape), _const_spec(w1v.shape),
                  _const_spec(w2k.shape), _const_spec(w2v.shape)],
        out_specs=[pl.BlockSpec((1, rows, LANES), lambda i: (i, 0, 0)),
                   pl.BlockSpec((1, HEAD_DIM, rows), lambda i: (i, 0, 0))],
        out_shape=[jax.ShapeDtypeStruct((bg, rows, LANES), BF16),
                   jax.ShapeDtypeStruct((bg, HEAD_DIM, rows), BF16)],
        compiler_params=pltpu.CompilerParams(dimension_semantics=("arbitrary",),
                                             vmem_limit_bytes=VMEM_LIMIT),
        name="nsa_compress",
    )(ak, av, posk, posv, w1k, w1v, w2k, w2v)


def _nsa_kernel(q_ref, kcmp_ref, vcmpt_ref, ks_ref, kw_ref, vsw_ref, gate_ref, ovl_ref, o_ref,
                vst_ref, vwt_ref, gt_ref, ssel_ref, swin_ref, *, n_tiles, n_sel):
    grp = pl.program_id(1)
    hg = NSA_HEADS_PER_GROUP
    t = Q_TILE
    wq = hg * t
    sub = KEY_TILE // NSA_SLC_BLOCK
    back = NSA_WINDOW // KEY_TILE
    n_top = min(NSA_SLC_TOPN, n_sel)

    for n in range(n_tiles):
        rows = slice(n * KEY_TILE, (n + 1) * KEY_TILE)
        blk = vsw_ref[rows, :].astype(F32).T.astype(BF16)
        for vt_ref, part in ((vst_ref, blk[:HEAD_DIM, :]), (vwt_ref, blk[HEAD_DIM:, :])):
            vt_ref[n, :HEAD_DIM, :] = part
            vt_ref[n, HEAD_DIM:, :] = jnp.ones((ONES_ROWS, KEY_TILE), BF16)
        gt_ref[:, rows] = jax.nn.sigmoid(gate_ref[rows, :]).T
    kloc = lax.broadcasted_iota(jnp.int32, (KEY_TILE, wq), 0)
    tloc = lax.broadcasted_iota(jnp.int32, (KEY_TILE, wq), 1) % t
    causal = kloc <= tloc
    window_tail = kloc > tloc

    def begin(i):
        rows = slice(i * t, (i + 1) * t)
        qcat = jnp.concatenate([_pair_queries(q_ref[rows, :LANES]),
                                _pair_queries(q_ref[rows, LANES:])], axis=0)
        tq = i * t + lax.broadcasted_iota(jnp.int32, (1, wq), 1) % t

        n_cmp_pad = kcmp_ref.shape[1]
        reach = min(n_cmp_pad, (i + 1) * t // NSA_CMP_STRIDE)
        s = _nt(kcmp_ref[0, :reach, :], qcat)
        cmp_end = lax.broadcasted_iota(jnp.int32, s.shape, 0) * NSA_CMP_STRIDE + (NSA_CMP_BLOCK - 1)
        valid = cmp_end <= tq
        m = jnp.max(jnp.where(valid, s, NEG_INF), axis=0, keepdims=True)
        e = jnp.where(valid, jnp.exp2(s - m), 0.0)
        l = jnp.sum(e, axis=0, keepdims=True)
        p_cmp = e * jnp.where(l > 0.0, 1.0 / l, 0.0)
        if reach < n_cmp_pad:
            p_cmp = jnp.concatenate([p_cmp, jnp.zeros((n_cmp_pad - reach, wq), F32)], axis=0)
        o_cmp = _nn(vcmpt_ref[0], p_cmp.astype(BF16))

        bias = None
        if sub * (i + 1) > n_top:
            p_grp = p_cmp[:, :t]
            for h in range(1, hg):
                p_grp = p_grp + p_cmp[:, h * t:(h + 1) * t]
            p_hi, p_lo = _split_bf16(p_grp)
            imp = _nn(ovl_ref[...], p_hi) + _nn(ovl_ref[...], p_lo)
            j = lax.broadcasted_iota(jnp.int32, imp.shape, 0)
            q_blk = (i * t + lax.broadcasted_iota(jnp.int32, imp.shape, 1)) // NSA_SLC_BLOCK
            forced = jnp.logical_or(j == 0, jnp.logical_or(j == q_blk, j == q_blk - 1))
            score = jnp.where(j <= q_blk, jnp.where(forced, POS_BIG, imp), NEG_INF)
            sel = jnp.zeros(imp.shape, F32)
            for _ in range(n_top):
                mx = jnp.max(score, axis=0, keepdims=True)
                first = jnp.min(jnp.where(score == mx, j, n_sel), axis=0, keepdims=True)
                hit = j == first
                sel = jnp.where(jnp.logical_and(hit, mx > 0.5 * NEG_INF), 1.0, sel)
                score = jnp.where(hit, -jnp.inf, score)
            bias = jnp.concatenate([jnp.where(sel > 0.5, 0.0, NEG_INF)] * hg, axis=1)
        zeros_acc = jnp.zeros((HEAD_DIM + ONES_ROWS, wq), F32)
        return dict(i=i, qcat=qcat, o_cmp=o_cmp, bias=bias,
                    sel=dict(m8=None, acc=zeros_acc), win=dict(m8=None, acc=zeros_acc))

    def note_max(br, s):
        bm = _fold_rows(s, jnp.max)
        br["m8"] = bm if br["m8"] is None else jnp.maximum(br["m8"], bm)

    def sel_scores(st, n):
        s = _nt(ks_ref[n * KEY_TILE:(n + 1) * KEY_TILE, :], st["qcat"])
        if st["bias"] is not None:
            s = jnp.concatenate(
                [s[jj * NSA_SLC_BLOCK:(jj + 1) * NSA_SLC_BLOCK, :]
                 + st["bias"][n * sub + jj:n * sub + jj + 1, :] for jj in range(sub)], axis=0)
        if n == st["i"]:
            s = jnp.where(causal, s, NEG_INF)
        ssel_ref[n] = s
        note_max(st["sel"], s)

    def win_scores(st, n):
        s = _nt(kw_ref[n * KEY_TILE:(n + 1) * KEY_TILE, :], st["qcat"])
        if n == st["i"]:
            s = jnp.where(causal, s, NEG_INF)
        elif n == st["i"] - back:
            s = jnp.where(window_tail, s, NEG_INF)
        swin_ref[st["i"] - n] = s
        note_max(st["win"], s)

    def weights(br, s_view, v_view):
        p = jnp.exp2(s_view[...] - br["m"])
        br["acc"] = br["acc"] + _nn(v_view[...], p.astype(BF16))

    def finish(st):
        i = st["i"]
        o_slc, o_win = [br["acc"][:HEAD_DIM, :] * (1.0 / br["acc"][HEAD_DIM:HEAD_DIM + 1, :])
                        for br in (st["sel"], st["win"])]
        gates = gt_ref[pl.ds(pl.multiple_of(grp * 16, 16), 16), i * t:(i + 1) * t]
        heads = []
        for h in range(hg):
            cols = slice(h * t, (h + 1) * t)
            heads.append(gates[3 * h:3 * h + 1, :] * st["o_cmp"][:, cols]
                         + gates[3 * h + 1:3 * h + 2, :] * o_slc[:, cols]
                         + gates[3 * h + 2:3 * h + 3, :] * o_win[:, cols])
        for pair in range(hg // 2):
            both = jnp.concatenate([heads[2 * pair], heads[2 * pair + 1]], axis=0)
            o_ref[i * t:(i + 1) * t, pair * LANES:(pair + 1) * LANES] = both.T.astype(o_ref.dtype)

    def interleave(first, second):
        for k in range(max(len(first), len(second))):
            for steps in (first, second):
                if k < len(steps):
                    steps[k]()

    def win_tiles(i):
        return [n for n in range(i, i - back - 1, -1) if n >= 0]

    cur = begin(0)
    sel_scores(cur, 0)
    for i in range(n_tiles):
        cur["sel"]["m"] = jnp.max(cur["sel"]["m8"], axis=0, keepdims=True)
        interleave([functools.partial(win_scores, cur, n) for n in win_tiles(i)],
                   [functools.partial(weights, cur["sel"], ssel_ref.at[n], vst_ref.at[n])
                    for n in range(i + 1)])
        cur["win"]["m"] = jnp.max(cur["win"]["m8"], axis=0, keepdims=True)
        nxt = begin(i + 1) if i + 1 < n_tiles else None
        interleave([functools.partial(sel_scores, nxt, n) for n in range(i + 2)] if nxt else [],
                   [functools.partial(weights, cur["win"], swin_ref.at[i - n], vwt_ref.at[n])
                    for n in win_tiles(i)])
        finish(cur)
        cur = nxt


def _nsa_attention(proj, gates, kcmp, vcmpt, overlap_t, batch, seq):
    n = proj.shape[0]
    n_tiles = seq // KEY_TILE
    n_sel = seq // NSA_SLC_BLOCK
    g = NSA_KV_GROUPS
    hg = NSA_HEADS_PER_GROUP
    n_cmp_pad = kcmp.shape[1]
    back = NSA_WINDOW // KEY_TILE
    ks0, kw0, vsw0 = 12, 16, 20
    return pl.pallas_call(
        functools.partial(_nsa_kernel, n_tiles=n_tiles, n_sel=n_sel),
        grid=(batch, g),
        in_specs=[pl.BlockSpec((seq, hg * HEAD_DIM), lambda b, k: (b, k)),
                  pl.BlockSpec((1, n_cmp_pad, LANES), lambda b, k: (b * g + k, 0, 0)),
                  pl.BlockSpec((1, HEAD_DIM, n_cmp_pad), lambda b, k: (b * g + k, 0, 0)),
                  pl.BlockSpec((seq, LANES), lambda b, k: (b, ks0 + k)),
                  pl.BlockSpec((seq, LANES), lambda b, k: (b, kw0 + k)),
                  pl.BlockSpec((seq, LANES), lambda b, k: (b, vsw0 + k)),
                  pl.BlockSpec((seq, LANES), lambda b, k: (b, 0)),
                  _const_spec(overlap_t.shape)],
        out_specs=pl.BlockSpec((seq, hg * HEAD_DIM), lambda b, k: (b, k)),
        out_shape=jax.ShapeDtypeStruct((n, N_HEADS * HEAD_DIM), BF16),
        scratch_shapes=[pltpu.VMEM((n_tiles, HEAD_DIM + ONES_ROWS, KEY_TILE), BF16),
                        pltpu.VMEM((n_tiles, HEAD_DIM + ONES_ROWS, KEY_TILE), BF16),
                        pltpu.VMEM((LANES, seq), F32),
                        pltpu.VMEM((n_tiles, KEY_TILE, hg * Q_TILE), F32),
                        pltpu.VMEM((back + 1, KEY_TILE, hg * Q_TILE), F32)],
        compiler_params=pltpu.CompilerParams(
            dimension_semantics=("arbitrary", "arbitrary"),
            vmem_limit_bytes=VMEM_LIMIT),
        name="nsa_attention",
    )(proj, kcmp, vcmpt, proj, proj, proj, gates, overlap_t)


def _rope_tables(seq):
    inv_freq = 1.0 / (ROPE_THETA ** (jnp.arange(0, HEAD_DIM, 2, dtype=F32) / HEAD_DIM))
    ang = jnp.arange(seq, dtype=F32)[:, None] * inv_freq[None, :]
    reps = LANES // (HEAD_DIM // 2)
    cos = jnp.tile(jnp.cos(ang), (1, reps))
    sin = jnp.tile(jnp.sin(ang), (1, reps))
    first_half = (jnp.arange(LANES) % HEAD_DIM) < HEAD_DIM // 2
    return cos, jnp.where(first_half, -sin, 0.0), jnp.where(first_half, 0.0, sin)


def _nsa_weight_layout(w_in):
    qd = N_HEADS * HEAD_DIM
    kv = NSA_KV_COLS
    q, kc, vc, ks, vs, kw, vw, gt = jnp.split(
        w_in, np.cumsum([qd] + [kv] * 6).tolist(), axis=1)

    def dup(w):
        w = w.reshape(D_MODEL, NSA_KV_GROUPS, 1, HEAD_DIM)
        return jnp.broadcast_to(w, (D_MODEL, NSA_KV_GROUPS, 2, HEAD_DIM)).reshape(D_MODEL, -1)

    vsw = jnp.concatenate([vs.reshape(D_MODEL, NSA_KV_GROUPS, 1, HEAD_DIM),
                           vw.reshape(D_MODEL, NSA_KV_GROUPS, 1, HEAD_DIM)], axis=2)
    main = jnp.concatenate([q, kc, vc, dup(ks), dup(kw), vsw.reshape(D_MODEL, -1)], axis=1)
    per_group = 3 * NSA_HEADS_PER_GROUP
    gt = gt.reshape(D_MODEL, NSA_KV_GROUPS, per_group)
    gt = jnp.pad(gt, ((0, 0), (0, 0), (0, 16 - per_group))).reshape(D_MODEL, -1)
    gt = jnp.pad(gt, ((0, 0), (0, LANES - gt.shape[1])))
    return main.astype(BF16), gt.astype(BF16)


def _overlap_t(seq):
    n_cmp = (seq - NSA_CMP_BLOCK) // NSA_CMP_STRIDE + 1
    n_sel = seq // NSA_SLC_BLOCK
    c_start = np.arange(n_cmp) * NSA_CMP_STRIDE
    s_start = np.arange(n_sel) * NSA_SLC_BLOCK
    lo = np.maximum(c_start[:, None], s_start[None, :])
    hi = np.minimum(c_start[:, None] + NSA_CMP_BLOCK, s_start[None, :] + NSA_SLC_BLOCK)
    ov = np.clip(hi - lo, 0, None) / NSA_CMP_BLOCK
    ov = np.pad(ov, ((0, seq // NSA_CMP_STRIDE - n_cmp), (0, 0)))
    return jnp.asarray(ov.T, dtype=BF16)


_QK_ROPE = (True,) * 16 + (False,) * 8
_NO_ROPE = (False,) * 24
_NSA_ROPE = (True,) * 10 + (False,) * 2 + (True,) * 8 + (False,) * 4


def kernel(x, l0_ln_mix_pre, l0_w_in, l0_w_out, l0_ln_mix_post, l0_ln_mlp_pre, l0_w_up, l0_w_down, l0_ln_mlp_post, l1_ln_mix_pre, l1_w_in, l1_w_out, l1_ln_mix_post, l1_ln_mlp_pre, l1_w_up, l1_w_down, l1_ln_mlp_post, l2_ln_mix_pre, l2_w_in, l2_cmp_pos_k, l2_cmp_w1_k, l2_cmp_w2_k, l2_cmp_pos_v, l2_cmp_w1_v, l2_cmp_w2_v, l2_w_out, l2_ln_mix_post, l2_ln_mlp_pre, l2_w_up, l2_w_down, l2_ln_mlp_post, l3_ln_mix_pre, l3_w_in, l3_w_out, l3_ln_mix_post, l3_ln_mlp_pre, l3_w_up, l3_w_down, l3_ln_mlp_post):
    batch, seq, _ = x.shape
    assert seq % ROW_TILE == 0 and seq % KEY_TILE == 0 and Q_TILE == KEY_TILE == MOBA_BLOCK
    tables = _rope_tables(seq)
    x2 = x.reshape(batch * seq, D_MODEL)
    bf = lambda w: w.astype(BF16)

    def tail(o, x2, w_out, g_post, g_pre, w_up, w_down, g_mlp):
        return _layer_tail(o, x2, bf(w_out), g_post, g_pre, w_up, w_down, g_mlp)

    qkv = _norm_proj(x2, l0_ln_mix_pre, bf(l0_w_in), tables, _QK_ROPE, seq)
    x2 = tail(_moba_attention(qkv, batch, seq), x2, l0_w_out, l0_ln_mix_post, l0_ln_mlp_pre,
              l0_w_up, l0_w_down, l0_ln_mlp_post)

    qkv = _norm_proj(x2, l1_ln_mix_pre, bf(l1_w_in), tables, _NO_ROPE, seq)
    x2 = tail(_sb_attention(qkv, batch, seq), x2, l1_w_out, l1_ln_mix_post, l1_ln_mlp_pre,
              l1_w_up, l1_w_down, l1_ln_mlp_post)

    w_main, w_gate = _nsa_weight_layout(l2_w_in)
    proj, gates = _norm_proj(x2, l2_ln_mix_pre, w_main, tables, _NSA_ROPE, seq, wg=w_gate)
    qd = N_HEADS * HEAD_DIM
    rows = seq // NSA_CMP_STRIDE

    def row_groups(cols):
        a = cols.reshape(batch, seq, NSA_KV_GROUPS, HEAD_DIM).transpose(0, 2, 1, 3)
        return a.reshape(batch * NSA_KV_GROUPS, rows, NSA_CMP_STRIDE * HEAD_DIM)

    flat_pos = lambda p: p.reshape(2, NSA_CMP_STRIDE * HEAD_DIM)
    w2_dup = lambda w: bf(jnp.concatenate([w, w], axis=1))
    kcmp, vcmpt = _nsa_compress(
        row_groups(proj[:, qd:qd + NSA_KV_COLS]), row_groups(proj[:, qd + NSA_KV_COLS:qd + 2 * NSA_KV_COLS]),
        flat_pos(l2_cmp_pos_k), flat_pos(l2_cmp_pos_v), bf(l2_cmp_w1_k), bf(l2_cmp_w1_v),
        w2_dup(l2_cmp_w2_k), w2_dup(l2_cmp_w2_v))
    o = _nsa_attention(proj, gates, kcmp, vcmpt, _overlap_t(seq), batch, seq)
    x2 = tail(o, x2, l2_w_out, l2_ln_mix_post, l2_ln_mlp_pre, l2_w_up, l2_w_down, l2_ln_mlp_post)

    qkv = _norm_proj(x2, l3_ln_mix_pre, bf(l3_w_in), tables, _QK_ROPE, seq)
    x2 = tail(_moba_attention(qkv, batch, seq), x2, l3_w_out, l3_ln_mix_post, l3_ln_mlp_pre,
              l3_w_up, l3_w_down, l3_ln_mlp_post)
    return x2.reshape(batch, seq, D_MODEL)
```

```python
import functools
import math

import numpy as np
import jax
import jax.numpy as jnp
from jax import lax
from jax.experimental import pallas as pl
from jax.experimental.pallas import tpu as pltpu

D_MODEL = 1024
N_HEADS = 16
HEAD_DIM = 64
D_FF = 4 * D_MODEL
ROPE_THETA = 10000.0
NORM_EPS = 1e-6
NEG_INF = -1e30
POS_BIG = 1e30

MOBA_BLOCK = 256
MOBA_TOPK = 3

NSA_KV_GROUPS = 4
NSA_HEADS_PER_GROUP = N_HEADS // NSA_KV_GROUPS
NSA_CMP_BLOCK = 32
NSA_CMP_STRIDE = 16
NSA_CMP_HIDDEN = 256
NSA_SLC_BLOCK = 64
NSA_SLC_TOPN = 16
NSA_WINDOW = 512
NSA_KV_COLS = NSA_KV_GROUPS * HEAD_DIM

LANES = 128
SUBLANES = 8
KEY_TILE = 256
ONES_ROWS = 16
Q_TILE = 256
ROW_TILE = 512
FF_CHUNK = 512
PROJ_COLS = 3 * N_HEADS * HEAD_DIM
VMEM_LIMIT = 56 * 1024 * 1024
QUERY_SCALE = HEAD_DIM ** -0.5 * math.log2(math.e)
SB_UNDERFLOW_LOG2 = -150.0

F32 = jnp.float32
BF16 = jnp.bfloat16


def _nt(a, b):
    return lax.dot_general(a, b, (((1,), (1,)), ((), ())), preferred_element_type=F32)


def _nn(a, b):
    return jnp.dot(a, b, preferred_element_type=F32)


def _split_bf16(x):
    hi = x.astype(BF16)
    lo = (x - hi.astype(F32)).astype(BF16)
    return hi, lo


def _rms(x, g):
    ms = jnp.mean(x * x, axis=-1, keepdims=True)
    return x * lax.rsqrt(ms + NORM_EPS) * g


def _const_spec(shape):
    nd = len(shape)
    return pl.BlockSpec(shape, lambda *_: (0,) * nd, pipeline_mode=pl.Buffered(1))


def _proj_kernel(x_ref, g_ref, w_ref, cos_ref, sina_ref, sinb_ref, *rest, rope_slabs, has_gate):
    if has_gate:
        wg_ref, o_ref, og_ref = rest
    else:
        (o_ref,) = rest
    h = _rms(x_ref[...], g_ref[...]).astype(BF16)
    cos = cos_ref[...]
    sina = sina_ref[...]
    sinb = sinb_ref[...]
    n_slabs = len(rope_slabs)
    for c in range(n_slabs // 2):
        y = _nn(h, w_ref[:, c * 2 * LANES:(c + 1) * 2 * LANES])
        for half in range(2):
            s = 2 * c + half
            ys = y[:, half * LANES:(half + 1) * LANES]
            if s < N_HEADS * HEAD_DIM // LANES:
                ys = ys * QUERY_SCALE
            if rope_slabs[s]:
                ys = (ys * cos + pltpu.roll(ys, LANES - HEAD_DIM // 2, 1) * sina
                      + pltpu.roll(ys, HEAD_DIM // 2, 1) * sinb)
            o_ref[:, s * LANES:(s + 1) * LANES] = ys.astype(o_ref.dtype)
    if has_gate:
        og_ref[...] = _nn(h, wg_ref[...])


def _norm_proj(x2, g, w, tables, rope_slabs, seq, wg=None):
    n = x2.shape[0]
    cols = w.shape[1]
    pos_blocks = seq // ROW_TILE
    has_gate = wg is not None
    row = lambda i: (i, 0)
    tab = pl.BlockSpec((ROW_TILE, LANES), lambda i: (i % pos_blocks, 0))
    in_specs = [pl.BlockSpec((ROW_TILE, D_MODEL), row), _const_spec((1, D_MODEL)),
                _const_spec((D_MODEL, cols)), tab, tab, tab]
    args = [x2, g.reshape(1, D_MODEL), w, *tables]
    out_shape = [jax.ShapeDtypeStruct((n, cols), BF16)]
    out_specs = [pl.BlockSpec((ROW_TILE, cols), row)]
    if has_gate:
        in_specs.append(_const_spec((D_MODEL, LANES)))
        args.append(wg)
        out_shape.append(jax.ShapeDtypeStruct((n, LANES), F32))
        out_specs.append(pl.BlockSpec((ROW_TILE, LANES), row))
    outs = pl.pallas_call(
        functools.partial(_proj_kernel, rope_slabs=rope_slabs, has_gate=has_gate),
        grid=(n // ROW_TILE,),
        in_specs=in_specs,
        out_specs=out_specs,
        out_shape=out_shape,
        compiler_params=pltpu.CompilerParams(dimension_semantics=("arbitrary",),
                                             vmem_limit_bytes=VMEM_LIMIT),
        name="norm_proj",
    )(*args)
    return outs if has_gate else outs[0]


def _tail_kernel(o_ref, x_ref, wout_ref, gpost_ref, gpre_ref, wup_ref, wdown_ref, gmlp_ref, out_ref):
    m = _nn(o_ref[...], wout_ref[...])
    x1 = x_ref[...] + _rms(m, gpost_ref[...])
    h = _rms(x1, gpre_ref[...]).astype(BF16)
    acc = jnp.zeros((ROW_TILE, D_MODEL), F32)
    for j in range(D_FF // FF_CHUNK):
        u = _nn(h, wup_ref[:, j * FF_CHUNK:(j + 1) * FF_CHUNK].astype(BF16))
        a = jnp.square(jnp.maximum(u, 0.0)).astype(BF16)
        acc = acc + _nn(a, wdown_ref[j * FF_CHUNK:(j + 1) * FF_CHUNK, :].astype(BF16))
    out_ref[...] = x1 + _rms(acc, gmlp_ref[...])


def _layer_tail(o2, x2, wout, gpost, gpre, wup, wdown, gmlp):
    n = x2.shape[0]
    row = lambda i: (i, 0)
    vec = lambda v: v.reshape(1, D_MODEL)
    return pl.pallas_call(
        _tail_kernel,
        grid=(n // ROW_TILE,),
        in_specs=[pl.BlockSpec((ROW_TILE, N_HEADS * HEAD_DIM), row),
                  pl.BlockSpec((ROW_TILE, D_MODEL), row),
                  _const_spec((N_HEADS * HEAD_DIM, D_MODEL)), _const_spec((1, D_MODEL)),
                  _const_spec((1, D_MODEL)), _const_spec((D_MODEL, D_FF)),
                  _const_spec((D_FF, D_MODEL)), _const_spec((1, D_MODEL))],
        out_specs=pl.BlockSpec((ROW_TILE, D_MODEL), row),
        out_shape=jax.ShapeDtypeStruct((n, D_MODEL), F32),
        compiler_params=pltpu.CompilerParams(dimension_semantics=("arbitrary",),
                                             vmem_limit_bytes=VMEM_LIMIT),
        name="layer_tail",
    )(o2, x2, wout, vec(gpost), vec(gpre), wup, wdown, vec(gmlp))


def _pair_queries(q):
    lane = lax.broadcasted_iota(jnp.int32, q.shape, 1)
    zero = jnp.zeros_like(q)
    return jnp.concatenate([jnp.where(lane < HEAD_DIM, q, zero),
                            jnp.where(lane >= HEAD_DIM, q, zero)], axis=0)


def _transpose_value_tiles(v_ref, vt_ref, n_tiles):
    for n in range(n_tiles):
        blk = v_ref[n * KEY_TILE:(n + 1) * KEY_TILE, :].astype(F32)
        vt_ref[n] = blk.T.astype(BF16)


def _fold_rows(x, op):
    r, w = x.shape
    return op(x.reshape(r // SUBLANES, SUBLANES, w), axis=0)


def _pair_pv(vt, p):
    t = p.shape[1] // 2
    return _nn(vt[:HEAD_DIM, :], p[:, :t]), _nn(vt[HEAD_DIM:, :], p[:, t:])


def _pair_output(acc_a, acc_b, inv_l=None):
    if inv_l is not None:
        t = acc_a.shape[1]
        acc_a = acc_a * inv_l[:, :t]
        acc_b = acc_b * inv_l[:, t:]
    return jnp.concatenate([acc_a, acc_b], axis=0).T


def _moba_kernel(q_ref, k_ref, v_ref, o_ref, kmean_ref, vt_ref, s_ref, *, n_blocks):
    for n in range(n_blocks):
        blk = k_ref[n * MOBA_BLOCK:(n + 1) * MOBA_BLOCK, :].astype(F32)
        kmean_ref[n:n + 1, :] = jnp.mean(blk, axis=0, keepdims=True)
    _transpose_value_tiles(v_ref, vt_ref, n_blocks)
    km_hi, km_lo = _split_bf16(kmean_ref[...])
    kpos = lax.broadcasted_iota(jnp.int32, (KEY_TILE, Q_TILE), 0)
    tpos = lax.broadcasted_iota(jnp.int32, (KEY_TILE, Q_TILE), 1)
    causal = kpos <= tpos

    def begin(i):
        qcat = _pair_queries(q_ref[i * Q_TILE:(i + 1) * Q_TILE, :])
        chosen = [None] * i
        if i > MOBA_TOPK:
            gate = _nt(km_hi, qcat) + _nt(km_lo, qcat)
            blk_id = lax.broadcasted_iota(jnp.int32, gate.shape, 0)
            g = jnp.where(blk_id < i, gate, NEG_INF)
            sel = jnp.zeros(gate.shape, F32)
            for _ in range(MOBA_TOPK):
                mx = jnp.max(g, axis=0, keepdims=True)
                first = jnp.min(jnp.where(g == mx, blk_id, n_blocks), axis=0, keepdims=True)
                hit = blk_id == first
                sel = jnp.where(hit, 1.0, sel)
                g = jnp.where(hit, -jnp.inf, g)
            chosen = [sel[n:n + 1, :] > 0.5 for n in range(i)]
        heads = [dict(m8=None, l8=jnp.zeros((SUBLANES, Q_TILE), F32), acc=jnp.zeros((HEAD_DIM, Q_TILE), F32))
                 for _ in range(2)]
        return dict(i=i, qcat=qcat, chosen=chosen, heads=heads)

    def score_tile(st, n, h):
        i = st["i"]
        hd = st["heads"][h]
        lanes = slice(h * Q_TILE, (h + 1) * Q_TILE)
        s = _nt(k_ref[n * KEY_TILE:(n + 1) * KEY_TILE, :], st["qcat"][lanes, :])
        if n == i:
            s = jnp.where(causal, s, NEG_INF)
        s_ref[i % 2, n, :, lanes] = s
        bm = _fold_rows(s, jnp.max)
        if n < i and st["chosen"][n] is not None:
            bm = jnp.where(st["chosen"][n][:, lanes], bm, NEG_INF)
        hd["m8"] = bm if hd["m8"] is None else jnp.maximum(hd["m8"], bm)

    def weight_tile(st, n, h):
        i = st["i"]
        hd = st["heads"][h]
        lanes = slice(h * Q_TILE, (h + 1) * Q_TILE)
        keep = None if (n == i or st["chosen"][n] is None) else st["chosen"][n][:, lanes]
        shift = hd["m"] if keep is None else jnp.where(keep, hd["m"], POS_BIG)
        p = jnp.exp2(s_ref[i % 2, n, :, lanes] - shift)
        hd["l8"] = hd["l8"] + _fold_rows(p, jnp.sum)
        hd["acc"] = hd["acc"] + _nn(vt_ref[n, h * HEAD_DIM:(h + 1) * HEAD_DIM, :], p.astype(BF16))

    cur = begin(0)
    for h in range(2):
        score_tile(cur, 0, h)
    for i in range(n_blocks):
        for hd in cur["heads"]:
            hd["m"] = jnp.max(hd["m8"], axis=0, keepdims=True)
        nxt = begin(i + 1) if i + 1 < n_blocks else None
        for n in range(i + 2):
            for h in range(2):
                if nxt is not None:
                    score_tile(nxt, n, h)
                if n <= i:
                    weight_tile(cur, n, h)
        acc = [hd["acc"] * (1.0 / jnp.sum(hd["l8"], axis=0, keepdims=True)) for hd in cur["heads"]]
        o_ref[i * Q_TILE:(i + 1) * Q_TILE, :] = _pair_output(acc[0], acc[1]).astype(o_ref.dtype)
        cur = nxt


def _moba_attention(qkv, batch, seq):
    n = qkv.shape[0]
    n_blocks = seq // MOBA_BLOCK
    pairs = N_HEADS // 2
    return pl.pallas_call(
        functools.partial(_moba_kernel, n_blocks=n_blocks),
        grid=(batch, pairs),
        in_specs=[pl.BlockSpec((seq, LANES), lambda b, p: (b, p)),
                  pl.BlockSpec((seq, LANES), lambda b, p: (b, pairs + p)),
                  pl.BlockSpec((seq, LANES), lambda b, p: (b, 2 * pairs + p))],
        out_specs=pl.BlockSpec((seq, LANES), lambda b, p: (b, p)),
        out_shape=jax.ShapeDtypeStruct((n, N_HEADS * HEAD_DIM), BF16),
        scratch_shapes=[pltpu.VMEM((n_blocks, LANES), F32),
                        pltpu.VMEM((n_blocks, LANES, KEY_TILE), BF16),
                        pltpu.VMEM((2, n_blocks, KEY_TILE, 2 * Q_TILE), F32)],
        compiler_params=pltpu.CompilerParams(
            dimension_semantics=("arbitrary", "arbitrary"),
            vmem_limit_bytes=VMEM_LIMIT),
        name="moba_attention",
    )(qkv, qkv, qkv)


def _log2_sigmoid_pair(z):
    log_beta = jnp.minimum(z, 0.0) - jnp.log2(1.0 + jnp.exp2(-jnp.abs(z)))
    return log_beta, log_beta - z


def _sb_kernel(q_ref, k_ref, v_ref, o_ref, vt_ref, acc_ref, carry_ref, *, n_tiles):
    w2 = 2 * Q_TILE
    near = 3
    _transpose_value_tiles(v_ref, vt_ref, n_tiles)
    r = lax.broadcasted_iota(jnp.int32, (KEY_TILE, KEY_TILE), 0)
    c = lax.broadcasted_iota(jnp.int32, (KEY_TILE, KEY_TILE), 1)
    later = jnp.where(c > r, 1.0, 0.0).astype(BF16)
    kpos = lax.broadcasted_iota(jnp.int32, (KEY_TILE, w2), 0)
    tpos = lax.broadcasted_iota(jnp.int32, (KEY_TILE, w2), 1) % Q_TILE
    strict = kpos < tpos

    def scores(state, t):
        st = state[t["i"]]
        t["z"] = _nt(k_ref[t["n"] * KEY_TILE:(t["n"] + 1) * KEY_TILE, :], st["qcat"])

    def log_terms(state, t):
        st = state[t["i"]]
        log_beta, log_1m = _log2_sigmoid_pair(t.pop("z"))
        if t["n"] == t["i"]:
            log_1m = jnp.where(strict, log_1m, 0.0)
        t["carry"] = st["carry"]
        if t["n"] > 0:
            tile_sum = jnp.sum(log_1m, axis=0, keepdims=True)
            st["carry"] = tile_sum if st["carry"] is None else st["carry"] + tile_sum
        t["log_beta"] = log_beta
        t["suffix"] = _nn(later, log_1m.astype(BF16))

    def weights(state, t):
        st = state[t["i"]]
        expo = t.pop("log_beta") + t.pop("suffix")
        if t["carry"] is not None:
            expo = expo + t["carry"]
        wgt = jnp.exp2(expo)
        if t["n"] == t["i"]:
            wgt = jnp.where(strict, wgt, 0.0)
        pv_a, pv_b = _pair_pv(vt_ref[t["n"]], wgt.astype(BF16))
        st["acc_a"] = st["acc_a"] + pv_a
        st["acc_b"] = st["acc_b"] + pv_b

    def run(state, tiles):
        stages = (scores, log_terms, weights)
        for step in range(len(tiles) + len(stages) - 1):
            for lag, stage in enumerate(stages):
                if 0 <= step - lag < len(tiles):
                    stage(state, tiles[step - lag])

    def query_operand(i):
        return _pair_queries(q_ref[i * Q_TILE:(i + 1) * Q_TILE, :])

    def write_out(i, st):
        o_ref[i * Q_TILE:(i + 1) * Q_TILE, :] = _pair_output(st["acc_a"], st["acc_b"]).astype(o_ref.dtype)

    zeros = jnp.zeros((HEAD_DIM, Q_TILE), F32)
    state = {i: dict(qcat=query_operand(i), carry=None, acc_a=zeros, acc_b=zeros) for i in range(n_tiles)}
    run(state, [dict(i=i, n=n) for i in range(n_tiles) for n in range(i, max(i - near, -1), -1)])
    for i in range(n_tiles):
        write_out(i, state[i])
        if i >= near:
            acc_ref[i] = jnp.concatenate([state[i]["acc_a"], state[i]["acc_b"]], axis=0)
            carry_ref[i] = state[i]["carry"]

    live = {i: jnp.max(state[i]["carry"]) > SB_UNDERFLOW_LOG2 for i in range(near, n_tiles)}
    for i in range(near, n_tiles):
        @pl.when(live[i])
        def _(i=i):
            acc = acc_ref[i]
            far = {i: dict(qcat=query_operand(i), carry=carry_ref[i],
                           acc_a=acc[:HEAD_DIM, :], acc_b=acc[HEAD_DIM:, :])}
            run(far, [dict(i=i, n=n) for n in range(i - near, -1, -1)])
            write_out(i, far[i])


def _sb_attention(qkv, batch, seq):
    n = qkv.shape[0]
    n_tiles = seq // KEY_TILE
    pairs = N_HEADS // 2
    return pl.pallas_call(
        functools.partial(_sb_kernel, n_tiles=n_tiles),
        grid=(batch, pairs),
        in_specs=[pl.BlockSpec((seq, LANES), lambda b, p: (b, p)),
                  pl.BlockSpec((seq, LANES), lambda b, p: (b, pairs + p)),
                  pl.BlockSpec((seq, LANES), lambda b, p: (b, 2 * pairs + p))],
        out_specs=pl.BlockSpec((seq, LANES), lambda b, p: (b, p)),
        out_shape=jax.ShapeDtypeStruct((n, N_HEADS * HEAD_DIM), BF16),
        scratch_shapes=[pltpu.VMEM((n_tiles, LANES, KEY_TILE), BF16),
                        pltpu.VMEM((n_tiles, LANES, Q_TILE), F32),
                        pltpu.VMEM((n_tiles, 1, 2 * Q_TILE), F32)],
        compiler_params=pltpu.CompilerParams(
            dimension_semantics=("arbitrary", "arbitrary"),
            vmem_limit_bytes=VMEM_LIMIT),
        name="stick_breaking_attention",
    )(qkv, qkv, qkv)


def _nsa_compress_kernel(ak_ref, av_ref, posk_ref, posv_ref, w1k_ref, w1v_ref, w2k_ref, w2v_ref,
                         kc_ref, vct_ref):
    half = NSA_CMP_STRIDE * HEAD_DIM
    n_rows = ak_ref.shape[1]

    def compress(a_all, g, pos_ref, w1_ref, w2_ref):
        a = jnp.concatenate(
            [a_all[:, l * NSA_KV_COLS + g * HEAD_DIM:l * NSA_KV_COLS + (g + 1) * HEAD_DIM]
             for l in range(NSA_CMP_STRIDE)], axis=1)
        first = (a + pos_ref[0:1, :]).astype(BF16)
        second = (a + pos_ref[1:2, :]).astype(BF16)
        pre = _nn(first, w1_ref[:half, :]) + pltpu.roll(_nn(second, w1_ref[half:, :]), n_rows - 1, 0)
        return _nn(jax.nn.gelu(pre).astype(BF16), w2_ref[...])

    ak = ak_ref[0].astype(F32)
    av = av_ref[0].astype(F32)
    for g in range(NSA_KV_GROUPS):
        kc_ref[g] = compress(ak, g, posk_ref, w1k_ref, w2k_ref).astype(BF16)
        vct_ref[g] = compress(av, g, posv_ref, w1v_ref, w2v_ref).T[:HEAD_DIM, :].astype(BF16)


def _nsa_compress(ak, av, posk, posv, w1k, w1v, w2k, w2v):
    batch, rows, feat = ak.shape
    g = NSA_KV_GROUPS
    blk = pl.BlockSpec((1, rows, feat), lambda i: (i, 0, 0))
    return pl.pallas_call(
        _nsa_compress_kernel,
        grid=(batch,),
        in_specs=[blk, blk, _const_spec(posk.shape), _const_spec(posv.shape),
                  _const_spec(w1k.shape), _const_spec(w1v.shape),
                  _const_spec(w2k.shape), _const_spec(w2v.shape)],
        out_specs=[pl.BlockSpec((g, rows, LANES), lambda i: (i, 0, 0)),
                   pl.BlockSpec((g, HEAD_DIM, rows), lambda i: (i, 0, 0))],
        out_shape=[jax.ShapeDtypeStruct((batch * g, rows, LANES), BF16),
                   jax.ShapeDtypeStruct((batch * g, HEAD_DIM, rows), BF16)],
        compiler_params=pltpu.CompilerParams(dimension_semantics=("arbitrary",),
                                             vmem_limit_bytes=VMEM_LIMIT),
        name="nsa_compress",
    )(ak, av, posk, posv, w1k, w1v, w2k, w2v)


def _nsa_kernel(q_ref, kcmp_ref, vcmpt_ref, ks_ref, kw_ref, vsw_ref, gate_ref, ovl_ref, o_ref,
                vst_ref, vwt_ref, gt_ref, ssel_ref, swin_ref, *, n_tiles, n_sel):
    grp = pl.program_id(1)
    hg = NSA_HEADS_PER_GROUP
    t = Q_TILE
    wq = hg * t
    sub = KEY_TILE // NSA_SLC_BLOCK
    back = NSA_WINDOW // KEY_TILE
    n_top = min(NSA_SLC_TOPN, n_sel)

    for n in range(n_tiles):
        rows = slice(n * KEY_TILE, (n + 1) * KEY_TILE)
        blk = vsw_ref[rows, :].astype(F32).T.astype(BF16)
        for vt_ref, part in ((vst_ref, blk[:HEAD_DIM, :]), (vwt_ref, blk[HEAD_DIM:, :])):
            vt_ref[n, :HEAD_DIM, :] = part
            vt_ref[n, HEAD_DIM:, :] = jnp.ones((ONES_ROWS, KEY_TILE), BF16)
        gt_ref[:, rows] = jax.nn.sigmoid(gate_ref[rows, :]).T
    kloc = lax.broadcasted_iota(jnp.int32, (KEY_TILE, wq), 0)
    tloc = lax.broadcasted_iota(jnp.int32, (KEY_TILE, wq), 1) % t
    causal = kloc <= tloc
    window_tail = kloc > tloc

    def begin(i):
        rows = slice(i * t, (i + 1) * t)
        qcat = jnp.concatenate([_pair_queries(q_ref[rows, :LANES]),
                                _pair_queries(q_ref[rows, LANES:])], axis=0)
        tq = i * t + lax.broadcasted_iota(jnp.int32, (1, wq), 1) % t

        n_cmp_pad = kcmp_ref.shape[1]
        reach = min(n_cmp_pad, (i + 1) * t // NSA_CMP_STRIDE)
        s = _nt(kcmp_ref[0, :reach, :], qcat)
        cmp_end = lax.broadcasted_iota(jnp.int32, s.shape, 0) * NSA_CMP_STRIDE + (NSA_CMP_BLOCK - 1)
        valid = cmp_end <= tq
        m = jnp.max(jnp.where(valid, s, NEG_INF), axis=0, keepdims=True)
        e = jnp.where(valid, jnp.exp2(s - m), 0.0)
        l = jnp.sum(e, axis=0, keepdims=True)
        p_cmp = e * jnp.where(l > 0.0, 1.0 / l, 0.0)
        if reach < n_cmp_pad:
            p_cmp = jnp.concatenate([p_cmp, jnp.zeros((n_cmp_pad - reach, wq), F32)], axis=0)
        o_cmp = _nn(vcmpt_ref[0], p_cmp.astype(BF16))

        bias = None
        if sub * (i + 1) > n_top:
            p_grp = p_cmp[:, :t]
            for h in range(1, hg):
                p_grp = p_grp + p_cmp[:, h * t:(h + 1) * t]
            p_hi, p_lo = _split_bf16(p_grp)
            imp = _nn(ovl_ref[...], p_hi) + _nn(ovl_ref[...], p_lo)
            j = lax.broadcasted_iota(jnp.int32, imp.shape, 0)
            q_blk = (i * t + lax.broadcasted_iota(jnp.int32, imp.shape, 1)) // NSA_SLC_BLOCK
            forced = jnp.logical_or(j == 0, jnp.logical_or(j == q_blk, j == q_blk - 1))
            score = jnp.where(j <= q_blk, jnp.where(forced, POS_BIG, imp), NEG_INF)
            sel = jnp.zeros(imp.shape, F32)
            for _ in range(n_top):
                mx = jnp.max(score, axis=0, keepdims=True)
                first = jnp.min(jnp.where(score == mx, j, n_sel), axis=0, keepdims=True)
                hit = j == first
                sel = jnp.where(jnp.logical_and(hit, mx > 0.5 * NEG_INF), 1.0, sel)
                score = jnp.where(hit, -jnp.inf, score)
            bias = jnp.concatenate([jnp.where(sel > 0.5, 0.0, NEG_INF)] * hg, axis=1)
        zeros_acc = jnp.zeros((HEAD_DIM + ONES_ROWS, wq), F32)
        return dict(i=i, qcat=qcat, o_cmp=o_cmp, bias=bias,
                    sel=dict(m8=None, acc=zeros_acc), win=dict(m8=None, acc=zeros_acc))

    def note_max(br, s):
        bm = _fold_rows(s, jnp.max)
        br["m8"] = bm if br["m8"] is None else jnp.maximum(br["m8"], bm)

    def sel_scores(st, n):
        s = _nt(ks_ref[n * KEY_TILE:(n + 1) * KEY_TILE, :], st["qcat"])
        if st["bias"] is not None:
            s = jnp.concatenate(
                [s[jj * NSA_SLC_BLOCK:(jj + 1) * NSA_SLC_BLOCK, :]
                 + st["bias"][n * sub + jj:n * sub + jj + 1, :] for jj in range(sub)], axis=0)
        if n == st["i"]:
            s = jnp.where(causal, s, NEG_INF)
        ssel_ref[n] = s
        note_max(st["sel"], s)

    def win_scores(st, n):
        s = _nt(kw_ref[n * KEY_TILE:(n + 1) * KEY_TILE, :], st["qcat"])
        if n == st["i"]:
            s = jnp.where(causal, s, NEG_INF)
        elif n == st["i"] - back:
            s = jnp.where(window_tail, s, NEG_INF)
        swin_ref[st["i"] - n] = s
        note_max(st["win"], s)

    def weights(br, s_view, v_view):
        p = jnp.exp2(s_view[...] - br["m"])
        br["acc"] = br["acc"] + _nn(v_view[...], p.astype(BF16))

    def finish(st):
        i = st["i"]
        o_slc, o_win = [br["acc"][:HEAD_DIM, :] * (1.0 / br["acc"][HEAD_DIM:HEAD_DIM + 1, :])
                        for br in (st["sel"], st["win"])]
        gates = gt_ref[pl.ds(pl.multiple_of(grp * 16, 16), 16), i * t:(i + 1) * t]
        heads = []
        for h in range(hg):
            cols = slice(h * t, (h + 1) * t)
            heads.append(gates[3 * h:3 * h + 1, :] * st["o_cmp"][:, cols]
                         + gates[3 * h + 1:3 * h + 2, :] * o_slc[:, cols]
                         + gates[3 * h + 2:3 * h + 3, :] * o_win[:, cols])
        for pair in range(hg // 2):
            both = jnp.concatenate([heads[2 * pair], heads[2 * pair + 1]], axis=0)
            o_ref[i * t:(i + 1) * t, pair * LANES:(pair + 1) * LANES] = both.T.astype(o_ref.dtype)

    def interleave(first, second):
        for k in range(max(len(first), len(second))):
            for steps in (first, second):
                if k < len(steps):
                    steps[k]()

    def win_tiles(i):
        return [n for n in range(i, i - back - 1, -1) if n >= 0]

    cur = begin(0)
    sel_scores(cur, 0)
    for i in range(n_tiles):
        cur["sel"]["m"] = jnp.max(cur["sel"]["m8"], axis=0, keepdims=True)
        interleave([functools.partial(win_scores, cur, n) for n in win_tiles(i)],
                   [functools.partial(weights, cur["sel"], ssel_ref.at[n], vst_ref.at[n])
                    for n in range(i + 1)])
        cur["win"]["m"] = jnp.max(cur["win"]["m8"], axis=0, keepdims=True)
        nxt = begin(i + 1) if i + 1 < n_tiles else None
        interleave([functools.partial(sel_scores, nxt, n) for n in range(i + 2)] if nxt else [],
                   [functools.partial(weights, cur["win"], swin_ref.at[i - n], vwt_ref.at[n])
                    for n in win_tiles(i)])
        finish(cur)
        cur = nxt


def _nsa_attention(proj, gates, kcmp, vcmpt, overlap_t, batch, seq):
    n = proj.shape[0]
    n_tiles = seq // KEY_TILE
    n_sel = seq // NSA_SLC_BLOCK
    g = NSA_KV_GROUPS
    hg = NSA_HEADS_PER_GROUP
    n_cmp_pad = kcmp.shape[1]
    back = NSA_WINDOW // KEY_TILE
    ks0, kw0, vsw0 = 12, 16, 20
    return pl.pallas_call(
        functools.partial(_nsa_kernel, n_tiles=n_tiles, n_sel=n_sel),
        grid=(batch, g),
        in_specs=[pl.BlockSpec((seq, hg * HEAD_DIM), lambda b, k: (b, k)),
                  pl.BlockSpec((1, n_cmp_pad, LANES), lambda b, k: (b * g + k, 0, 0)),
                  pl.BlockSpec((1, HEAD_DIM, n_cmp_pad), lambda b, k: (b * g + k, 0, 0)),
                  pl.BlockSpec((seq, LANES), lambda b, k: (b, ks0 + k)),
                  pl.BlockSpec((seq, LANES), lambda b, k: (b, kw0 + k)),
                  pl.BlockSpec((seq, LANES), lambda b, k: (b, vsw0 + k)),
                  pl.BlockSpec((seq, LANES), lambda b, k: (b, 0)),
                  _const_spec(overlap_t.shape)],
        out_specs=pl.BlockSpec((seq, hg * HEAD_DIM), lambda b, k: (b, k)),
        out_shape=jax.ShapeDtypeStruct((n, N_HEADS * HEAD_DIM), BF16),
        scratch_shapes=[pltpu.VMEM((n_tiles, HEAD_DIM + ONES_ROWS, KEY_TILE), BF16),
                        pltpu.VMEM((n_tiles, HEAD_DIM + ONES_ROWS, KEY_TILE), BF16),
                        pltpu.VMEM((LANES, seq), F32),
                        pltpu.VMEM((n_tiles, KEY_TILE, hg * Q_TILE), F32),
                        pltpu.VMEM((back + 1, KEY_TILE, hg * Q_TILE), F32)],
        compiler_params=pltpu.CompilerParams(
            dimension_semantics=("arbitrary", "arbitrary"),
            vmem_limit_bytes=VMEM_LIMIT),
        name="nsa_attention",
    )(proj, kcmp, vcmpt, proj, proj, proj, gates, overlap_t)


def _rope_tables(seq):
    inv_freq = 1.0 / (ROPE_THETA ** (jnp.arange(0, HEAD_DIM, 2, dtype=F32) / HEAD_DIM))
    ang = jnp.arange(seq, dtype=F32)[:, None] * inv_freq[None, :]
    reps = LANES // (HEAD_DIM // 2)
    cos = jnp.tile(jnp.cos(ang), (1, reps))
    sin = jnp.tile(jnp.sin(ang), (1, reps))
    first_half = (jnp.arange(LANES) % HEAD_DIM) < HEAD_DIM // 2
    return cos, jnp.where(first_half, -sin, 0.0), jnp.where(first_half, 0.0, sin)


def _nsa_weight_layout(w_in):
    qd = N_HEADS * HEAD_DIM
    kv = NSA_KV_COLS
    q, kc, vc, ks, vs, kw, vw, gt = jnp.split(
        w_in, np.cumsum([qd] + [kv] * 6).tolist(), axis=1)

    def dup(w):
        w = w.reshape(D_MODEL, NSA_KV_GROUPS, 1, HEAD_DIM)
        return jnp.broadcast_to(w, (D_MODEL, NSA_KV_GROUPS, 2, HEAD_DIM)).reshape(D_MODEL, -1)

    vsw = jnp.concatenate([vs.reshape(D_MODEL, NSA_KV_GROUPS, 1, HEAD_DIM),
                           vw.reshape(D_MODEL, NSA_KV_GROUPS, 1, HEAD_DIM)], axis=2)
    main = jnp.concatenate([q, kc, vc, dup(ks), dup(kw), vsw.reshape(D_MODEL, -1)], axis=1)
    per_group = 3 * NSA_HEADS_PER_GROUP
    gt = gt.reshape(D_MODEL, NSA_KV_GROUPS, per_group)
    gt = jnp.pad(gt, ((0, 0), (0, 0), (0, 16 - per_group))).reshape(D_MODEL, -1)
    gt = jnp.pad(gt, ((0, 0), (0, LANES - gt.shape[1])))
    return main.astype(BF16), gt.astype(BF16)


def _overlap_t(seq):
    n_cmp = (seq - NSA_CMP_BLOCK) // NSA_CMP_STRIDE + 1
    n_sel = seq // NSA_SLC_BLOCK
    c_start = np.arange(n_cmp) * NSA_CMP_STRIDE
    s_start = np.arange(n_sel) * NSA_SLC_BLOCK
    lo = np.maximum(c_start[:, None], s_start[None, :])
    hi = np.minimum(c_start[:, None] + NSA_CMP_BLOCK, s_start[None, :] + NSA_SLC_BLOCK)
    ov = np.clip(hi - lo, 0, None) / NSA_CMP_BLOCK
    ov = np.pad(ov, ((0, seq // NSA_CMP_STRIDE - n_cmp), (0, 0)))
    return jnp.asarray(ov.T, dtype=BF16)


_QK_ROPE = (True,) * 16 + (False,) * 8
_NO_ROPE = (False,) * 24
_NSA_ROPE = (True,) * 10 + (False,) * 2 + (True,) * 8 + (False,) * 4


def kernel(x, l0_ln_mix_pre, l0_w_in, l0_w_out, l0_ln_mix_post, l0_ln_mlp_pre, l0_w_up, l0_w_down, l0_ln_mlp_post, l1_ln_mix_pre, l1_w_in, l1_w_out, l1_ln_mix_post, l1_ln_mlp_pre, l1_w_up, l1_w_down, l1_ln_mlp_post, l2_ln_mix_pre, l2_w_in, l2_cmp_pos_k, l2_cmp_w1_k, l2_cmp_w2_k, l2_cmp_pos_v, l2_cmp_w1_v, l2_cmp_w2_v, l2_w_out, l2_ln_mix_post, l2_ln_mlp_pre, l2_w_up, l2_w_down, l2_ln_mlp_post, l3_ln_mix_pre, l3_w_in, l3_w_out, l3_ln_mix_post, l3_ln_mlp_pre, l3_w_up, l3_w_down, l3_ln_mlp_post):
    batch, seq, _ = x.shape
    assert seq % ROW_TILE == 0 and seq % KEY_TILE == 0 and Q_TILE == KEY_TILE == MOBA_BLOCK
    tables = _rope_tables(seq)
    x2 = x.reshape(batch * seq, D_MODEL)
    bf = lambda w: w.astype(BF16)

    def tail(o, x2, w_out, g_post, g_pre, w_up, w_down, g_mlp):
        return _layer_tail(o, x2, bf(w_out), g_post, g_pre, w_up, w_down, g_mlp)

    qkv = _norm_proj(x2, l0_ln_mix_pre, bf(l0_w_in), tables, _QK_ROPE, seq)
    x2 = tail(_moba_attention(qkv, batch, seq), x2, l0_w_out, l0_ln_mix_post, l0_ln_mlp_pre,
              l0_w_up, l0_w_down, l0_ln_mlp_post)

    qkv = _norm_proj(x2, l1_ln_mix_pre, bf(l1_w_in), tables, _NO_ROPE, seq)
    x2 = tail(_sb_attention(qkv, batch, seq), x2, l1_w_out, l1_ln_mix_post, l1_ln_mlp_pre,
              l1_w_up, l1_w_down, l1_ln_mlp_post)

    w_main, w_gate = _nsa_weight_layout(l2_w_in)
    proj, gates = _norm_proj(x2, l2_ln_mix_pre, w_main, tables, _NSA_ROPE, seq, wg=w_gate)
    qd = N_HEADS * HEAD_DIM
    rows = seq // NSA_CMP_STRIDE

    def row_groups(cols):
        return cols.reshape(batch, rows, NSA_CMP_STRIDE * NSA_KV_COLS)

    flat_pos = lambda p: p.reshape(2, NSA_CMP_STRIDE * HEAD_DIM)
    w2_dup = lambda w: bf(jnp.concatenate([w, w], axis=1))
    kcmp, vcmpt = _nsa_compress(
        row_groups(proj[:, qd:qd + NSA_KV_COLS]), row_groups(proj[:, qd + NSA_KV_COLS:qd + 2 * NSA_KV_COLS]),
        flat_pos(l2_cmp_pos_k), flat_pos(l2_cmp_pos_v), bf(l2_cmp_w1_k), bf(l2_cmp_w1_v),
        w2_dup(l2_cmp_w2_k), w2_dup(l2_cmp_w2_v))
    o = _nsa_attention(proj, gates, kcmp, vcmpt, _overlap_t(seq), batch, seq)
    x2 = tail(o, x2, l2_w_out, l2_ln_mix_post, l2_ln_mlp_pre, l2_w_up, l2_w_down, l2_ln_mlp_post)

    qkv = _norm_proj(x2, l3_ln_mix_pre, bf(l3_w_in), tables, _QK_ROPE, seq)
    x2 = tail(_moba_attention(qkv, batch, seq), x2, l3_w_out, l3_ln_mix_post, l3_ln_mlp_pre,
              l3_w_up, l3_w_down, l3_ln_mlp_post)
    return x2.reshape(batch, seq, D_MODEL)
```

```python
import functools
import math

import numpy as np
import jax
import jax.numpy as jnp
from jax import lax
from jax.experimental import pallas as pl
from jax.experimental.pallas import tpu as pltpu

D_MODEL = 1024
N_HEADS = 16
HEAD_DIM = 64
D_FF = 4 * D_MODEL
ROPE_THETA = 10000.0
NORM_EPS = 1e-6
NEG_INF = -1e30
POS_BIG = 1e30

MOBA_BLOCK = 256
MOBA_TOPK = 3

NSA_KV_GROUPS = 4
NSA_HEADS_PER_GROUP = N_HEADS // NSA_KV_GROUPS
NSA_CMP_BLOCK = 32
NSA_CMP_STRIDE = 16
NSA_CMP_HIDDEN = 256
NSA_SLC_BLOCK = 64
NSA_SLC_TOPN = 16
NSA_WINDOW = 512
NSA_KV_COLS = NSA_KV_GROUPS * HEAD_DIM

LANES = 128
SUBLANES = 8
KEY_TILE = 256
ONES_ROWS = 16
NSA_F32_SLABS = (8, 9, 10, 11)
Q_TILE = 256
ROW_TILE = 512
FF_CHUNK = 512
PROJ_COLS = 3 * N_HEADS * HEAD_DIM
VMEM_LIMIT = 56 * 1024 * 1024
QUERY_SCALE = HEAD_DIM ** -0.5 * math.log2(math.e)
SB_UNDERFLOW_LOG2 = -150.0

F32 = jnp.float32
BF16 = jnp.bfloat16


def _nt(a, b):
    return lax.dot_general(a, b, (((1,), (1,)), ((), ())), preferred_element_type=F32)


def _nn(a, b):
    return jnp.dot(a, b, preferred_element_type=F32)


def _split_bf16(x):
    hi = x.astype(BF16)
    lo = (x - hi.astype(F32)).astype(BF16)
    return hi, lo


def _rms(x, g):
    ms = jnp.mean(x * x, axis=-1, keepdims=True)
    return x * lax.rsqrt(ms + NORM_EPS) * g


def _const_spec(shape):
    nd = len(shape)
    return pl.BlockSpec(shape, lambda *_: (0,) * nd, pipeline_mode=pl.Buffered(1))


def _proj_kernel(x_ref, g_ref, w_ref, cos_ref, sina_ref, sinb_ref, *rest, rope_slabs, has_gate):
    if has_gate:
        wg_ref, o_ref, og_ref = rest
    else:
        (o_ref,) = rest
    h = _rms(x_ref[...], g_ref[...]).astype(BF16)
    cos = cos_ref[...]
    sina = sina_ref[...]
    sinb = sinb_ref[...]
    n_slabs = len(rope_slabs)
    for c in range(n_slabs // 2):
        y = _nn(h, w_ref[:, c * 2 * LANES:(c + 1) * 2 * LANES])
        for half in range(2):
            s = 2 * c + half
            ys = y[:, half * LANES:(half + 1) * LANES]
            if s < N_HEADS * HEAD_DIM // LANES:
                ys = ys * QUERY_SCALE
            if rope_slabs[s]:
                ys = (ys * cos + pltpu.roll(ys, LANES - HEAD_DIM // 2, 1) * sina
                      + pltpu.roll(ys, HEAD_DIM // 2, 1) * sinb)
            o_ref[:, s * LANES:(s + 1) * LANES] = ys.astype(o_ref.dtype)
            if has_gate and s in NSA_F32_SLABS:
                k = NSA_F32_SLABS.index(s)
                og_ref[:, k * LANES:(k + 1) * LANES] = ys
    if has_gate:
        og_ref[:, len(NSA_F32_SLABS) * LANES:] = _nn(h, wg_ref[...])


def _norm_proj(x2, g, w, tables, rope_slabs, seq, wg=None):
    n = x2.shape[0]
    cols = w.shape[1]
    pos_blocks = seq // ROW_TILE
    has_gate = wg is not None
    row = lambda i: (i, 0)
    tab = pl.BlockSpec((ROW_TILE, LANES), lambda i: (i % pos_blocks, 0))
    in_specs = [pl.BlockSpec((ROW_TILE, D_MODEL), row), _const_spec((1, D_MODEL)),
                _const_spec((D_MODEL, cols)), tab, tab, tab]
    args = [x2, g.reshape(1, D_MODEL), w, *tables]
    out_shape = [jax.ShapeDtypeStruct((n, cols), BF16)]
    out_specs = [pl.BlockSpec((ROW_TILE, cols), row)]
    if has_gate:
        in_specs.append(_const_spec((D_MODEL, LANES)))
        args.append(wg)
        side_cols = (len(NSA_F32_SLABS) + 1) * LANES
        out_shape.append(jax.ShapeDtypeStruct((n, side_cols), F32))
        out_specs.append(pl.BlockSpec((ROW_TILE, side_cols), row))
    outs = pl.pallas_call(
        functools.partial(_proj_kernel, rope_slabs=rope_slabs, has_gate=has_gate),
        grid=(n // ROW_TILE,),
        in_specs=in_specs,
        out_specs=out_specs,
        out_shape=out_shape,
        compiler_params=pltpu.CompilerParams(dimension_semantics=("arbitrary",),
                                             vmem_limit_bytes=VMEM_LIMIT),
        name="norm_proj",
    )(*args)
    return outs if has_gate else outs[0]


def _tail_kernel(o_ref, x_ref, wout_ref, gpost_ref, gpre_ref, wup_ref, wdown_ref, gmlp_ref, out_ref):
    m = _nn(o_ref[...], wout_ref[...])
    x1 = x_ref[...] + _rms(m, gpost_ref[...])
    h = _rms(x1, gpre_ref[...]).astype(BF16)
    acc = jnp.zeros((ROW_TILE, D_MODEL), F32)
    for j in range(D_FF // FF_CHUNK):
        u = _nn(h, wup_ref[:, j * FF_CHUNK:(j + 1) * FF_CHUNK].astype(BF16))
        a = jnp.square(jnp.maximum(u, 0.0)).astype(BF16)
        acc = acc + _nn(a, wdown_ref[j * FF_CHUNK:(j + 1) * FF_CHUNK, :].astype(BF16))
    out_ref[...] = x1 + _rms(acc, gmlp_ref[...])


def _layer_tail(o2, x2, wout, gpost, gpre, wup, wdown, gmlp):
    n = x2.shape[0]
    row = lambda i: (i, 0)
    vec = lambda v: v.reshape(1, D_MODEL)
    return pl.pallas_call(
        _tail_kernel,
        grid=(n // ROW_TILE,),
        in_specs=[pl.BlockSpec((ROW_TILE, N_HEADS * HEAD_DIM), row),
                  pl.BlockSpec((ROW_TILE, D_MODEL), row),
                  _const_spec((N_HEADS * HEAD_DIM, D_MODEL)), _const_spec((1, D_MODEL)),
                  _const_spec((1, D_MODEL)), _const_spec((D_MODEL, D_FF)),
                  _const_spec((D_FF, D_MODEL)), _const_spec((1, D_MODEL))],
        out_specs=pl.BlockSpec((ROW_TILE, D_MODEL), row),
        out_shape=jax.ShapeDtypeStruct((n, D_MODEL), F32),
        compiler_params=pltpu.CompilerParams(dimension_semantics=("arbitrary",),
                                             vmem_limit_bytes=VMEM_LIMIT),
        name="layer_tail",
    )(o2, x2, wout, vec(gpost), vec(gpre), wup, wdown, vec(gmlp))


def _pair_queries(q):
    lane = lax.broadcasted_iota(jnp.int32, q.shape, 1)
    zero = jnp.zeros_like(q)
    return jnp.concatenate([jnp.where(lane < HEAD_DIM, q, zero),
                            jnp.where(lane >= HEAD_DIM, q, zero)], axis=0)


def _transpose_value_tiles(v_ref, vt_ref, n_tiles):
    for n in range(n_tiles):
        blk = v_ref[n * KEY_TILE:(n + 1) * KEY_TILE, :].astype(F32)
        vt_ref[n] = blk.T.astype(BF16)


def _fold_rows(x, op):
    r, w = x.shape
    return op(x.reshape(r // SUBLANES, SUBLANES, w), axis=0)


def _pair_pv(vt, p):
    t = p.shape[1] // 2
    return _nn(vt[:HEAD_DIM, :], p[:, :t]), _nn(vt[HEAD_DIM:, :], p[:, t:])


def _pair_output(acc_a, acc_b, inv_l=None):
    if inv_l is not None:
        t = acc_a.shape[1]
        acc_a = acc_a * inv_l[:, :t]
        acc_b = acc_b * inv_l[:, t:]
    return jnp.concatenate([acc_a, acc_b], axis=0).T


def _moba_kernel(q_ref, k_ref, v_ref, o_ref, kmean_ref, vt_ref, s_ref, *, n_blocks):
    for n in range(n_blocks):
        blk = k_ref[n * MOBA_BLOCK:(n + 1) * MOBA_BLOCK, :].astype(F32)
        kmean_ref[n:n + 1, :] = jnp.mean(blk, axis=0, keepdims=True)
    _transpose_value_tiles(v_ref, vt_ref, n_blocks)
    km_hi, km_lo = _split_bf16(kmean_ref[...])
    kpos = lax.broadcasted_iota(jnp.int32, (KEY_TILE, Q_TILE), 0)
    tpos = lax.broadcasted_iota(jnp.int32, (KEY_TILE, Q_TILE), 1)
    causal = kpos <= tpos

    def begin(i):
        qcat = _pair_queries(q_ref[i * Q_TILE:(i + 1) * Q_TILE, :])
        chosen = [None] * i
        if i > MOBA_TOPK:
            gate = _nt(km_hi, qcat) + _nt(km_lo, qcat)
            blk_id = lax.broadcasted_iota(jnp.int32, gate.shape, 0)
            g = jnp.where(blk_id < i, gate, NEG_INF)
            sel = jnp.zeros(gate.shape, F32)
            for _ in range(MOBA_TOPK):
                mx = jnp.max(g, axis=0, keepdims=True)
                first = jnp.min(jnp.where(g == mx, blk_id, n_blocks), axis=0, keepdims=True)
                hit = blk_id == first
                sel = jnp.where(hit, 1.0, sel)
                g = jnp.where(hit, -jnp.inf, g)
            chosen = [sel[n:n + 1, :] > 0.5 for n in range(i)]
        heads = [dict(m8=None, l8=jnp.zeros((SUBLANES, Q_TILE), F32), acc=jnp.zeros((HEAD_DIM, Q_TILE), F32))
                 for _ in range(2)]
        return dict(i=i, qcat=qcat, chosen=chosen, heads=heads)

    def score_tile(st, n, h):
        i = st["i"]
        hd = st["heads"][h]
        lanes = slice(h * Q_TILE, (h + 1) * Q_TILE)
        s = _nt(k_ref[n * KEY_TILE:(n + 1) * KEY_TILE, :], st["qcat"][lanes, :])
        if n == i:
            s = jnp.where(causal, s, NEG_INF)
        s_ref[i % 2, n, :, lanes] = s
        bm = _fold_rows(s, jnp.max)
        if n < i and st["chosen"][n] is not None:
            bm = jnp.where(st["chosen"][n][:, lanes], bm, NEG_INF)
        hd["m8"] = bm if hd["m8"] is None else jnp.maximum(hd["m8"], bm)

    def weight_tile(st, n, h):
        i = st["i"]
        hd = st["heads"][h]
        lanes = slice(h * Q_TILE, (h + 1) * Q_TILE)
        keep = None if (n == i or st["chosen"][n] is None) else st["chosen"][n][:, lanes]
        shift = hd["m"] if keep is None else jnp.where(keep, hd["m"], POS_BIG)
        p = jnp.exp2(s_ref[i % 2, n, :, lanes] - shift)
        hd["l8"] = hd["l8"] + _fold_rows(p, jnp.sum)
        hd["acc"] = hd["acc"] + _nn(vt_ref[n, h * HEAD_DIM:(h + 1) * HEAD_DIM, :], p.astype(BF16))

    cur = begin(0)
    for h in range(2):
        score_tile(cur, 0, h)
    for i in range(n_blocks):
        for hd in cur["heads"]:
            hd["m"] = jnp.max(hd["m8"], axis=0, keepdims=True)
        nxt = begin(i + 1) if i + 1 < n_blocks else None
        for n in range(i + 2):
            for h in range(2):
                if nxt is not None:
                    score_tile(nxt, n, h)
                if n <= i:
                    weight_tile(cur, n, h)
        acc = [hd["acc"] * (1.0 / jnp.sum(hd["l8"], axis=0, keepdims=True)) for hd in cur["heads"]]
        o_ref[i * Q_TILE:(i + 1) * Q_TILE, :] = _pair_output(acc[0], acc[1]).astype(o_ref.dtype)
        cur = nxt


def _moba_attention(qkv, batch, seq):
    n = qkv.shape[0]
    n_blocks = seq // MOBA_BLOCK
    pairs = N_HEADS // 2
    return pl.pallas_call(
        functools.partial(_moba_kernel, n_blocks=n_blocks),
        grid=(batch, pairs),
        in_specs=[pl.BlockSpec((seq, LANES), lambda b, p: (b, p)),
                  pl.BlockSpec((seq, LANES), lambda b, p: (b, pairs + p)),
                  pl.BlockSpec((seq, LANES), lambda b, p: (b, 2 * pairs + p))],
        out_specs=pl.BlockSpec((seq, LANES), lambda b, p: (b, p)),
        out_shape=jax.ShapeDtypeStruct((n, N_HEADS * HEAD_DIM), BF16),
        scratch_shapes=[pltpu.VMEM((n_blocks, LANES), F32),
                        pltpu.VMEM((n_blocks, LANES, KEY_TILE), BF16),
                        pltpu.VMEM((2, n_blocks, KEY_TILE, 2 * Q_TILE), F32)],
        compiler_params=pltpu.CompilerParams(
            dimension_semantics=("arbitrary", "arbitrary"),
            vmem_limit_bytes=VMEM_LIMIT),
        name="moba_attention",
    )(qkv, qkv, qkv)


def _log2_sigmoid_pair(z):
    log_beta = jnp.minimum(z, 0.0) - jnp.log2(1.0 + jnp.exp2(-jnp.abs(z)))
    return log_beta, log_beta - z


def _sb_kernel(q_ref, k_ref, v_ref, o_ref, vt_ref, acc_ref, carry_ref, *, n_tiles):
    w2 = 2 * Q_TILE
    near = 3
    _transpose_value_tiles(v_ref, vt_ref, n_tiles)
    r = lax.broadcasted_iota(jnp.int32, (KEY_TILE, KEY_TILE), 0)
    c = lax.broadcasted_iota(jnp.int32, (KEY_TILE, KEY_TILE), 1)
    later = jnp.where(c > r, 1.0, 0.0).astype(BF16)
    kpos = lax.broadcasted_iota(jnp.int32, (KEY_TILE, w2), 0)
    tpos = lax.broadcasted_iota(jnp.int32, (KEY_TILE, w2), 1) % Q_TILE
    strict = kpos < tpos

    def scores(state, t):
        st = state[t["i"]]
        t["z"] = _nt(k_ref[t["n"] * KEY_TILE:(t["n"] + 1) * KEY_TILE, :], st["qcat"])

    def log_terms(state, t):
        st = state[t["i"]]
        log_beta, log_1m = _log2_sigmoid_pair(t.pop("z"))
        if t["n"] == t["i"]:
            log_1m = jnp.where(strict, log_1m, 0.0)
        t["carry"] = st["carry"]
        if t["n"] > 0:
            tile_sum = jnp.sum(log_1m, axis=0, keepdims=True)
            st["carry"] = tile_sum if st["carry"] is None else st["carry"] + tile_sum
        t["log_beta"] = log_beta
        t["suffix"] = _nn(later, log_1m.astype(BF16))

    def weights(state, t):
        st = state[t["i"]]
        expo = t.pop("log_beta") + t.pop("suffix")
        if t["carry"] is not None:
            expo = expo + t["carry"]
        wgt = jnp.exp2(expo)
        if t["n"] == t["i"]:
            wgt = jnp.where(strict, wgt, 0.0)
        pv_a, pv_b = _pair_pv(vt_ref[t["n"]], wgt.astype(BF16))
        st["acc_a"] = st["acc_a"] + pv_a
        st["acc_b"] = st["acc_b"] + pv_b

    def run(state, tiles):
        stages = (scores, log_terms, weights)
        for step in range(len(tiles) + len(stages) - 1):
            for lag, stage in enumerate(stages):
                if 0 <= step - lag < len(tiles):
                    stage(state, tiles[step - lag])

    def query_operand(i):
        return _pair_queries(q_ref[i * Q_TILE:(i + 1) * Q_TILE, :])

    def write_out(i, st):
        o_ref[i * Q_TILE:(i + 1) * Q_TILE, :] = _pair_output(st["acc_a"], st["acc_b"]).astype(o_ref.dtype)

    zeros = jnp.zeros((HEAD_DIM, Q_TILE), F32)
    state = {i: dict(qcat=query_operand(i), carry=None, acc_a=zeros, acc_b=zeros) for i in range(n_tiles)}
    run(state, [dict(i=i, n=n) for i in range(n_tiles) for n in range(i, max(i - near, -1), -1)])
    for i in range(n_tiles):
        write_out(i, state[i])
        if i >= near:
            acc_ref[i] = jnp.concatenate([state[i]["acc_a"], state[i]["acc_b"]], axis=0)
            carry_ref[i] = state[i]["carry"]

    live = {i: jnp.max(state[i]["carry"]) > SB_UNDERFLOW_LOG2 for i in range(near, n_tiles)}
    for i in range(near, n_tiles):
        @pl.when(live[i])
        def _(i=i):
            acc = acc_ref[i]
            far = {i: dict(qcat=query_operand(i), carry=carry_ref[i],
                           acc_a=acc[:HEAD_DIM, :], acc_b=acc[HEAD_DIM:, :])}
            run(far, [dict(i=i, n=n) for n in range(i - near, -1, -1)])
            write_out(i, far[i])


def _sb_attention(qkv, batch, seq):
    n = qkv.shape[0]
    n_tiles = seq // KEY_TILE
    pairs = N_HEADS // 2
    return pl.pallas_call(
        functools.partial(_sb_kernel, n_tiles=n_tiles),
        grid=(batch, pairs),
        in_specs=[pl.BlockSpec((seq, LANES), lambda b, p: (b, p)),
                  pl.BlockSpec((seq, LANES), lambda b, p: (b, pairs + p)),
                  pl.BlockSpec((seq, LANES), lambda b, p: (b, 2 * pairs + p))],
        out_specs=pl.BlockSpec((seq, LANES), lambda b, p: (b, p)),
        out_shape=jax.ShapeDtypeStruct((n, N_HEADS * HEAD_DIM), BF16),
        scratch_shapes=[pltpu.VMEM((n_tiles, LANES, KEY_TILE), BF16),
                        pltpu.VMEM((n_tiles, LANES, Q_TILE), F32),
                        pltpu.VMEM((n_tiles, 1, 2 * Q_TILE), F32)],
        compiler_params=pltpu.CompilerParams(
            dimension_semantics=("arbitrary", "arbitrary"),
            vmem_limit_bytes=VMEM_LIMIT),
        name="stick_breaking_attention",
    )(qkv, qkv, qkv)


def _nsa_compress_kernel(ak0_ref, ak1_ref, av0_ref, av1_ref, posk_ref, posv_ref, w1k_ref, w1v_ref,
                         w2k_ref, w2v_ref, kc_ref, vct_ref):
    half = NSA_CMP_STRIDE * HEAD_DIM
    n_rows = kc_ref.shape[1]

    def token_rows(a_ref):
        return [a_ref[pl.ds(l, n_rows, stride=NSA_CMP_STRIDE), :] for l in range(NSA_CMP_STRIDE)]

    def compress(slabs, g, pos_ref, w1_ref, w2_ref):
        lo = (g % 2) * HEAD_DIM
        a = jnp.concatenate([tok[:, lo:lo + HEAD_DIM] for tok in slabs[g // 2]], axis=1)
        first = (a + pos_ref[0:1, :]).astype(BF16)
        second = (a + pos_ref[1:2, :]).astype(BF16)
        pre = _nn(first, w1_ref[:half, :]) + pltpu.roll(_nn(second, w1_ref[half:, :]), n_rows - 1, 0)
        return _nn(jax.nn.gelu(pre).astype(BF16), w2_ref[...])

    ak = [token_rows(ak0_ref), token_rows(ak1_ref)]
    av = [token_rows(av0_ref), token_rows(av1_ref)]
    for g in range(NSA_KV_GROUPS):
        kc_ref[g] = compress(ak, g, posk_ref, w1k_ref, w2k_ref).astype(BF16)
        vct_ref[g] = compress(av, g, posv_ref, w1v_ref, w2v_ref).T[:HEAD_DIM, :].astype(BF16)


def _nsa_compress(side, batch, seq, posk, posv, w1k, w1v, w2k, w2v):
    rows = seq // NSA_CMP_STRIDE
    g = NSA_KV_GROUPS
    slab = lambda k: pl.BlockSpec((seq, LANES), lambda i: (i, k))
    return pl.pallas_call(
        _nsa_compress_kernel,
        grid=(batch,),
        in_specs=[slab(0), slab(1), slab(2), slab(3),
                  _const_spec(posk.shape), _const_spec(posv.shape),
                  _const_spec(w1k.shape), _const_spec(w1v.shape),
                  _const_spec(w2k.shape), _const_spec(w2v.shape)],
        out_specs=[pl.BlockSpec((g, rows, LANES), lambda i: (i, 0, 0)),
                   pl.BlockSpec((g, HEAD_DIM, rows), lambda i: (i, 0, 0))],
        out_shape=[jax.ShapeDtypeStruct((batch * g, rows, LANES), BF16),
                   jax.ShapeDtypeStruct((batch * g, HEAD_DIM, rows), BF16)],
        compiler_params=pltpu.CompilerParams(dimension_semantics=("arbitrary",),
                                             vmem_limit_bytes=VMEM_LIMIT),
        name="nsa_compress",
    )(side, side, side, side, posk, posv, w1k, w1v, w2k, w2v)


def _nsa_kernel(q_ref, kcmp_ref, vcmpt_ref, ks_ref, kw_ref, vsw_ref, gate_ref, ovl_ref, o_ref,
                vst_ref, vwt_ref, gt_ref, ssel_ref, swin_ref, *, n_tiles, n_sel):
    grp = pl.program_id(1)
    hg = NSA_HEADS_PER_GROUP
    t = Q_TILE
    wq = hg * t
    sub = KEY_TILE // NSA_SLC_BLOCK
    back = NSA_WINDOW // KEY_TILE
    n_top = min(NSA_SLC_TOPN, n_sel)

    for n in range(n_tiles):
        rows = slice(n * KEY_TILE, (n + 1) * KEY_TILE)
        blk = vsw_ref[rows, :].astype(F32).T.astype(BF16)
        for vt_ref, part in ((vst_ref, blk[:HEAD_DIM, :]), (vwt_ref, blk[HEAD_DIM:, :])):
            vt_ref[n, :HEAD_DIM, :] = part
            vt_ref[n, HEAD_DIM:, :] = jnp.ones((ONES_ROWS, KEY_TILE), BF16)
        gt_ref[:, rows] = jax.nn.sigmoid(gate_ref[rows, :]).T
    kloc = lax.broadcasted_iota(jnp.int32, (KEY_TILE, wq), 0)
    tloc = lax.broadcasted_iota(jnp.int32, (KEY_TILE, wq), 1) % t
    causal = kloc <= tloc
    window_tail = kloc > tloc

    def begin(i):
        rows = slice(i * t, (i + 1) * t)
        qcat = jnp.concatenate([_pair_queries(q_ref[rows, :LANES]),
                                _pair_queries(q_ref[rows, LANES:])], axis=0)
        tq = i * t + lax.broadcasted_iota(jnp.int32, (1, wq), 1) % t

        n_cmp_pad = kcmp_ref.shape[1]
        reach = min(n_cmp_pad, (i + 1) * t // NSA_CMP_STRIDE)
        s = _nt(kcmp_ref[0, :reach, :], qcat)
        cmp_end = lax.broadcasted_iota(jnp.int32, s.shape, 0) * NSA_CMP_STRIDE + (NSA_CMP_BLOCK - 1)
        valid = cmp_end <= tq
        m = jnp.max(jnp.where(valid, s, NEG_INF), axis=0, keepdims=True)
        e = jnp.where(valid, jnp.exp2(s - m), 0.0)
        l = jnp.sum(e, axis=0, keepdims=True)
        p_cmp = e * jnp.where(l > 0.0, 1.0 / l, 0.0)
        if reach < n_cmp_pad:
            p_cmp = jnp.concatenate([p_cmp, jnp.zeros((n_cmp_pad - reach, wq), F32)], axis=0)
        o_cmp = _nn(vcmpt_ref[0], p_cmp.astype(BF16))

        bias = None
        if sub * (i + 1) > n_top:
            p_grp = p_cmp[:, :t]
            for h in range(1, hg):
                p_grp = p_grp + p_cmp[:, h * t:(h + 1) * t]
            p_hi, p_lo = _split_bf16(p_grp)
            imp = _nn(ovl_ref[...], p_hi) + _nn(ovl_ref[...], p_lo)
            j = lax.broadcasted_iota(jnp.int32, imp.shape, 0)
            q_blk = (i * t + lax.broadcasted_iota(jnp.int32, imp.shape, 1)) // NSA_SLC_BLOCK
            forced = jnp.logical_or(j == 0, jnp.logical_or(j == q_blk, j == q_blk - 1))
            score = jnp.where(j <= q_blk, jnp.where(forced, POS_BIG, imp), NEG_INF)
            sel = jnp.zeros(imp.shape, F32)
            for _ in range(n_top):
                mx = jnp.max(score, axis=0, keepdims=True)
                first = jnp.min(jnp.where(score == mx, j, n_sel), axis=0, keepdims=True)
                hit = j == first
                sel = jnp.where(jnp.logical_and(hit, mx > 0.5 * NEG_INF), 1.0, sel)
                score = jnp.where(hit, -jnp.inf, score)
            bias = jnp.concatenate([jnp.where(sel > 0.5, 0.0, NEG_INF)] * hg, axis=1)
        zeros_acc = jnp.zeros((HEAD_DIM + ONES_ROWS, wq), F32)
        return dict(i=i, qcat=qcat, o_cmp=o_cmp, bias=bias,
                    sel=dict(m8=None, acc=zeros_acc), win=dict(m8=None, acc=zeros_acc))

    def note_max(br, s):
        bm = _fold_rows(s, jnp.max)
        br["m8"] = bm if br["m8"] is None else jnp.maximum(br["m8"], bm)

    def sel_scores(st, n):
        s = _nt(ks_ref[n * KEY_TILE:(n + 1) * KEY_TILE, :], st["qcat"])
        if st["bias"] is not None:
            s = jnp.concatenate(
                [s[jj * NSA_SLC_BLOCK:(jj + 1) * NSA_SLC_BLOCK, :]
                 + st["bias"][n * sub + jj:n * sub + jj + 1, :] for jj in range(sub)], axis=0)
        if n == st["i"]:
            s = jnp.where(causal, s, NEG_INF)
        ssel_ref[n] = s
        note_max(st["sel"], s)

    def win_scores(st, n):
        s = _nt(kw_ref[n * KEY_TILE:(n + 1) * KEY_TILE, :], st["qcat"])
        if n == st["i"]:
            s = jnp.where(causal, s, NEG_INF)
        elif n == st["i"] - back:
            s = jnp.where(window_tail, s, NEG_INF)
        swin_ref[st["i"] - n] = s
        note_max(st["win"], s)

    def weights(br, s_view, v_view):
        p = jnp.exp2(s_view[...] - br["m"])
        br["acc"] = br["acc"] + _nn(v_view[...], p.astype(BF16))

    def finish(st):
        i = st["i"]
        o_slc, o_win = [br["acc"][:HEAD_DIM, :] * (1.0 / br["acc"][HEAD_DIM:HEAD_DIM + 1, :])
                        for br in (st["sel"], st["win"])]
        gates = gt_ref[pl.ds(pl.multiple_of(grp * 16, 16), 16), i * t:(i + 1) * t]
        heads = []
        for h in range(hg):
            cols = slice(h * t, (h + 1) * t)
            heads.append(gates[3 * h:3 * h + 1, :] * st["o_cmp"][:, cols]
                         + gates[3 * h + 1:3 * h + 2, :] * o_slc[:, cols]
                         + gates[3 * h + 2:3 * h + 3, :] * o_win[:, cols])
        for pair in range(hg // 2):
            both = jnp.concatenate([heads[2 * pair], heads[2 * pair + 1]], axis=0)
            o_ref[i * t:(i + 1) * t, pair * LANES:(pair + 1) * LANES] = both.T.astype(o_ref.dtype)

    def interleave(first, second):
        for k in range(max(len(first), len(second))):
            for steps in (first, second):
                if k < len(steps):
                    steps[k]()

    def win_tiles(i):
        return [n for n in range(i, i - back - 1, -1) if n >= 0]

    cur = begin(0)
    sel_scores(cur, 0)
    for i in range(n_tiles):
        cur["sel"]["m"] = jnp.max(cur["sel"]["m8"], axis=0, keepdims=True)
        interleave([functools.partial(win_scores, cur, n) for n in win_tiles(i)],
                   [functools.partial(weights, cur["sel"], ssel_ref.at[n], vst_ref.at[n])
                    for n in range(i + 1)])
        cur["win"]["m"] = jnp.max(cur["win"]["m8"], axis=0, keepdims=True)
        nxt = begin(i + 1) if i + 1 < n_tiles else None
        interleave([functools.partial(sel_scores, nxt, n) for n in range(i + 2)] if nxt else [],
                   [functools.partial(weights, cur["win"], swin_ref.at[i - n], vwt_ref.at[n])
                    for n in win_tiles(i)])
        finish(cur)
        cur = nxt


def _nsa_attention(proj, gates, kcmp, vcmpt, overlap_t, batch, seq):
    n = proj.shape[0]
    n_tiles = seq // KEY_TILE
    n_sel = seq // NSA_SLC_BLOCK
    g = NSA_KV_GROUPS
    hg = NSA_HEADS_PER_GROUP
    n_cmp_pad = kcmp.shape[1]
    back = NSA_WINDOW // KEY_TILE
    ks0, kw0, vsw0 = 12, 16, 20
    return pl.pallas_call(
        functools.partial(_nsa_kernel, n_tiles=n_tiles, n_sel=n_sel),
        grid=(batch, g),
        in_specs=[pl.BlockSpec((seq, hg * HEAD_DIM), lambda b, k: (b, k)),
                  pl.BlockSpec((1, n_cmp_pad, LANES), lambda b, k: (b * g + k, 0, 0)),
                  pl.BlockSpec((1, HEAD_DIM, n_cmp_pad), lambda b, k: (b * g + k, 0, 0)),
                  pl.BlockSpec((seq, LANES), lambda b, k: (b, ks0 + k)),
                  pl.BlockSpec((seq, LANES), lambda b, k: (b, kw0 + k)),
                  pl.BlockSpec((seq, LANES), lambda b, k: (b, vsw0 + k)),
                  pl.BlockSpec((seq, LANES), lambda b, k: (b, len(NSA_F32_SLABS))),
                  _const_spec(overlap_t.shape)],
        out_specs=pl.BlockSpec((seq, hg * HEAD_DIM), lambda b, k: (b, k)),
        out_shape=jax.ShapeDtypeStruct((n, N_HEADS * HEAD_DIM), BF16),
        scratch_shapes=[pltpu.VMEM((n_tiles, HEAD_DIM + ONES_ROWS, KEY_TILE), BF16),
                        pltpu.VMEM((n_tiles, HEAD_DIM + ONES_ROWS, KEY_TILE), BF16),
                        pltpu.VMEM((LANES, seq), F32),
                        pltpu.VMEM((n_tiles, KEY_TILE, hg * Q_TILE), F32),
                        pltpu.VMEM((back + 1, KEY_TILE, hg * Q_TILE), F32)],
        compiler_params=pltpu.CompilerParams(
            dimension_semantics=("arbitrary", "arbitrary"),
            vmem_limit_bytes=VMEM_LIMIT),
        name="nsa_attention",
    )(proj, kcmp, vcmpt, proj, proj, proj, gates, overlap_t)


def _rope_tables(seq):
    inv_freq = 1.0 / (ROPE_THETA ** (jnp.arange(0, HEAD_DIM, 2, dtype=F32) / HEAD_DIM))
    ang = jnp.arange(seq, dtype=F32)[:, None] * inv_freq[None, :]
    reps = LANES // (HEAD_DIM // 2)
    cos = jnp.tile(jnp.cos(ang), (1, reps))
    sin = jnp.tile(jnp.sin(ang), (1, reps))
    first_half = (jnp.arange(LANES) % HEAD_DIM) < HEAD_DIM // 2
    return cos, jnp.where(first_half, -sin, 0.0), jnp.where(first_half, 0.0, sin)


def _nsa_weight_layout(w_in):
    qd = N_HEADS * HEAD_DIM
    kv = NSA_KV_COLS
    q, kc, vc, ks, vs, kw, vw, gt = jnp.split(
        w_in, np.cumsum([qd] + [kv] * 6).tolist(), axis=1)

    def dup(w):
        w = w.reshape(D_MODEL, NSA_KV_GROUPS, 1, HEAD_DIM)
        return jnp.broadcast_to(w, (D_MODEL, NSA_KV_GROUPS, 2, HEAD_DIM)).reshape(D_MODEL, -1)

    vsw = jnp.concatenate([vs.reshape(D_MODEL, NSA_KV_GROUPS, 1, HEAD_DIM),
                           vw.reshape(D_MODEL, NSA_KV_GROUPS, 1, HEAD_DIM)], axis=2)
    main = jnp.concatenate([q, kc, vc, dup(ks), dup(kw), vsw.reshape(D_MODEL, -1)], axis=1)
    per_group = 3 * NSA_HEADS_PER_GROUP
    gt = gt.reshape(D_MODEL, NSA_KV_GROUPS, per_group)
    gt = jnp.pad(gt, ((0, 0), (0, 0), (0, 16 - per_group))).reshape(D_MODEL, -1)
    gt = jnp.pad(gt, ((0, 0), (0, LANES - gt.shape[1])))
    return main.astype(BF16), gt.astype(BF16)


def _overlap_t(seq):
    n_cmp = (seq - NSA_CMP_BLOCK) // NSA_CMP_STRIDE + 1
    n_sel = seq // NSA_SLC_BLOCK
    c_start = np.arange(n_cmp) * NSA_CMP_STRIDE
    s_start = np.arange(n_sel) * NSA_SLC_BLOCK
    lo = np.maximum(c_start[:, None], s_start[None, :])
    hi = np.minimum(c_start[:, None] + NSA_CMP_BLOCK, s_start[None, :] + NSA_SLC_BLOCK)
    ov = np.clip(hi - lo, 0, None) / NSA_CMP_BLOCK
    ov = np.pad(ov, ((0, seq // NSA_CMP_STRIDE - n_cmp), (0, 0)))
    return jnp.asarray(ov.T, dtype=BF16)


_QK_ROPE = (True,) * 16 + (False,) * 8
_NO_ROPE = (False,) * 24
_NSA_ROPE = (True,) * 10 + (False,) * 2 + (True,) * 8 + (False,) * 4


def kernel(x, l0_ln_mix_pre, l0_w_in, l0_w_out, l0_ln_mix_post, l0_ln_mlp_pre, l0_w_up, l0_w_down, l0_ln_mlp_post, l1_ln_mix_pre, l1_w_in, l1_w_out, l1_ln_mix_post, l1_ln_mlp_pre, l1_w_up, l1_w_down, l1_ln_mlp_post, l2_ln_mix_pre, l2_w_in, l2_cmp_pos_k, l2_cmp_w1_k, l2_cmp_w2_k, l2_cmp_pos_v, l2_cmp_w1_v, l2_cmp_w2_v, l2_w_out, l2_ln_mix_post, l2_ln_mlp_pre, l2_w_up, l2_w_down, l2_ln_mlp_post, l3_ln_mix_pre, l3_w_in, l3_w_out, l3_ln_mix_post, l3_ln_mlp_pre, l3_w_up, l3_w_down, l3_ln_mlp_post):
    batch, seq, _ = x.shape
    assert seq % ROW_TILE == 0 and seq % KEY_TILE == 0 and Q_TILE == KEY_TILE == MOBA_BLOCK
    tables = _rope_tables(seq)
    x2 = x.reshape(batch * seq, D_MODEL)
    bf = lambda w: w.astype(BF16)

    def tail(o, x2, w_out, g_post, g_pre, w_up, w_down, g_mlp):
        return _layer_tail(o, x2, bf(w_out), g_post, g_pre, w_up, w_down, g_mlp)

    qkv = _norm_proj(x2, l0_ln_mix_pre, bf(l0_w_in), tables, _QK_ROPE, seq)
    x2 = tail(_moba_attention(qkv, batch, seq), x2, l0_w_out, l0_ln_mix_post, l0_ln_mlp_pre,
              l0_w_up, l0_w_down, l0_ln_mlp_post)

    qkv = _norm_proj(x2, l1_ln_mix_pre, bf(l1_w_in), tables, _NO_ROPE, seq)
    x2 = tail(_sb_attention(qkv, batch, seq), x2, l1_w_out, l1_ln_mix_post, l1_ln_mlp_pre,
              l1_w_up, l1_w_down, l1_ln_mlp_post)

    w_main, w_gate = _nsa_weight_layout(l2_w_in)
    proj, gates = _norm_proj(x2, l2_ln_mix_pre, w_main, tables, _NSA_ROPE, seq, wg=w_gate)
    qd = N_HEADS * HEAD_DIM
    rows = seq // NSA_CMP_STRIDE

    flat_pos = lambda p: p.reshape(2, NSA_CMP_STRIDE * HEAD_DIM)
    w2_dup = lambda w: bf(jnp.concatenate([w, w], axis=1))
    kcmp, vcmpt = _nsa_compress(
        gates, batch, seq,
        flat_pos(l2_cmp_pos_k), flat_pos(l2_cmp_pos_v), bf(l2_cmp_w1_k), bf(l2_cmp_w1_v),
        w2_dup(l2_cmp_w2_k), w2_dup(l2_cmp_w2_v))
    o = _nsa_attention(proj, gates, kcmp, vcmpt, _overlap_t(seq), batch, seq)
    x2 = tail(o, x2, l2_w_out, l2_ln_mix_post, l2_ln_mlp_pre, l2_w_up, l2_w_down, l2_ln_mlp_post)

    qkv = _norm_proj(x2, l3_ln_mix_pre, bf(l3_w_in), tables, _QK_ROPE, seq)
    x2 = tail(_moba_attention(qkv, batch, seq), x2, l3_w_out, l3_ln_mix_post, l3_ln_mlp_pre,
              l3_w_up, l3_w_down, l3_ln_mlp_post)
    return x2.reshape(batch, seq, D_MODEL)
```
